```python
import jax, jax.numpy as jnp
from jax import lax
import numpy as np

D_MODEL = 1024
BATCH = 8
SEQ = 2048
DEPTH = 2
DEC_BATCH = 128
DEC_SEQ = 4
PAST_LEN = 16384
PAGE_SIZE = 128

H_A = 4
H_B = 4
DK = 128
DV = 128
D_MIX = (H_A + H_B) * DV
CONV_W = 4
D_CONV = 3 * H_A * DK
D_FF = -(-8 * D_MODEL // (3 * 256)) * 256
D_PLE = 256
CHUNK = 64
EPS = 1e-6
NEG = -1e30
SPLIT_SIZES = (D_CONV, H_A * DV, H_A, H_A, H_B * DK, H_B * DK, H_B * DV, H_B * DV, H_B, H_B)
N_IN = D_CONV + H_A * DV + 2 * H_A + 2 * H_B * DK + 2 * H_B * DV + 2 * H_B

kernel_name = "hymba_gdn_mlstm_decoder_step"


def rmsnorm(x, g):
    xf = x.astype(jnp.float32)
    y = xf * lax.rsqrt(jnp.mean(xf * xf, axis=-1, keepdims=True) + EPS) * g.astype(jnp.float32)
    return y.astype(x.dtype)


def l2norm(x):
    return x * lax.rsqrt(jnp.sum(x * x, axis=-1, keepdims=True) + EPS)


def split_proj(proj):
    outs, start = [], 0
    for s in SPLIT_SIZES:
        outs.append(proj[..., start:start + s])
        start += s
    return outs


def heads(u, h):
    B, T, _ = u.shape
    return u.reshape(B, T, h, -1).transpose(0, 2, 1, 3)


def causal_conv(u, buf, w):
    T = u.shape[1]
    xp = jnp.concatenate([buf, u], axis=1)
    y = sum(xp[:, i:i + T] * w[i] for i in range(CONV_W))
    return jax.nn.silu(y), xp[:, -(CONV_W - 1):]


def pad_chunk(a, pad, n, c, value=0.0):
    a = jnp.pad(a, [(0, 0), (0, 0), (0, pad)] + [(0, 0)] * (a.ndim - 3), constant_values=value)
    return a.reshape(a.shape[:2] + (n, c) + a.shape[3:])


def gated_delta_chunked(q, k, v, g, beta, S0):
    B, H, T, _ = q.shape
    c = min(CHUNK, T)
    n = -(-T // c)
    pad = n * c - T
    q, k, v, g, beta = [pad_chunk(a, pad, n, c) for a in (q, k, v, g, beta)]
    gc = jnp.cumsum(g, axis=-1)
    causal = jnp.tril(jnp.ones((c, c), bool))
    strict = jnp.tril(jnp.ones((c, c), bool), -1)
    diff = gc[..., :, None] - gc[..., None, :]
    decay = jnp.where(causal, jnp.exp(jnp.where(causal, diff, 0.0)), 0.0)
    kk = jnp.einsum('bhncd,bhnsd->bhncs', k, k)
    L = jnp.where(strict, beta[..., :, None] * kk * decay, 0.0)
    A = L + jnp.eye(c, dtype=L.dtype)
    rhs = jnp.concatenate([beta[..., None] * v, (beta * jnp.exp(gc))[..., None] * k], axis=-1)
    sol = lax.linalg.triangular_solve(A, rhs, left_side=True, lower=True, unit_diagonal=True)
    dv = v.shape[-1]
    wv, wk = sol[..., :dv], sol[..., dv:]
    qk = jnp.einsum('bhncd,bhnsd->bhncs', q, k) * decay
    gl = gc[..., -1]
    kd = k * jnp.exp(gl[..., None] - gc)[..., None]
    qg = q * jnp.exp(gc)[..., None]
    xs = [jnp.moveaxis(a, 2, 0) for a in (qg, qk, wv, wk, kd, jnp.exp(gl))]

    def step(S, inp):
        qg_c, qk_c, wv_c, wk_c, kd_c, el = inp
        U = wv_c - jnp.einsum('bhck,bhkv->bhcv', wk_c, S)
        o = jnp.einsum('bhck,bhkv->bhcv', qg_c, S) + jnp.einsum('bhcs,bhsv->bhcv', qk_c, U)
        S = el[..., None, None] * S + jnp.einsum('bhck,bhcv->bhkv', kd_c, U)
        return S, o

    S, o = lax.scan(step, S0, xs)
    o = jnp.moveaxis(o, 0, 2).reshape(B, H, n * c, dv)[:, :, :T]
    return o, S


def mlstm_chunked(q, k, v, ig, fl, C0, n0, m0):
    B, H, T, _ = q.shape
    c = min(CHUNK, T)
    n = -(-T // c)
    pad = n * c - T
    q, k, v, fl = [pad_chunk(a, pad, n, c) for a in (q, k, v, fl)]
    ig = pad_chunk(ig, pad, n, c, NEG)
    F = jnp.cumsum(fl, axis=-1)
    b = ig - F
    bmax = lax.cummax(b, axis=b.ndim - 1)
    causal = jnp.tril(jnp.ones((c, c), bool))
    Dlog = F[..., :, None] + b[..., None, :]
    qk = jnp.einsum('bhncd,bhnsd->bhncs', q, k)
    Fl = F[..., -1]
    xs = [jnp.moveaxis(a, 2, 0) for a in (q, k, v, F, b, bmax, Dlog, qk, Fl)]

    def step(carry, inp):
        C, nv, mp = carry
        q_c, k_c, v_c, F_c, b_c, bm_c, D_c, qk_c, Fl_c = inp
        m = F_c + jnp.maximum(mp[..., None], bm_c)
        a = jnp.exp(F_c + mp[..., None] - m)
        w = jnp.where(causal, jnp.exp(jnp.where(causal, D_c - m[..., :, None], 0.0)), 0.0) * qk_c
        num = a[..., None] * jnp.einsum('bhck,bhkv->bhcv', q_c, C) + jnp.einsum('bhcs,bhsv->bhcv', w, v_c)
        den = a * jnp.einsum('bhck,bhk->bhc', q_c, nv) + jnp.sum(w, axis=-1)
        h = num / jnp.maximum(jnp.abs(den), jnp.exp(-m))[..., None]
        ml = m[..., -1]
        al = jnp.exp(Fl_c + mp - ml)
        kw = k_c * jnp.exp(Fl_c[..., None] + b_c - ml[..., None])[..., None]
        C = al[..., None, None] * C + jnp.einsum('bhck,bhcv->bhkv', kw, v_c)
        nv = al[..., None] * nv + jnp.sum(kw, axis=-2)
        return (C, nv, ml), h

    (C, nv, m), h = lax.scan(step, (C0, n0, m0), xs)
    h = jnp.moveaxis(h, 0, 2).reshape(B, H, n * c, -1)[:, :, :T]
    return h, C, nv, m


def layer(x, p, S0, conv0, C0, n0, m0, w_in, conv_w, a_log, dt_bias, gdn_norm, i_bias, f_bias,
          mlstm_norm, w_out, norm_mix, norm_ffn, w_gate, w_up, w_down, norm_ple, w_ple_gate, w_ple_proj):
    f32 = jnp.float32
    B, T, _ = x.shape
    h = rmsnorm(x, norm_mix)
    proj = (h @ w_in).astype(f32)
    qkv_a, z_a, b_a, a_a, q_b, k_b, v_b, o_b, i_b, f_b = split_proj(proj)
    qkv_a, conv_new = causal_conv(qkv_a, conv0.astype(f32), conv_w.astype(f32))
    q_a = l2norm(heads(qkv_a[..., :H_A * DK], H_A)) * (DK ** -0.5)
    k_a = l2norm(heads(qkv_a[..., H_A * DK:2 * H_A * DK], H_A))
    v_a = heads(qkv_a[..., 2 * H_A * DK:], H_A)
    beta = jax.nn.sigmoid(b_a).transpose(0, 2, 1)
    g = (-jnp.exp(a_log.astype(f32)) * jax.nn.softplus(a_a + dt_bias.astype(f32))).transpose(0, 2, 1)
    o_a, S_new = gated_delta_chunked(q_a, k_a, v_a, g, beta, S0.astype(f32))
    o_a = rmsnorm(o_a.transpose(0, 2, 1, 3), gdn_norm) * jax.nn.silu(z_a.reshape(B, T, H_A, DV))
    qm = heads(q_b, H_B)
    km = heads(k_b, H_B) * (DK ** -0.5)
    vm = heads(v_b, H_B)
    ig = (i_b + i_bias.astype(f32)).transpose(0, 2, 1)
    fl = jax.nn.log_sigmoid(f_b + f_bias.astype(f32)).transpose(0, 2, 1)
    h_b, C_new, n_new, m_new = mlstm_chunked(qm, km, vm, ig, fl, C0.astype(f32), n0.astype(f32), m0.astype(f32))
    h_b = h_b.transpose(0, 2, 1, 3) * jax.nn.sigmoid(o_b.reshape(B, T, H_B, DV))
    h_b = rmsnorm(h_b, mlstm_norm.reshape(H_B, DV))
    mix = jnp.concatenate([o_a.reshape(B, T, -1), h_b.reshape(B, T, -1)], axis=-1).astype(x.dtype)
    x = x + mix @ w_out
    u = rmsnorm(x, norm_ffn)
    x = x + (jax.nn.silu(u @ w_gate) * (u @ w_up)) @ w_down
    x = x + (p @ w_ple_proj) * jax.nn.sigmoid(rmsnorm(x, norm_ple) @ w_ple_gate)
    return x, (S_new, conv_new, C_new, n_new, m_new)


def setup_inputs(seed: int = 0) -> dict:
    key = jax.random.key(seed)
    ks = jax.random.split(key, 32)
    nrm = jax.random.normal
    f32 = jnp.float32
    dt = jnp.exp(jax.random.uniform(ks[13], (DEPTH, H_A), f32, np.log(1e-3), np.log(1e-1)))
    return {
        "x_prompt": nrm(ks[0], (BATCH, SEQ, D_MODEL), f32),
        "x_sample": nrm(ks[1], (DEC_BATCH, DEC_SEQ, D_MODEL), f32),
        "p_prompt": nrm(ks[2], (DEPTH, BATCH, SEQ, D_PLE), f32),
        "p_sample": nrm(ks[3], (DEPTH, DEC_BATCH, DEC_SEQ, D_PLE), f32),
        "state_gdn": 0.5 * nrm(ks[4], (DEPTH, DEC_BATCH, H_A, DK, DV), f32),
        "state_gdn_conv": nrm(ks[5], (DEPTH, DEC_BATCH, CONV_W - 1, D_CONV), f32),
        "state_mlstm_C": 0.5 * nrm(ks[6], (DEPTH, DEC_BATCH, H_B, DK, DV), f32),
        "state_mlstm_n": 0.5 * nrm(ks[7], (DEPTH, DEC_BATCH, H_B, DK), f32),
        "state_mlstm_m": nrm(ks[8], (DEPTH, DEC_BATCH, H_B), f32),
        "w_in": nrm(ks[9], (DEPTH, D_MODEL, N_IN), f32) * D_MODEL ** -0.5,
        "conv_w": nrm(ks[10], (DEPTH, CONV_W, D_CONV), f32) * CONV_W ** -0.5,
        "gdn_a_log": jnp.log(jax.random.uniform(ks[11], (DEPTH, H_A), f32, 1.0, 16.0)),
        "gdn_dt_bias": dt + jnp.log(-jnp.expm1(-dt)),
        "gdn_norm": 1.0 + 0.02 * nrm(ks[12], (DEPTH, DV), f32),
        "mlstm_i_bias": 0.1 * nrm(ks[14], (DEPTH, H_B), f32),
        "mlstm_f_bias": jnp.tile(jnp.linspace(3.0, 6.0, H_B, dtype=f32), (DEPTH, 1)) + 0.1 * nrm(ks[15], (DEPTH, H_B), f32),
        "mlstm_norm": 1.0 + 0.02 * nrm(ks[16], (DEPTH, H_B * DV), f32),
        "w_out": nrm(ks[17], (DEPTH, D_MIX, D_MODEL), f32) * D_MIX ** -0.5,
        "norm_mix": 1.0 + 0.02 * nrm(ks[18], (DEPTH, D_MODEL), f32),
        "norm_ffn": 1.0 + 0.02 * nrm(ks[19], (DEPTH, D_MODEL), f32),
        "w_gate": nrm(ks[20], (DEPTH, D_MODEL, D_FF), f32) * D_MODEL ** -0.5,
        "w_up": nrm(ks[21], (DEPTH, D_MODEL, D_FF), f32) * D_MODEL ** -0.5,
        "w_down": nrm(ks[22], (DEPTH, D_FF, D_MODEL), f32) * D_FF ** -0.5,
        "norm_ple": 1.0 + 0.02 * nrm(ks[23], (DEPTH, D_MODEL), f32),
        "w_ple_gate": nrm(ks[24], (DEPTH, D_MODEL, D_MODEL), f32) * D_MODEL ** -0.5,
        "w_ple_proj": nrm(ks[25], (DEPTH, D_PLE, D_MODEL), f32) * D_PLE ** -0.5,
        "norm_final": 1.0 + 0.02 * nrm(ks[26], (D_MODEL,), f32),
    }


def reference(x_prompt, x_sample, p_prompt, p_sample, state_gdn, state_gdn_conv, state_mlstm_C,
              state_mlstm_n, state_mlstm_m, w_in, conv_w, gdn_a_log, gdn_dt_bias, gdn_norm,
              mlstm_i_bias, mlstm_f_bias, mlstm_norm, w_out, norm_mix, norm_ffn, w_gate, w_up,
              w_down, norm_ple, w_ple_gate, w_ple_proj, norm_final):
    f32 = jnp.float32
    B = x_prompt.shape[0]
    S0_p = jnp.zeros((B, H_A, DK, DV), f32)
    conv0_p = jnp.zeros((B, CONV_W - 1, D_CONV), f32)
    C0_p = jnp.zeros((B, H_B, DK, DV), f32)
    n0_p = jnp.zeros((B, H_B, DK), f32)
    m0_p = jnp.zeros((B, H_B), f32)
    xp, xs = x_prompt, x_sample
    st_p = [[] for _ in range(5)]
    st_s = [[] for _ in range(5)]
    for i in range(DEPTH):
        w = (w_in[i], conv_w[i], gdn_a_log[i], gdn_dt_bias[i], gdn_norm[i], mlstm_i_bias[i],
             mlstm_f_bias[i], mlstm_norm[i], w_out[i], norm_mix[i], norm_ffn[i], w_gate[i], w_up[i],
             w_down[i], norm_ple[i], w_ple_gate[i], w_ple_proj[i])
        xp, sp = layer(xp, p_prompt[i], S0_p, conv0_p, C0_p, n0_p, m0_p, *w)
        xs, ss = layer(xs, p_sample[i], state_gdn[i], state_gdn_conv[i], state_mlstm_C[i],
                       state_mlstm_n[i], state_mlstm_m[i], *w)
        for j in range(5):
            st_p[j].append(sp[j])
            st_s[j].append(ss[j])
    y_prompt = rmsnorm(xp, norm_final)
    y_sample = rmsnorm(xs, norm_final)
    gdn_S_p, gdn_conv_p, mC_p, mn_p, mm_p = [jnp.stack(a, axis=0) for a in st_p]
    gdn_S_s, gdn_conv_s, mC_s, mn_s, mm_s = [jnp.stack(a, axis=0) for a in st_s]
    return (y_prompt, y_sample, gdn_S_p, gdn_conv_p, mC_p, mn_p, mm_p, gdn_S_s, gdn_conv_s, mC_s, mn_s, mm_s)
```

```python
import functools

import jax
import jax.numpy as jnp
from jax import lax
from jax.experimental import pallas as pl
from jax.experimental.pallas import tpu as pltpu

F32 = jnp.float32
BF16 = jnp.bfloat16

D_MODEL = 1024
N_HEADS = 4
DH = 128
D_CONV = 3 * N_HEADS * DH
D_FF = 2816
D_PLE = 256
CONV_W = 4
CHUNK = 64
EPS = 1e-6
NEG = -1e30
N_GATE = 16
LANES = 128
N_PROJ = D_CONV + 5 * N_HEADS * DH + LANES
GATE_COL_BLOCK = (N_PROJ - LANES) // LANES
SLOT = 8
SLOT_PAD = 4
V7X_VMEM_LIMIT_BYTES = 56 * 1024 * 1024
ROW_TILE = 512
FF_CHUNK = 256


def _cparams(n_axes):
    return pltpu.CompilerParams(dimension_semantics=("arbitrary",) * n_axes,
                                vmem_limit_bytes=V7X_VMEM_LIMIT_BYTES)


def _rms(x, g):
    return x * lax.rsqrt(jnp.mean(x * x, axis=-1, keepdims=True) + EPS) * g


def _softplus(x):
    return jnp.maximum(x, 0.0) + jnp.log1p(jnp.exp(-jnp.abs(x)))


def _dot(a, b):
    return jnp.dot(a, b, preferred_element_type=F32)


def _dot_nt(a, b):
    return lax.dot_general(a, b, (((1,), (1,)), ((), ())), preferred_element_type=F32)


def _split(a):
    hi = a.astype(BF16)
    lo = (a - hi.astype(F32)).astype(BF16)
    return hi, lo


def _mm3(a, b):
    ah, al = _split(a)
    bh, bl = _split(b)
    return _dot(ah, bh) + (_dot(ah, bl) + _dot(al, bh))


def _unit_lower_inverse(L, ri, ci, seq_len):
    eye = jnp.where(ri == ci, 1.0, 0.0)
    n0 = jnp.where((ri >> 3) == (ci >> 3), L, 0.0)
    n2 = _mm3(n0, n0)
    n4 = _mm3(n2, n2)
    d = _mm3(_mm3(eye - n0, eye + n2), eye + n4)
    s = 8
    while s < seq_len:
        sh = s.bit_length() - 1
        e = jnp.where(((ri >> (sh + 1)) == (ci >> (sh + 1))) & ((ri >> sh) != (ci >> sh)), L, 0.0)
        d = d - _mm3(_mm3(d, e), d)
        s *= 2
    return d


def _in_proj_kernel(x_ref, g_ref, w_ref, wgt_ref, proj_ref, gt_ref):
    hb = _rms(x_ref[...], g_ref[...]).astype(BF16)
    proj_ref[...] = _dot(hb, w_ref[...])
    gt_ref[...] = _dot_nt(wgt_ref[...], hb)


def _in_proj(x, g, w, wgt):
    rows = x.shape[0]
    tm = ROW_TILE
    return pl.pallas_call(
        _in_proj_kernel,
        grid=(rows // tm,),
        in_specs=[
            pl.BlockSpec((tm, D_MODEL), lambda i: (i, 0)),
            pl.BlockSpec((1, D_MODEL), lambda i: (0, 0)),
            pl.BlockSpec((D_MODEL, N_PROJ), lambda i: (0, 0)),
            pl.BlockSpec((N_GATE, D_MODEL), lambda i: (0, 0)),
        ],
        out_specs=[
            pl.BlockSpec((tm, N_PROJ), lambda i: (i, 0)),
            pl.BlockSpec((N_GATE, tm), lambda i: (0, i)),
        ],
        out_shape=[jax.ShapeDtypeStruct((rows, N_PROJ), F32),
                   jax.ShapeDtypeStruct((N_GATE, rows), F32)],
        compiler_params=_cparams(1),
        name="in_proj",
    )(x, g, w, wgt)


def _chunk_masks(seq_len):
    c = CHUNK
    ri = lax.broadcasted_iota(jnp.int32, (c, c), 0)
    ci = lax.broadcasted_iota(jnp.int32, (c, c), 1)
    if seq_len < c:
        sh = seq_len.bit_length() - 1
        same = (ri >> sh) == (ci >> sh)
    else:
        same = ri >= 0
    return ri, ci, same


def _gdn_prep_kernel(*refs, tb, seq_len, valid_lo, idt):
    has_cprev = valid_lo > 0
    if has_cprev:
        (u_ref, prev_ref, gcol_ref, grow_ref, cprev_ref, cw_ref, pa_ref, pl_ref, ra_ref, rl_ref,
         wkqg_ref, wv_ref, qk_ref, kdt_ref, el_ref) = refs
    else:
        (u_ref, prev_ref, gcol_ref, grow_ref, cw_ref, pa_ref, pl_ref, ra_ref, rl_ref,
         wkqg_ref, wv_ref, qk_ref, kdt_ref, el_ref) = refs
    c = CHUNK
    sh = seq_len.bit_length() - 1
    u = u_ref[0]
    rowpos = lax.broadcasted_iota(jnp.int32, (tb, 1), 0) & (seq_len - 1)
    if has_cprev:
        u = jnp.where((rowpos >= valid_lo - (CONV_W - 1)) & (rowpos < valid_lo), cprev_ref[0], u)
    prev = jnp.where(pl.program_id(1) == 0, 0.0, prev_ref[0])
    xp = jnp.concatenate([prev, u], axis=0)
    cw = cw_ref[...]
    y = None
    for i in range(CONV_W):
        s = CONV_W - 1 - i
        xs = u if s == 0 else pltpu.roll(xp, s, 0)[8:8 + tb]
        t = xs * cw[i:i + 1, :]
        y = t if y is None else y + t
    y = jax.nn.silu(y)

    lane = lax.broadcasted_iota(jnp.int32, (1, LANES), 1)
    xg = gcol_ref[0] + pa_ref[0:1, :]
    gcolv = jnp.where(lane < N_HEADS, jax.nn.sigmoid(xg), -jnp.exp(pl_ref[0:1, :]) * _softplus(xg))
    if valid_lo > 0:
        gcolv = jnp.where(rowpos >= valid_lo, gcolv, 0.0)

    ri, ci, same = _chunk_masks(seq_len)
    causal = (ri >= ci) & same
    strict = (ri > ci) & same
    upper = (ri <= ci) & same
    colpos = lax.broadcasted_iota(jnp.int32, (1, c), 1) & (seq_len - 1)
    r8 = lax.broadcasted_iota(jnp.int32, (8, c), 0)
    c8 = lax.broadcasted_iota(jnp.int32, (8, c), 1)

    for k in range(tb // c):
        r0 = k * c
        growv = -jnp.exp(rl_ref[:, 0:1]) * _softplus(grow_ref[k] + ra_ref[:, 0:1])
        if valid_lo > 0:
            growv = jnp.where(colpos >= valid_lo, growv, 0.0)
        for h in range(N_HEADS):
            hs = slice(DH * h, DH * (h + 1))
            qc = y[r0:r0 + c, DH * h:DH * (h + 1)]
            kc = y[r0:r0 + c, N_HEADS * DH + DH * h:N_HEADS * DH + DH * (h + 1)]
            vc = y[r0:r0 + c, 2 * N_HEADS * DH + DH * h:2 * N_HEADS * DH + DH * (h + 1)]
            qn = qc * lax.rsqrt(jnp.sum(qc * qc, axis=-1, keepdims=True) + EPS) * (DH ** -0.5)
            kn = kc * lax.rsqrt(jnp.sum(kc * kc, axis=-1, keepdims=True) + EPS)
            beta = gcolv[r0:r0 + c, h:h + 1]
            g_col = gcolv[r0:r0 + c, N_HEADS + h:N_HEADS + h + 1]
            g_row = growv[N_HEADS + h:N_HEADS + h + 1, :]
            gc_col = jnp.sum(jnp.where(causal, g_row, 0.0), axis=1, keepdims=True)
            gc_row = jnp.sum(jnp.where(upper, g_col, 0.0), axis=0, keepdims=True)
            gl_col = jnp.sum(jnp.where(same, g_row, 0.0), axis=1, keepdims=True)
            decay = jnp.where(causal, jnp.exp(jnp.where(causal, gc_col - gc_row, 0.0)), 0.0)
            qb = qn.astype(BF16)
            kb = kn.astype(BF16)
            kk = _dot_nt(kb, kb)
            L = jnp.where(strict, beta * kk * decay, 0.0)
            tinv = _unit_lower_inverse(L, ri, ci, seq_len)
            wv = _mm3(tinv, beta * vc)
            wk = _mm3(tinv, (beta * jnp.exp(gc_col)) * kn)
            qk = _dot_nt(qb, kb) * decay
            kd = kn * jnp.exp(gl_col - gc_col)
            qg = qn * jnp.exp(gc_col)
            el = jnp.exp(jnp.sum(jnp.where((c8 >> sh) == r8, g_row, 0.0), axis=1, keepdims=True))
            wkqg_ref[0, k, 0:c, hs] = wk.astype(idt)
            wkqg_ref[0, k, c:2 * c, hs] = qg.astype(idt)
            wv_ref[0, k, :, hs] = wv
            qk_ref[0, k, h] = qk.astype(idt)
            kdt_ref[0, k, hs, :] = kd.T.astype(idt)
            el_ref[0, k, :, hs] = jnp.broadcast_to(el, (8, DH))


def _gdn_prep(proj3, grow, cprev3, cw, pa, pl_, ra, rl, *, seq_len, valid_lo, idt, tb):
    nb, t, _ = proj3.shape
    c = CHUNK
    nt = t // tb
    kb = tb // c
    has_cprev = valid_lo > 0
    in_specs = [
        pl.BlockSpec((1, tb, D_CONV), lambda b, i: (b, i, 0)),
        pl.BlockSpec((1, 8, D_CONV), lambda b, i: (b, jnp.maximum(i * (tb // 8) - 1, 0), 0)),
        pl.BlockSpec((1, tb, LANES), lambda b, i: (b, i, GATE_COL_BLOCK)),
        pl.BlockSpec((kb, N_GATE, c), lambda b, i: (b * nt + i, 0, 0)),
    ]
    args = [proj3, proj3, proj3, grow]
    if has_cprev:
        in_specs.append(pl.BlockSpec((1, tb, D_CONV), lambda b, i: (b, i, 0)))
        args.append(cprev3)
    in_specs += [
        pl.BlockSpec((CONV_W, D_CONV), lambda b, i: (0, 0)),
        pl.BlockSpec((8, LANES), lambda b, i: (0, 0)),
        pl.BlockSpec((8, LANES), lambda b, i: (0, 0)),
        pl.BlockSpec((N_GATE, LANES), lambda b, i: (0, 0)),
        pl.BlockSpec((N_GATE, LANES), lambda b, i: (0, 0)),
    ]
    args += [cw, pa, pl_, ra, rl]
    n = t // c
    hd = N_HEADS * DH
    out_shape = [
        jax.ShapeDtypeStruct((nb, n, 2 * c, hd), idt),
        jax.ShapeDtypeStruct((nb, n, c, hd), F32),
        jax.ShapeDtypeStruct((nb, n, N_HEADS, c, c), idt),
        jax.ShapeDtypeStruct((nb, n, hd, c), idt),
        jax.ShapeDtypeStruct((nb, n, 8, hd), F32),
    ]
    out_specs = [
        pl.BlockSpec((1, kb, 2 * c, hd), lambda b, i: (b, i, 0, 0)),
        pl.BlockSpec((1, kb, c, hd), lambda b, i: (b, i, 0, 0)),
        pl.BlockSpec((1, kb, N_HEADS, c, c), lambda b, i: (b, i, 0, 0, 0)),
        pl.BlockSpec((1, kb, hd, c), lambda b, i: (b, i, 0, 0)),
        pl.BlockSpec((1, kb, 8, hd), lambda b, i: (b, i, 0, 0)),
    ]
    return pl.pallas_call(
        functools.partial(_gdn_prep_kernel, tb=tb, seq_len=seq_len, valid_lo=valid_lo, idt=idt),
        grid=(nb, nt),
        in_specs=in_specs,
        out_specs=out_specs,
        out_shape=out_shape,
        compiler_params=_cparams(2),
        name="gdn_prep",
    )(*args)


def _gated_norm_store(o, z, gn, mix_ref, idx):
    mix_ref[idx] = (_rms(o, gn) * jax.nn.silu(z)).astype(BF16)


def _gdn_scan_prompt_kernel(wkqg_ref, wv_ref, qk_ref, kdt_ref, el_ref, z_ref, gn_ref,
                            mix_ref, sout_ref, s_scr, *, nc):
    c = CHUNK

    @pl.when(pl.program_id(1) == 0)
    def _():
        s_scr[...] = jnp.zeros_like(s_scr)

    gn = gn_ref[...]

    def body(n, carry):
        r0 = pl.multiple_of(n * c, c)
        for h in range(N_HEADS):
            hs = slice(DH * h, DH * (h + 1))
            s = s_scr[h]
            a = _dot(wkqg_ref[0, n, :, hs], s.astype(BF16))
            ub = (wv_ref[0, n, :, hs] - a[0:c]).astype(BF16)
            o = a[c:2 * c] + _dot(qk_ref[0, n, h], ub)
            s_scr[h] = el_ref[0, n, 0:1, hs] * s + _dot(kdt_ref[0, n, hs, :], ub)
            _gated_norm_store(o, z_ref[0, pl.ds(r0, c), hs], gn, mix_ref, (0, pl.ds(r0, c), hs))
        return carry

    lax.fori_loop(0, nc, body, 0)

    @pl.when(pl.program_id(1) == pl.num_programs(1) - 1)
    def _():
        sout_ref[0] = s_scr[...]


def _gdn_scan_prompt(wkqg, wv, qk, kdt, el, proj3, gn, *, ts):
    nb, n, _, hd = wkqg.shape
    c = CHUNK
    t = n * c
    nc = ts // c
    zblk = D_CONV // hd
    return pl.pallas_call(
        functools.partial(_gdn_scan_prompt_kernel, nc=nc),
        grid=(nb, t // ts),
        in_specs=[
            pl.BlockSpec((1, nc, 2 * c, hd), lambda b, i: (b, i, 0, 0)),
            pl.BlockSpec((1, nc, c, hd), lambda b, i: (b, i, 0, 0)),
            pl.BlockSpec((1, nc, N_HEADS, c, c), lambda b, i: (b, i, 0, 0, 0)),
            pl.BlockSpec((1, nc, hd, c), lambda b, i: (b, i, 0, 0)),
            pl.BlockSpec((1, nc, 8, hd), lambda b, i: (b, i, 0, 0)),
            pl.BlockSpec((1, ts, hd), lambda b, i: (b, i, zblk)),
            pl.BlockSpec((1, DH), lambda b, i: (0, 0)),
        ],
        out_specs=[
            pl.BlockSpec((1, ts, hd), lambda b, i: (b, i, 0)),
            pl.BlockSpec((1, N_HEADS, DH, DH), lambda b, i: (b, 0, 0, 0)),
        ],
        out_shape=[jax.ShapeDtypeStruct((nb, t, hd), BF16),
                   jax.ShapeDtypeStruct((nb, N_HEADS, DH, DH), F32)],
        scratch_shapes=[pltpu.VMEM((N_HEADS, DH, DH), F32)],
        compiler_params=_cparams(2),
        name="gdn_scan_prompt",
    )(wkqg, wv, qk, kdt, el, proj3, gn)


def _gdn_scan_decode_kernel(wkqg_ref, wv_ref, qk_ref, kdt_ref, el_ref, z_ref, gn_ref, s0_ref,
                            mix_ref, sout_ref):
    c = CHUNK
    nseq = c // SLOT
    rowseq = lax.broadcasted_iota(jnp.int32, (c, 1), 0) >> 3
    gn = gn_ref[...]
    for h in range(N_HEADS):
        hs = slice(DH * h, DH * (h + 1))
        w = wkqg_ref[0, 0, :, hs]
        a1, a2 = [], []
        for j in range(nseq):
            wj = jnp.concatenate([w[SLOT * j:SLOT * (j + 1)], w[c + SLOT * j:c + SLOT * (j + 1)]],
                                 axis=0).astype(BF16)
            aj = _dot(wj, s0_ref[j, h].astype(BF16))
            a1.append(aj[0:SLOT])
            a2.append(aj[SLOT:2 * SLOT])
        u = wv_ref[0, 0, :, hs] - jnp.concatenate(a1, axis=0)
        o = jnp.concatenate(a2, axis=0) + _dot(qk_ref[0, 0, h].astype(BF16), u.astype(BF16))
        kdt = kdt_ref[0, 0, hs, :].astype(BF16)
        for j in range(nseq):
            uj = jnp.where(rowseq == j, u, 0.0).astype(BF16)
            sout_ref[j, h] = el_ref[0, 0, j:j + 1, hs] * s0_ref[j, h] + _dot(kdt, uj)
        _gated_norm_store(o, z_ref[:, hs], gn, mix_ref, (slice(None), hs))


def _gdn_scan_decode(wkqg, wv, qk, kdt, el, proj, gn, s0):
    _, n, _, hd = wkqg.shape
    c = CHUNK
    nseq = c // SLOT
    zblk = D_CONV // hd
    return pl.pallas_call(
        _gdn_scan_decode_kernel,
        grid=(n,),
        in_specs=[
            pl.BlockSpec((1, 1, 2 * c, hd), lambda i: (0, i, 0, 0)),
            pl.BlockSpec((1, 1, c, hd), lambda i: (0, i, 0, 0)),
            pl.BlockSpec((1, 1, N_HEADS, c, c), lambda i: (0, i, 0, 0, 0)),
            pl.BlockSpec((1, 1, hd, c), lambda i: (0, i, 0, 0)),
            pl.BlockSpec((1, 1, 8, hd), lambda i: (0, i, 0, 0)),
            pl.BlockSpec((c, hd), lambda i: (i, zblk)),
            pl.BlockSpec((1, DH), lambda i: (0, 0)),
            pl.BlockSpec((nseq, N_HEADS, DH, DH), lambda i: (i, 0, 0, 0)),
        ],
        out_specs=[
            pl.BlockSpec((c, hd), lambda i: (i, 0)),
            pl.BlockSpec((nseq, N_HEADS, DH, DH), lambda i: (i, 0, 0, 0)),
        ],
        out_shape=[jax.ShapeDtypeStruct((n * c, hd), BF16),
                   jax.ShapeDtypeStruct(s0.shape, F32)],
        compiler_params=_cparams(1),
        name="gdn_scan_decode",
    )(wkqg, wv, qk, kdt, el, proj, gn, s0)


def _mlstm_gates(gcol, growk, pa_ref, ra_ref, h, rows, valid_lo, rowpos, colpos):
    xg = gcol + pa_ref[0:1, :]
    ig_col = xg[:, 2 * N_HEADS + h:2 * N_HEADS + h + 1]
    fl_col = -_softplus(-xg[:, 3 * N_HEADS + h:3 * N_HEADS + h + 1])
    xr = growk + ra_ref[:, 0:1]
    ig_row = xr[2 * N_HEADS + h:2 * N_HEADS + h + 1, :]
    fl_row = -_softplus(-xr[3 * N_HEADS + h:3 * N_HEADS + h + 1, :])
    if valid_lo > 0:
        ig_col = jnp.where(rowpos >= valid_lo, ig_col, NEG)
        fl_col = jnp.where(rowpos >= valid_lo, fl_col, 0.0)
        ig_row = jnp.where(colpos >= valid_lo, ig_row, NEG)
        fl_row = jnp.where(colpos >= valid_lo, fl_row, 0.0)
    return ig_col, fl_col, ig_row, fl_row


def _mlstm_chunk(q, k, v, ig_col, fl_col, ig_row, fl_row, mp_col, n_rows, c_list, seq_len):
    c = CHUNK
    nseq = c // seq_len
    ri, ci, same = _chunk_masks(seq_len)
    causal = (ri >= ci) & same
    upper = (ri <= ci) & same
    f_col = jnp.sum(jnp.where(causal, fl_row, 0.0), axis=1, keepdims=True)
    f_row = jnp.sum(jnp.where(upper, fl_col, 0.0), axis=0, keepdims=True)
    b_col = ig_col - f_col
    b_row = ig_row - f_row
    bmax_col = jnp.max(jnp.where(causal, b_row, -jnp.inf), axis=1, keepdims=True)
    km = k * (DH ** -0.5)
    qb = q.astype(BF16)
    kb = km.astype(BF16)
    vb = v.astype(BF16)
    qk = _dot_nt(qb, kb)
    m_col = f_col + jnp.maximum(mp_col, bmax_col)
    a_col = jnp.exp(f_col + mp_col - m_col)
    w = jnp.where(causal, jnp.exp(jnp.where(causal, (f_col + b_row) - m_col, 0.0)), 0.0) * qk
    if nseq == 1:
        qc = _dot(qb, c_list[0].astype(BF16))
    else:
        parts = []
        zpad = jnp.zeros((16 - seq_len, DH), F32)
        for j in range(nseq):
            qj = jnp.concatenate([q[seq_len * j:seq_len * (j + 1)], zpad], axis=0).astype(BF16)
            parts.append(_dot(qj, c_list[j].astype(BF16))[0:seq_len])
        qc = jnp.concatenate(parts, axis=0)
    num = a_col * qc + _dot(w.astype(BF16), vb)
    den = a_col * jnp.sum(q * n_rows, axis=1, keepdims=True) + jnp.sum(w, axis=1, keepdims=True)
    hout = num / jnp.maximum(jnp.abs(den), jnp.exp(-m_col))
    seq_end = ci == (ri | (seq_len - 1))
    fl_end = jnp.sum(jnp.where(seq_end, f_row, 0.0), axis=1, keepdims=True)
    bmax_end = jnp.max(jnp.where(same, b_row, -jnp.inf), axis=1, keepdims=True)
    ml_col = fl_end + jnp.maximum(mp_col, bmax_end)
    al_col = jnp.exp(fl_end + mp_col - ml_col)
    kw = km * jnp.exp(fl_end + b_col - ml_col)
    kwt = kw.T.astype(BF16)
    al = jnp.broadcast_to(al_col, (c, DH))
    if nseq == 1:
        new_c = [al[0:1] * c_list[0] + _dot(kwt, vb)]
    else:
        rowseq = lax.broadcasted_iota(jnp.int32, (c, 1), 0) >> (seq_len.bit_length() - 1)
        new_c = []
        for j in range(nseq):
            vj = jnp.where(rowseq == j, v, 0.0).astype(BF16)
            new_c.append(al[seq_len * j:seq_len * j + 1] * c_list[j] + _dot(kwt, vj))
    return hout, new_c, kw, al, ml_col


def _mlstm_out_store(hout, og, gnorm, mix_ref, idx):
    mix_ref[idx] = _rms(hout * jax.nn.sigmoid(og), gnorm).astype(BF16)


def _mlstm_prompt_kernel(q_ref, k_ref, v_ref, o_ref, gcol_ref, grow_ref, pa_ref, ra_ref, nrm_ref,
                         mix_ref, cout_ref, nout_ref, mout_ref, c_scr, n_scr, m_scr, *, nc):
    c = CHUNK

    @pl.when(pl.program_id(1) == 0)
    def _():
        c_scr[...] = jnp.zeros_like(c_scr)
        n_scr[...] = jnp.zeros_like(n_scr)
        m_scr[...] = jnp.zeros_like(m_scr)

    def body(n, carry):
        r0 = pl.multiple_of(n * c, c)
        gcol = gcol_ref[0, pl.ds(r0, c), :]
        growk = grow_ref[n]
        for h in range(N_HEADS):
            hs = slice(DH * h, DH * (h + 1))
            ig_col, fl_col, ig_row, fl_row = _mlstm_gates(gcol, growk, pa_ref, ra_ref, h, c, 0, None, None)
            mp_col = m_scr[h][:, 0:1]
            n_rows = n_scr[h]
            hout, new_c, kw, al, ml_col = _mlstm_chunk(
                q_ref[0, pl.ds(r0, c), hs], k_ref[0, pl.ds(r0, c), hs], v_ref[0, pl.ds(r0, c), hs],
                ig_col, fl_col, ig_row, fl_row, mp_col, n_rows, [c_scr[h]], c)
            c_scr[h] = new_c[0]
            n_scr[h] = jnp.broadcast_to(al[0:1] * n_rows[0:1] + jnp.sum(kw, axis=0, keepdims=True), (c, DH))
            m_scr[h] = jnp.broadcast_to(ml_col, (c, DH))
            _mlstm_out_store(hout, o_ref[0, pl.ds(r0, c), hs], nrm_ref[:, hs], mix_ref, (0, pl.ds(r0, c), hs))
        return carry

    lax.fori_loop(0, nc, body, 0)

    @pl.when(pl.program_id(1) == pl.num_programs(1) - 1)
    def _():
        lane = lax.broadcasted_iota(jnp.int32, (8, DH), 1)
        mo = jnp.zeros((8, DH), F32)
        for h in range(N_HEADS):
            cout_ref[0, h] = c_scr[h]
            nout_ref[0, h:h + 1, :] = n_scr[h][0:1]
            mo = jnp.where(lane == h, m_scr[h][0:8], mo)
        mout_ref[0] = mo


def _mlstm_prompt(proj3, grow, pa, ra, nrm, *, ts):
    nb, t, _ = proj3.shape
    c = CHUNK
    hd = N_HEADS * DH
    nc = ts // c
    nt = t // ts
    qblk = (D_CONV + hd) // hd
    return pl.pallas_call(
        functools.partial(_mlstm_prompt_kernel, nc=nc),
        grid=(nb, nt),
        in_specs=[
            pl.BlockSpec((1, ts, hd), lambda b, i: (b, i, qblk)),
            pl.BlockSpec((1, ts, hd), lambda b, i: (b, i, qblk + 1)),
            pl.BlockSpec((1, ts, hd), lambda b, i: (b, i, qblk + 2)),
            pl.BlockSpec((1, ts, hd), lambda b, i: (b, i, qblk + 3)),
            pl.BlockSpec((1, ts, LANES), lambda b, i: (b, i, GATE_COL_BLOCK)),
            pl.BlockSpec((nc, N_GATE, c), lambda b, i: (b * nt + i, 0, 0)),
            pl.BlockSpec((8, LANES), lambda b, i: (0, 0)),
            pl.BlockSpec((N_GATE, LANES), lambda b, i: (0, 0)),
            pl.BlockSpec((1, hd), lambda b, i: (0, 0)),
        ],
        out_specs=[
            pl.BlockSpec((1, ts, hd), lambda b, i: (b, i, 0)),
            pl.BlockSpec((1, N_HEADS, DH, DH), lambda b, i: (b, 0, 0, 0)),
            pl.BlockSpec((1, N_HEADS, DH), lambda b, i: (b, 0, 0)),
            pl.BlockSpec((1, 8, DH), lambda b, i: (b, 0, 0)),
        ],
        out_shape=[jax.ShapeDtypeStruct((nb, t, hd), BF16),
                   jax.ShapeDtypeStruct((nb, N_HEADS, DH, DH), F32),
                   jax.ShapeDtypeStruct((nb, N_HEADS, DH), F32),
                   jax.ShapeDtypeStruct((nb, 8, DH), F32)],
        scratch_shapes=[pltpu.VMEM((N_HEADS, DH, DH), F32),
                        pltpu.VMEM((N_HEADS, c, DH), F32),
                        pltpu.VMEM((N_HEADS, c, DH), F32)],
        compiler_params=_cparams(2),
        name="mlstm_prompt",
    )(proj3, proj3, proj3, proj3, proj3, grow, pa, ra, nrm)


def _mlstm_decode_kernel(q_ref, k_ref, v_ref, o_ref, gcol_ref, grow_ref, pa_ref, ra_ref, nrm_ref,
                         c0_ref, n0_ref, m0_ref, mix_ref, cout_ref, nout_ref, mout_ref):
    c = CHUNK
    nseq = c // SLOT
    rowpos = lax.broadcasted_iota(jnp.int32, (c, 1), 0) & (SLOT - 1)
    colpos = lax.broadcasted_iota(jnp.int32, (1, c), 1) & (SLOT - 1)
    lane = lax.broadcasted_iota(jnp.int32, (c, DH), 1)
    gcol = gcol_ref[...]
    growk = grow_ref[0]
    mo = jnp.zeros((c, DH), F32)
    for h in range(N_HEADS):
        hs = slice(DH * h, DH * (h + 1))
        ig_col, fl_col, ig_row, fl_row = _mlstm_gates(gcol, growk, pa_ref, ra_ref, h, c, SLOT_PAD, rowpos, colpos)
        mp_col = m0_ref[:, h:h + 1]
        n_rows = jnp.concatenate(
            [jnp.broadcast_to(n0_ref[j, h:h + 1, :], (SLOT, DH)) for j in range(nseq)], axis=0)
        hout, new_c, kw, al, ml_col = _mlstm_chunk(
            q_ref[:, hs], k_ref[:, hs], v_ref[:, hs], ig_col, fl_col, ig_row, fl_row, mp_col, n_rows,
            [c0_ref[j, h] for j in range(nseq)], SLOT)
        for j in range(nseq):
            cout_ref[j, h] = new_c[j]
            rs = slice(SLOT * j, SLOT * (j + 1))
            nout_ref[j, h:h + 1, :] = (al[SLOT * j:SLOT * j + 1] * n0_ref[j, h:h + 1, :]
                                       + jnp.sum(kw[rs], axis=0, keepdims=True))
        mo = jnp.where(lane == h, jnp.broadcast_to(ml_col, (c, DH)), mo)
        _mlstm_out_store(hout, o_ref[:, hs], nrm_ref[:, hs], mix_ref, (slice(None), hs))
    mout_ref[...] = mo


def _mlstm_decode(proj, grow, pa, ra, nrm, c0, n0, m0rows):
    rows = proj.shape[0]
    c = CHUNK
    hd = N_HEADS * DH
    nseq = c // SLOT
    qblk = (D_CONV + hd) // hd
    return pl.pallas_call(
        _mlstm_decode_kernel,
        grid=(rows // c,),
        in_specs=[
            pl.BlockSpec((c, hd), lambda i: (i, qblk)),
            pl.BlockSpec((c, hd), lambda i: (i, qblk + 1)),
            pl.BlockSpec((c, hd), lambda i: (i, qblk + 2)),
            pl.BlockSpec((c, hd), lambda i: (i, qblk + 3)),
            pl.BlockSpec((c, LANES), lambda i: (i, GATE_COL_BLOCK)),
            pl.BlockSpec((1, N_GATE, c), lambda i: (i, 0, 0)),
            pl.BlockSpec((8, LANES), lambda i: (0, 0)),
            pl.BlockSpec((N_GATE, LANES), lambda i: (0, 0)),
            pl.BlockSpec((1, hd), lambda i: (0, 0)),
            pl.BlockSpec((nseq, N_HEADS, DH, DH), lambda i: (i, 0, 0, 0)),
            pl.BlockSpec((nseq, N_HEADS, DH), lambda i: (i, 0, 0)),
            pl.BlockSpec((c, LANES), lambda i: (i, 0)),
        ],
        out_specs=[
            pl.BlockSpec((c, hd), lambda i: (i, 0)),
            pl.BlockSpec((nseq, N_HEADS, DH, DH), lambda i: (i, 0, 0, 0)),
            pl.BlockSpec((nseq, N_HEADS, DH), lambda i: (i, 0, 0)),
            pl.BlockSpec((c, LANES), lambda i: (i, 0)),
        ],
        out_shape=[jax.ShapeDtypeStruct((rows, hd), BF16),
                   jax.ShapeDtypeStruct(c0.shape, F32),
                   jax.ShapeDtypeStruct(n0.shape, F32),
                   jax.ShapeDtypeStruct((rows, LANES), F32)],
        compiler_params=_cparams(1),
        name="mlstm_decode",
    )(proj, proj, proj, proj, proj, grow, pa, ra, nrm, c0, n0, m0rows)


def _post_kernel(x_ref, ma_ref, mb_ref, p_ref, woa_ref, wob_ref, gf_ref, wg_ref, wu_ref, wd_ref,
                 gp_ref, wpg_ref, wpp_ref, gfin_ref, o_ref, acc_ref, *, final):
    x = x_ref[...] + (_dot(ma_ref[...], woa_ref[...]) + _dot(mb_ref[...], wob_ref[...]))
    ub = _rms(x, gf_ref[...]).astype(BF16)
    for j in range(D_FF // FF_CHUNK):
        sl = slice(j * FF_CHUNK, (j + 1) * FF_CHUNK)
        a = (jax.nn.silu(_dot(ub, wg_ref[:, sl])) * _dot(ub, wu_ref[:, sl])).astype(BF16)
        d = _dot(a, wd_ref[sl, :])
        if j == 0:
            acc_ref[...] = d
        else:
            acc_ref[...] += d
    x = x + acc_ref[...]
    gate = jax.nn.sigmoid(_dot(_rms(x, gp_ref[...]).astype(BF16), wpg_ref[...]))
    x = x + _dot(p_ref[...].astype(BF16), wpp_ref[...]) * gate
    if final:
        x = _rms(x, gfin_ref[...])
    o_ref[...] = x


def _post(x, ma, mb, p, woa, wob, gf, wg, wu, wd, gp, wpg, wpp, gfin, *, final):
    rows = x.shape[0]
    tm = ROW_TILE
    hd = N_HEADS * DH
    row = lambda w: pl.BlockSpec((tm, w), lambda i: (i, 0))
    whole = lambda a, b: pl.BlockSpec((a, b), lambda i: (0, 0), pipeline_mode=pl.Buffered(1))
    return pl.pallas_call(
        functools.partial(_post_kernel, final=final),
        grid=(rows // tm,),
        in_specs=[row(D_MODEL), row(hd), row(hd), row(D_PLE),
                  whole(hd, D_MODEL), whole(hd, D_MODEL), whole(1, D_MODEL),
                  whole(D_MODEL, D_FF), whole(D_MODEL, D_FF), whole(D_FF, D_MODEL),
                  whole(1, D_MODEL), whole(D_MODEL, D_MODEL), whole(D_PLE, D_MODEL), whole(1, D_MODEL)],
        out_specs=row(D_MODEL),
        out_shape=jax.ShapeDtypeStruct((rows, D_MODEL), F32),
        scratch_shapes=[pltpu.VMEM((tm, D_MODEL), F32)],
        compiler_params=_cparams(1),
        name="post",
    )(x, ma, mb, p, woa, wob, gf, wg, wu, wd, gp, wpg, wpp, gfin)


def _lane_vec(pairs, rows):
    v = jnp.zeros((LANES,), F32)
    for off, val in pairs:
        v = v.at[off:off + N_HEADS].set(val.astype(F32))
    return jnp.broadcast_to(v[None, :], (rows, LANES))


def _col_vec(pairs):
    v = jnp.zeros((N_GATE,), F32)
    for off, val in pairs:
        v = v.at[off:off + N_HEADS].set(val.astype(F32))
    return jnp.broadcast_to(v[:, None], (N_GATE, LANES))


def kernel(x_prompt, x_sample, p_prompt, p_sample, state_gdn, state_gdn_conv, state_mlstm_C, state_mlstm_n, state_mlstm_m, w_in, conv_w, gdn_a_log, gdn_dt_bias, gdn_norm, mlstm_i_bias, mlstm_f_bias, mlstm_norm, w_out, norm_mix, norm_ffn, w_gate, w_up, w_down, norm_ple, w_ple_gate, w_ple_proj, norm_final):
    depth = w_in.shape[0]
    nb, t, _ = x_prompt.shape
    ns, tdec, _ = x_sample.shape
    hd = N_HEADS * DH
    c = CHUNK
    pad = SLOT - tdec

    xp = x_prompt.reshape(nb * t, D_MODEL)
    xs = jnp.pad(x_sample, ((0, 0), (pad, 0), (0, 0))).reshape(ns * SLOT, D_MODEL)
    ps_all = jnp.pad(p_sample, ((0, 0), (0, 0), (pad, 0), (0, 0))).reshape(depth, ns * SLOT, D_PLE)
    pp_all = p_prompt.reshape(depth, nb * t, D_PLE)
    gfin = norm_final.reshape(1, D_MODEL)

    outs_p = [[] for _ in range(5)]
    outs_s = [[] for _ in range(5)]
    for i in range(depth):
        w = w_in[i]
        o = D_CONV + hd
        w_main = jnp.concatenate([w[:, :o], w[:, o + 2 * N_HEADS:o + 2 * N_HEADS + 4 * hd]], axis=1)
        w_g = jnp.concatenate([w[:, o:o + 2 * N_HEADS], w[:, o + 2 * N_HEADS + 4 * hd:]], axis=1)
        w_r = jnp.concatenate([w_main, w_g, jnp.zeros((D_MODEL, LANES - N_GATE), F32)], axis=1).astype(BF16)
        w_gt = w_g.T.astype(BF16)
        g_mix = norm_mix[i].reshape(1, D_MODEL)

        adds = [(N_HEADS, gdn_dt_bias[i]), (2 * N_HEADS, mlstm_i_bias[i]), (3 * N_HEADS, mlstm_f_bias[i])]
        pa = _lane_vec(adds, 8)
        pl_ = _lane_vec([(N_HEADS, gdn_a_log[i])], 8)
        ra = _col_vec(adds)
        rl = _col_vec([(N_HEADS, gdn_a_log[i])])
        cw = conv_w[i].astype(F32)
        gn = gdn_norm[i].reshape(1, DH).astype(F32)
        nrm = mlstm_norm[i].reshape(1, hd).astype(F32)

        proj_p, gt_p = _in_proj(xp, g_mix, w_r, w_gt)
        proj_s, gt_s = _in_proj(xs, g_mix, w_r, w_gt)
        proj_p3 = proj_p.reshape(nb, t, N_PROJ)
        proj_s3 = proj_s.reshape(1, ns * SLOT, N_PROJ)
        grow_p = gt_p.reshape(N_GATE, nb * t // c, c).transpose(1, 0, 2)
        grow_s = gt_s.reshape(N_GATE, ns * SLOT // c, c).transpose(1, 0, 2)

        prep_p = _gdn_prep(proj_p3, grow_p, None, cw, pa, pl_, ra, rl, seq_len=c, valid_lo=0, idt=BF16, tb=128)
        mixa_p, s_p = _gdn_scan_prompt(*prep_p, proj_p3, gn, ts=512)
        cprev = jnp.pad(state_gdn_conv[i].astype(F32), ((0, 0), (pad - (CONV_W - 1), SLOT - pad), (0, 0)))
        cprev3 = cprev.reshape(1, ns * SLOT, D_CONV)
        prep_s = _gdn_prep(proj_s3, grow_s, cprev3, cw, pa, pl_, ra, rl, seq_len=SLOT, valid_lo=pad, idt=F32, tb=c)
        mixa_s, s_s = _gdn_scan_decode(*prep_s, proj_s, gn, state_gdn[i].astype(F32))

        mixb_p, c_p, n_p, m_p = _mlstm_prompt(proj_p3, grow_p, pa, ra, nrm, ts=512)
        m0rows = jnp.pad(jnp.repeat(state_mlstm_m[i].astype(F32), SLOT, axis=0), ((0, 0), (0, LANES - N_HEADS)))
        mixb_s, c_s, n_s, m_s = _mlstm_decode(proj_s, grow_s, pa, ra, nrm, state_mlstm_C[i].astype(F32),
                                              state_mlstm_n[i].astype(F32), m0rows)

        wts = (w_out[i][:hd].astype(BF16), w_out[i][hd:].astype(BF16), norm_ffn[i].reshape(1, D_MODEL),
               w_gate[i].astype(BF16), w_up[i].astype(BF16), w_down[i].astype(BF16),
               norm_ple[i].reshape(1, D_MODEL), w_ple_gate[i].astype(BF16), w_ple_proj[i].astype(BF16), gfin)
        final = i == depth - 1
        xp = _post(xp, mixa_p.reshape(nb * t, hd), mixb_p.reshape(nb * t, hd), pp_all[i], *wts, final=final)
        xs = _post(xs, mixa_s, mixb_s, ps_all[i], *wts, final=final)

        outs_p[0].append(s_p)
        outs_p[1].append(proj_p3[:, t - (CONV_W - 1):, :D_CONV])
        outs_p[2].append(c_p)
        outs_p[3].append(n_p)
        outs_p[4].append(m_p[:, 0, :N_HEADS])
        outs_s[0].append(s_s)
        outs_s[1].append(proj_s.reshape(ns, SLOT, N_PROJ)[:, SLOT - (CONV_W - 1):, :D_CONV])
        outs_s[2].append(c_s)
        outs_s[3].append(n_s)
        outs_s[4].append(m_s.reshape(ns, SLOT, LANES)[:, SLOT - 1, :N_HEADS])

    y_prompt = xp.reshape(nb, t, D_MODEL)
    y_sample = xs.reshape(ns, SLOT, D_MODEL)[:, pad:, :]
    sp = [jnp.stack(a, axis=0) for a in outs_p]
    ss = [jnp.stack(a, axis=0) for a in outs_s]
    return (y_prompt, y_sample, *sp, *ss)
```

```python
import functools

import jax
import jax.numpy as jnp
from jax import lax
from jax.experimental import pallas as pl
from jax.experimental.pallas import tpu as pltpu

F32 = jnp.float32
BF16 = jnp.bfloat16

D_MODEL = 1024
N_HEADS = 4
DH = 128
D_CONV = 3 * N_HEADS * DH
D_FF = 2816
D_PLE = 256
CONV_W = 4
CHUNK = 64
EPS = 1e-6
NEG = -1e30
N_GATE = 16
LANES = 128
N_PROJ = D_CONV + 5 * N_HEADS * DH + LANES
GATE_COL_BLOCK = (N_PROJ - LANES) // LANES
SLOT = 8
SLOT_PAD = 4
V7X_VMEM_LIMIT_BYTES = 56 * 1024 * 1024
ROW_TILE = 512
FF_CHUNK = 256


def _cparams(n_axes):
    return pltpu.CompilerParams(dimension_semantics=("arbitrary",) * n_axes,
                                vmem_limit_bytes=V7X_VMEM_LIMIT_BYTES)


def _rms(x, g):
    return x * lax.rsqrt(jnp.mean(x * x, axis=-1, keepdims=True) + EPS) * g


def _softplus(x):
    return jnp.maximum(x, 0.0) + jnp.log1p(jnp.exp(-jnp.abs(x)))


def _dot(a, b):
    return jnp.dot(a, b, preferred_element_type=F32)


def _dot_nt(a, b):
    return lax.dot_general(a, b, (((1,), (1,)), ((), ())), preferred_element_type=F32)


def _split(a):
    hi = a.astype(BF16)
    lo = (a - hi.astype(F32)).astype(BF16)
    return hi, lo


def _mm3(a, b):
    ah, al = _split(a)
    bh, bl = _split(b)
    return _dot(ah, bh) + (_dot(ah, bl) + _dot(al, bh))


def _mm3_all(a_list, b_list):
    return [_mm3(a, b) for a, b in zip(a_list, b_list)]


def _unit_lower_inverse_all(l_list, ri, ci, seq_len):
    eye = jnp.where(ri == ci, 1.0, 0.0)
    n0 = [jnp.where((ri >> 3) == (ci >> 3), l, 0.0) for l in l_list]
    n2 = _mm3_all(n0, n0)
    n4 = _mm3_all(n2, n2)
    p = _mm3_all([eye - a for a in n0], [eye + a for a in n2])
    d = _mm3_all(p, [eye + a for a in n4])
    s = 8
    while s < seq_len:
        sh = s.bit_length() - 1
        off = ((ri >> (sh + 1)) == (ci >> (sh + 1))) & ((ri >> sh) != (ci >> sh))
        e = [jnp.where(off, l, 0.0) for l in l_list]
        de = _mm3_all(d, e)
        ded = _mm3_all(de, d)
        d = [a - b for a, b in zip(d, ded)]
        s *= 2
    return d


def _in_proj_kernel(x_ref, g_ref, w_ref, wgt_ref, proj_ref, gt_ref):
    hb = _rms(x_ref[...], g_ref[...]).astype(BF16)
    proj_ref[...] = _dot(hb, w_ref[...])
    gt_ref[...] = _dot_nt(wgt_ref[...], hb)


def _in_proj(x, g, w, wgt):
    rows = x.shape[0]
    tm = ROW_TILE
    return pl.pallas_call(
        _in_proj_kernel,
        grid=(rows // tm,),
        in_specs=[
            pl.BlockSpec((tm, D_MODEL), lambda i: (i, 0)),
            pl.BlockSpec((1, D_MODEL), lambda i: (0, 0)),
            pl.BlockSpec((D_MODEL, N_PROJ), lambda i: (0, 0)),
            pl.BlockSpec((N_GATE, D_MODEL), lambda i: (0, 0)),
        ],
        out_specs=[
            pl.BlockSpec((tm, N_PROJ), lambda i: (i, 0)),
            pl.BlockSpec((N_GATE, tm), lambda i: (0, i)),
        ],
        out_shape=[jax.ShapeDtypeStruct((rows, N_PROJ), F32),
                   jax.ShapeDtypeStruct((N_GATE, rows), F32)],
        compiler_params=_cparams(1),
        name="in_proj",
    )(x, g, w, wgt)


def _chunk_masks(seq_len):
    c = CHUNK
    ri = lax.broadcasted_iota(jnp.int32, (c, c), 0)
    ci = lax.broadcasted_iota(jnp.int32, (c, c), 1)
    if seq_len < c:
        sh = seq_len.bit_length() - 1
        same = (ri >> sh) == (ci >> sh)
    else:
        same = ri >= 0
    return ri, ci, same


def _gdn_prep_kernel(*refs, tb, seq_len, valid_lo, idt):
    has_cprev = valid_lo > 0
    if has_cprev:
        (u_ref, prev_ref, gcol_ref, grow_ref, cprev_ref, cw_ref, pa_ref, pl_ref, ra_ref, rl_ref,
         wkqg_ref, wv_ref, qk_ref, kdt_ref, el_ref) = refs
    else:
        (u_ref, prev_ref, gcol_ref, grow_ref, cw_ref, pa_ref, pl_ref, ra_ref, rl_ref,
         wkqg_ref, wv_ref, qk_ref, kdt_ref, el_ref) = refs
    c = CHUNK
    sh = seq_len.bit_length() - 1
    u = u_ref[0]
    rowpos = lax.broadcasted_iota(jnp.int32, (tb, 1), 0) & (seq_len - 1)
    if has_cprev:
        u = jnp.where((rowpos >= valid_lo - (CONV_W - 1)) & (rowpos < valid_lo), cprev_ref[0], u)
    prev = jnp.where(pl.program_id(1) == 0, 0.0, prev_ref[0])
    xp = jnp.concatenate([prev, u], axis=0)
    cw = cw_ref[...]
    y = None
    for i in range(CONV_W):
        s = CONV_W - 1 - i
        xs = u if s == 0 else pltpu.roll(xp, s, 0)[8:8 + tb]
        t = xs * cw[i:i + 1, :]
        y = t if y is None else y + t
    y = jax.nn.silu(y)

    lane = lax.broadcasted_iota(jnp.int32, (1, LANES), 1)
    xg = gcol_ref[0] + pa_ref[0:1, :]
    gcolv = jnp.where(lane < N_HEADS, jax.nn.sigmoid(xg), -jnp.exp(pl_ref[0:1, :]) * _softplus(xg))
    if valid_lo > 0:
        gcolv = jnp.where(rowpos >= valid_lo, gcolv, 0.0)

    ri, ci, same = _chunk_masks(seq_len)
    causal = (ri >= ci) & same
    strict = (ri > ci) & same
    upper = (ri <= ci) & same
    colpos = lax.broadcasted_iota(jnp.int32, (1, c), 1) & (seq_len - 1)
    r8 = lax.broadcasted_iota(jnp.int32, (8, c), 0)
    c8 = lax.broadcasted_iota(jnp.int32, (8, c), 1)

    items = []
    for k in range(tb // c):
        r0 = k * c
        growv = -jnp.exp(rl_ref[:, 0:1]) * _softplus(grow_ref[k] + ra_ref[:, 0:1])
        if valid_lo > 0:
            growv = jnp.where(colpos >= valid_lo, growv, 0.0)
        for h in range(N_HEADS):
            hs = slice(DH * h, DH * (h + 1))
            qc = y[r0:r0 + c, DH * h:DH * (h + 1)]
            kc = y[r0:r0 + c, N_HEADS * DH + DH * h:N_HEADS * DH + DH * (h + 1)]
            vc = y[r0:r0 + c, 2 * N_HEADS * DH + DH * h:2 * N_HEADS * DH + DH * (h + 1)]
            qn = qc * lax.rsqrt(jnp.sum(qc * qc, axis=-1, keepdims=True) + EPS) * (DH ** -0.5)
            kn = kc * lax.rsqrt(jnp.sum(kc * kc, axis=-1, keepdims=True) + EPS)
            beta = gcolv[r0:r0 + c, h:h + 1]
            g_col = gcolv[r0:r0 + c, N_HEADS + h:N_HEADS + h + 1]
            g_row = growv[N_HEADS + h:N_HEADS + h + 1, :]
            gc_col = jnp.sum(jnp.where(causal, g_row, 0.0), axis=1, keepdims=True)
            gc_row = jnp.sum(jnp.where(upper, g_col, 0.0), axis=0, keepdims=True)
            gl_col = jnp.sum(jnp.where(same, g_row, 0.0), axis=1, keepdims=True)
            decay = jnp.where(causal, jnp.exp(jnp.where(causal, gc_col - gc_row, 0.0)), 0.0)
            el = jnp.exp(jnp.sum(jnp.where((c8 >> sh) == r8, g_row, 0.0), axis=1, keepdims=True))
            wkqg_ref[0, k, c:2 * c, hs] = (qn * jnp.exp(gc_col)).astype(idt)
            kdt_ref[0, k, hs, :] = (kn * jnp.exp(gl_col - gc_col)).T.astype(idt)
            el_ref[0, k, :, hs] = jnp.broadcast_to(el, (8, DH))
            items.append(dict(k=k, h=h, hs=hs, qb=qn.astype(BF16), kb=kn.astype(BF16), beta=beta,
                              decay=decay, rhs_v=beta * vc, rhs_k=(beta * jnp.exp(gc_col)) * kn))
    kks = [_dot_nt(it["kb"], it["kb"]) for it in items]
    qks = [_dot_nt(it["qb"], it["kb"]) for it in items]
    for it, qk in zip(items, qks):
        qk_ref[0, it["k"], it["h"]] = (qk * it["decay"]).astype(idt)
    ls = [jnp.where(strict, it["beta"] * kk * it["decay"], 0.0) for it, kk in zip(items, kks)]
    tinvs = _unit_lower_inverse_all(ls, ri, ci, seq_len)
    wvs = _mm3_all(tinvs, [it["rhs_v"] for it in items])
    wks = _mm3_all(tinvs, [it["rhs_k"] for it in items])
    for it, wv, wk in zip(items, wvs, wks):
        wv_ref[0, it["k"], :, it["hs"]] = wv
        wkqg_ref[0, it["k"], 0:c, it["hs"]] = wk.astype(idt)


def _gdn_prep(proj3, grow, cprev3, cw, pa, pl_, ra, rl, *, seq_len, valid_lo, idt, tb):
    nb, t, _ = proj3.shape
    c = CHUNK
    nt = t // tb
    kb = tb // c
    has_cprev = valid_lo > 0
    in_specs = [
        pl.BlockSpec((1, tb, D_CONV), lambda b, i: (b, i, 0)),
        pl.BlockSpec((1, 8, D_CONV), lambda b, i: (b, jnp.maximum(i * (tb // 8) - 1, 0), 0)),
        pl.BlockSpec((1, tb, LANES), lambda b, i: (b, i, GATE_COL_BLOCK)),
        pl.BlockSpec((kb, N_GATE, c), lambda b, i: (b * nt + i, 0, 0)),
    ]
    args = [proj3, proj3, proj3, grow]
    if has_cprev:
        in_specs.append(pl.BlockSpec((1, tb, D_CONV), lambda b, i: (b, i, 0)))
        args.append(cprev3)
    in_specs += [
        pl.BlockSpec((CONV_W, D_CONV), lambda b, i: (0, 0)),
        pl.BlockSpec((8, LANES), lambda b, i: (0, 0)),
        pl.BlockSpec((8, LANES), lambda b, i: (0, 0)),
        pl.BlockSpec((N_GATE, LANES), lambda b, i: (0, 0)),
        pl.BlockSpec((N_GATE, LANES), lambda b, i: (0, 0)),
    ]
    args += [cw, pa, pl_, ra, rl]
    n = t // c
    hd = N_HEADS * DH
    out_shape = [
        jax.ShapeDtypeStruct((nb, n, 2 * c, hd), idt),
        jax.ShapeDtypeStruct((nb, n, c, hd), F32),
        jax.ShapeDtypeStruct((nb, n, N_HEADS, c, c), idt),
        jax.ShapeDtypeStruct((nb, n, hd, c), idt),
        jax.ShapeDtypeStruct((nb, n, 8, hd), F32),
    ]
    out_specs = [
        pl.BlockSpec((1, kb, 2 * c, hd), lambda b, i: (b, i, 0, 0)),
        pl.BlockSpec((1, kb, c, hd), lambda b, i: (b, i, 0, 0)),
        pl.BlockSpec((1, kb, N_HEADS, c, c), lambda b, i: (b, i, 0, 0, 0)),
        pl.BlockSpec((1, kb, hd, c), lambda b, i: (b, i, 0, 0)),
        pl.BlockSpec((1, kb, 8, hd), lambda b, i: (b, i, 0, 0)),
    ]
    return pl.pallas_call(
        functools.partial(_gdn_prep_kernel, tb=tb, seq_len=seq_len, valid_lo=valid_lo, idt=idt),
        grid=(nb, nt),
        in_specs=in_specs,
        out_specs=out_specs,
        out_shape=out_shape,
        compiler_params=_cparams(2),
        name="gdn_prep",
    )(*args)


def _gated_norm_store(o, z, gn, mix_ref, idx):
    mix_ref[idx] = (_rms(o, gn) * jax.nn.silu(z)).astype(BF16)


def _gdn_scan_prompt_kernel(wkqg_ref, wv_ref, qk_ref, kdt_ref, el_ref, z_ref, gn_ref,
                            mix_ref, sout_ref, s_scr, *, nc):
    c = CHUNK

    @pl.when(pl.program_id(1) == 0)
    def _():
        s_scr[...] = jnp.zeros_like(s_scr)

    gn = gn_ref[...]

    def body(n, carry):
        r0 = pl.multiple_of(n * c, c)
        heads = [slice(DH * h, DH * (h + 1)) for h in range(N_HEADS)]
        ss = [s_scr[h] for h in range(N_HEADS)]
        aa = [_dot(wkqg_ref[0, n, :, hs], s.astype(BF16)) for hs, s in zip(heads, ss)]
        ubs = [(wv_ref[0, n, :, hs] - a[0:c]).astype(BF16) for hs, a in zip(heads, aa)]
        ds = [_dot(kdt_ref[0, n, hs, :], ub) for hs, ub in zip(heads, ubs)]
        os_ = [_dot(qk_ref[0, n, h], ubs[h]) for h in range(N_HEADS)]
        for h, hs in enumerate(heads):
            s_scr[h] = el_ref[0, n, 0:1, hs] * ss[h] + ds[h]
            _gated_norm_store(aa[h][c:2 * c] + os_[h], z_ref[0, pl.ds(r0, c), hs], gn, mix_ref,
                              (0, pl.ds(r0, c), hs))
        return carry

    lax.fori_loop(0, nc, body, 0)

    @pl.when(pl.program_id(1) == pl.num_programs(1) - 1)
    def _():
        sout_ref[0] = s_scr[...]


def _gdn_scan_prompt(wkqg, wv, qk, kdt, el, proj3, gn, *, ts):
    nb, n, _, hd = wkqg.shape
    c = CHUNK
    t = n * c
    nc = ts // c
    zblk = D_CONV // hd
    return pl.pallas_call(
        functools.partial(_gdn_scan_prompt_kernel, nc=nc),
        grid=(nb, t // ts),
        in_specs=[
            pl.BlockSpec((1, nc, 2 * c, hd), lambda b, i: (b, i, 0, 0)),
            pl.BlockSpec((1, nc, c, hd), lambda b, i: (b, i, 0, 0)),
            pl.BlockSpec((1, nc, N_HEADS, c, c), lambda b, i: (b, i, 0, 0, 0)),
            pl.BlockSpec((1, nc, hd, c), lambda b, i: (b, i, 0, 0)),
            pl.BlockSpec((1, nc, 8, hd), lambda b, i: (b, i, 0, 0)),
            pl.BlockSpec((1, ts, hd), lambda b, i: (b, i, zblk)),
            pl.BlockSpec((1, DH), lambda b, i: (0, 0)),
        ],
        out_specs=[
            pl.BlockSpec((1, ts, hd), lambda b, i: (b, i, 0)),
            pl.BlockSpec((1, N_HEADS, DH, DH), lambda b, i: (b, 0, 0, 0)),
        ],
        out_shape=[jax.ShapeDtypeStruct((nb, t, hd), BF16),
                   jax.ShapeDtypeStruct((nb, N_HEADS, DH, DH), F32)],
        scratch_shapes=[pltpu.VMEM((N_HEADS, DH, DH), F32)],
        compiler_params=_cparams(2),
        name="gdn_scan_prompt",
    )(wkqg, wv, qk, kdt, el, proj3, gn)


def _gdn_scan_decode_kernel(wkqg_ref, wv_ref, qk_ref, kdt_ref, el_ref, z_ref, gn_ref, s0_ref,
                            mix_ref, sout_ref):
    c = CHUNK
    nseq = c // SLOT
    rowseq = lax.broadcasted_iota(jnp.int32, (c, 1), 0) >> 3
    gn = gn_ref[...]
    heads = [slice(DH * h, DH * (h + 1)) for h in range(N_HEADS)]
    aa = []
    for h, hs in enumerate(heads):
        w = wkqg_ref[0, 0, :, hs]
        row = []
        for j in range(nseq):
            wj = jnp.concatenate([w[SLOT * j:SLOT * (j + 1)], w[c + SLOT * j:c + SLOT * (j + 1)]],
                                 axis=0).astype(BF16)
            row.append(_dot(wj, s0_ref[j, h].astype(BF16)))
        aa.append(row)
    us = [wv_ref[0, 0, :, hs] - jnp.concatenate([a[0:SLOT] for a in aa[h]], axis=0)
          for h, hs in enumerate(heads)]
    os_ = [_dot(qk_ref[0, 0, h].astype(BF16), us[h].astype(BF16)) for h in range(N_HEADS)]
    for h, hs in enumerate(heads):
        kdt = kdt_ref[0, 0, hs, :].astype(BF16)
        for j in range(nseq):
            uj = jnp.where(rowseq == j, us[h], 0.0).astype(BF16)
            sout_ref[j, h] = el_ref[0, 0, j:j + 1, hs] * s0_ref[j, h] + _dot(kdt, uj)
    for h, hs in enumerate(heads):
        o = jnp.concatenate([a[SLOT:2 * SLOT] for a in aa[h]], axis=0) + os_[h]
        _gated_norm_store(o, z_ref[:, hs], gn, mix_ref, (slice(None), hs))


def _gdn_scan_decode(wkqg, wv, qk, kdt, el, proj, gn, s0):
    _, n, _, hd = wkqg.shape
    c = CHUNK
    nseq = c // SLOT
    zblk = D_CONV // hd
    return pl.pallas_call(
        _gdn_scan_decode_kernel,
        grid=(n,),
        in_specs=[
            pl.BlockSpec((1, 1, 2 * c, hd), lambda i: (0, i, 0, 0)),
            pl.BlockSpec((1, 1, c, hd), lambda i: (0, i, 0, 0)),
            pl.BlockSpec((1, 1, N_HEADS, c, c), lambda i: (0, i, 0, 0, 0)),
            pl.BlockSpec((1, 1, hd, c), lambda i: (0, i, 0, 0)),
            pl.BlockSpec((1, 1, 8, hd), lambda i: (0, i, 0, 0)),
            pl.BlockSpec((c, hd), lambda i: (i, zblk)),
            pl.BlockSpec((1, DH), lambda i: (0, 0)),
            pl.BlockSpec((nseq, N_HEADS, DH, DH), lambda i: (i, 0, 0, 0)),
        ],
        out_specs=[
            pl.BlockSpec((c, hd), lambda i: (i, 0)),
            pl.BlockSpec((nseq, N_HEADS, DH, DH), lambda i: (i, 0, 0, 0)),
        ],
        out_shape=[jax.ShapeDtypeStruct((n * c, hd), BF16),
                   jax.ShapeDtypeStruct(s0.shape, F32)],
        compiler_params=_cparams(1),
        name="gdn_scan_decode",
    )(wkqg, wv, qk, kdt, el, proj, gn, s0)


def _mlstm_gates(gcol, growk, pa_ref, ra_ref, h, rows, valid_lo, rowpos, colpos):
    xg = gcol + pa_ref[0:1, :]
    ig_col = xg[:, 2 * N_HEADS + h:2 * N_HEADS + h + 1]
    fl_col = -_softplus(-xg[:, 3 * N_HEADS + h:3 * N_HEADS + h + 1])
    xr = growk + ra_ref[:, 0:1]
    ig_row = xr[2 * N_HEADS + h:2 * N_HEADS + h + 1, :]
    fl_row = -_softplus(-xr[3 * N_HEADS + h:3 * N_HEADS + h + 1, :])
    if valid_lo > 0:
        ig_col = jnp.where(rowpos >= valid_lo, ig_col, NEG)
        fl_col = jnp.where(rowpos >= valid_lo, fl_col, 0.0)
        ig_row = jnp.where(colpos >= valid_lo, ig_row, NEG)
        fl_row = jnp.where(colpos >= valid_lo, fl_row, 0.0)
    return ig_col, fl_col, ig_row, fl_row


def _mlstm_chunks(probs, seq_len):
    c = CHUNK
    nseq = c // seq_len
    ri, ci, same = _chunk_masks(seq_len)
    causal = (ri >= ci) & same
    upper = (ri <= ci) & same
    seq_end = ci == (ri | (seq_len - 1))
    rowseq = lax.broadcasted_iota(jnp.int32, (c, 1), 0) >> (seq_len.bit_length() - 1)
    zpad = jnp.zeros((16 - seq_len, DH), F32) if nseq > 1 else None
    for p in probs:
        q, k, v, mp_col = p["q"], p["k"], p["v"], p["mp_col"]
        f_col = jnp.sum(jnp.where(causal, p["fl_row"], 0.0), axis=1, keepdims=True)
        f_row = jnp.sum(jnp.where(upper, p["fl_col"], 0.0), axis=0, keepdims=True)
        b_col = p["ig_col"] - f_col
        b_row = p["ig_row"] - f_row
        bmax_col = jnp.max(jnp.where(causal, b_row, -jnp.inf), axis=1, keepdims=True)
        km = k * (DH ** -0.5)
        m_col = f_col + jnp.maximum(mp_col, bmax_col)
        p["m_col"] = m_col
        p["a_col"] = jnp.exp(f_col + mp_col - m_col)
        p["dexp"] = jnp.where(causal, jnp.exp(jnp.where(causal, (f_col + b_row) - m_col, 0.0)), 0.0)
        fl_end = jnp.sum(jnp.where(seq_end, f_row, 0.0), axis=1, keepdims=True)
        bmax_end = jnp.max(jnp.where(same, b_row, -jnp.inf), axis=1, keepdims=True)
        ml_col = fl_end + jnp.maximum(mp_col, bmax_end)
        p["ml_col"] = ml_col
        p["al"] = jnp.broadcast_to(jnp.exp(fl_end + mp_col - ml_col), (c, DH))
        kw = km * jnp.exp(fl_end + b_col - ml_col)
        p["kw"] = kw
        p["kwt"] = kw.T.astype(BF16)
        p["qb"] = q.astype(BF16)
        p["kb"] = km.astype(BF16)
        p["vb"] = v.astype(BF16)
        p["qn"] = jnp.sum(q * p["n_rows"], axis=1, keepdims=True)
    for p in probs:
        p["qk"] = _dot_nt(p["qb"], p["kb"])
    for p in probs:
        if nseq == 1:
            p["qc"] = _dot(p["qb"], p["c_list"][0].astype(BF16))
            p["upd"] = [_dot(p["kwt"], p["vb"])]
        else:
            parts, upd = [], []
            for j in range(nseq):
                qj = jnp.concatenate([p["q"][seq_len * j:seq_len * (j + 1)], zpad], axis=0).astype(BF16)
                parts.append(_dot(qj, p["c_list"][j].astype(BF16))[0:seq_len])
                upd.append(_dot(p["kwt"], jnp.where(rowseq == j, p["v"], 0.0).astype(BF16)))
            p["qc"] = jnp.concatenate(parts, axis=0)
            p["upd"] = upd
    for p in probs:
        p["w"] = p["dexp"] * p["qk"]
    for p in probs:
        p["wv"] = _dot(p["w"].astype(BF16), p["vb"])
    for p in probs:
        num = p["a_col"] * p["qc"] + p["wv"]
        den = p["a_col"] * p["qn"] + jnp.sum(p["w"], axis=1, keepdims=True)
        p["hout"] = num / jnp.maximum(jnp.abs(den), jnp.exp(-p["m_col"]))
        p["new_c"] = [p["al"][seq_len * j:seq_len * j + 1] * p["c_list"][j] + p["upd"][j]
                      for j in range(nseq)]
    return probs


def _mlstm_out_store(hout, og, gnorm, mix_ref, idx):
    mix_ref[idx] = _rms(hout * jax.nn.sigmoid(og), gnorm).astype(BF16)


def _mlstm_prompt_kernel(q_ref, k_ref, v_ref, o_ref, gcol_ref, grow_ref, pa_ref, ra_ref, nrm_ref,
                         mix_ref, cout_ref, nout_ref, mout_ref, c_scr, n_scr, m_scr, *, nc, bs):
    c = CHUNK

    @pl.when(pl.program_id(1) == 0)
    def _():
        c_scr[...] = jnp.zeros_like(c_scr)
        n_scr[...] = jnp.zeros_like(n_scr)
        m_scr[...] = jnp.zeros_like(m_scr)

    def body(n, carry):
        r0 = pl.multiple_of(n * c, c)
        rows = pl.ds(r0, c)
        probs = []
        for b in range(bs):
            gcol = gcol_ref[b, rows, :]
            growk = grow_ref[b, n]
            for h in range(N_HEADS):
                hs = slice(DH * h, DH * (h + 1))
                ig_col, fl_col, ig_row, fl_row = _mlstm_gates(gcol, growk, pa_ref, ra_ref, h, c, 0, None, None)
                st = b * N_HEADS + h
                probs.append(dict(b=b, hs=hs, st=st, q=q_ref[b, rows, hs], k=k_ref[b, rows, hs],
                                  v=v_ref[b, rows, hs], ig_col=ig_col, fl_col=fl_col, ig_row=ig_row,
                                  fl_row=fl_row, mp_col=m_scr[st][:, 0:1], n_rows=n_scr[st],
                                  c_list=[c_scr[st]]))
        for p in _mlstm_chunks(probs, c):
            st = p["st"]
            c_scr[st] = p["new_c"][0]
            n_scr[st] = jnp.broadcast_to(
                p["al"][0:1] * p["n_rows"][0:1] + jnp.sum(p["kw"], axis=0, keepdims=True), (c, DH))
            m_scr[st] = jnp.broadcast_to(p["ml_col"], (c, DH))
            _mlstm_out_store(p["hout"], o_ref[p["b"], rows, p["hs"]], nrm_ref[:, p["hs"]], mix_ref,
                             (p["b"], rows, p["hs"]))
        return carry

    lax.fori_loop(0, nc, body, 0)

    @pl.when(pl.program_id(1) == pl.num_programs(1) - 1)
    def _():
        lane = lax.broadcasted_iota(jnp.int32, (8, DH), 1)
        for b in range(bs):
            mo = jnp.zeros((8, DH), F32)
            for h in range(N_HEADS):
                st = b * N_HEADS + h
                cout_ref[b, h] = c_scr[st]
                nout_ref[b, h:h + 1, :] = n_scr[st][0:1]
                mo = jnp.where(lane == h, m_scr[st][0:8], mo)
            mout_ref[b] = mo


def _mlstm_prompt(proj3, grow4, pa, ra, nrm, *, ts, bs):
    nb, t, _ = proj3.shape
    c = CHUNK
    hd = N_HEADS * DH
    nc = ts // c
    qblk = (D_CONV + hd) // hd
    return pl.pallas_call(
        functools.partial(_mlstm_prompt_kernel, nc=nc, bs=bs),
        grid=(nb // bs, t // ts),
        in_specs=[
            pl.BlockSpec((bs, ts, hd), lambda b, i: (b, i, qblk)),
            pl.BlockSpec((bs, ts, hd), lambda b, i: (b, i, qblk + 1)),
            pl.BlockSpec((bs, ts, hd), lambda b, i: (b, i, qblk + 2)),
            pl.BlockSpec((bs, ts, hd), lambda b, i: (b, i, qblk + 3)),
            pl.BlockSpec((bs, ts, LANES), lambda b, i: (b, i, GATE_COL_BLOCK)),
            pl.BlockSpec((bs, nc, N_GATE, c), lambda b, i: (b, i, 0, 0)),
            pl.BlockSpec((8, LANES), lambda b, i: (0, 0)),
            pl.BlockSpec((N_GATE, LANES), lambda b, i: (0, 0)),
            pl.BlockSpec((1, hd), lambda b, i: (0, 0)),
        ],
        out_specs=[
            pl.BlockSpec((bs, ts, hd), lambda b, i: (b, i, 0)),
            pl.BlockSpec((bs, N_HEADS, DH, DH), lambda b, i: (b, 0, 0, 0)),
            pl.BlockSpec((bs, N_HEADS, DH), lambda b, i: (b, 0, 0)),
            pl.BlockSpec((bs, 8, DH), lambda b, i: (b, 0, 0)),
        ],
        out_shape=[jax.ShapeDtypeStruct((nb, t, hd), BF16),
                   jax.ShapeDtypeStruct((nb, N_HEADS, DH, DH), F32),
                   jax.ShapeDtypeStruct((nb, N_HEADS, DH), F32),
                   jax.ShapeDtypeStruct((nb, 8, DH), F32)],
        scratch_shapes=[pltpu.VMEM((bs * N_HEADS, DH, DH), F32),
                        pltpu.VMEM((bs * N_HEADS, c, DH), F32),
                        pltpu.VMEM((bs * N_HEADS, c, DH), F32)],
        compiler_params=_cparams(2),
        name="mlstm_prompt",
    )(proj3, proj3, proj3, proj3, proj3, grow4, pa, ra, nrm)


def _mlstm_decode_kernel(q_ref, k_ref, v_ref, o_ref, gcol_ref, grow_ref, pa_ref, ra_ref, nrm_ref,
                         c0_ref, n0_ref, m0_ref, mix_ref, cout_ref, nout_ref, mout_ref):
    c = CHUNK
    nseq = c // SLOT
    rowpos = lax.broadcasted_iota(jnp.int32, (c, 1), 0) & (SLOT - 1)
    colpos = lax.broadcasted_iota(jnp.int32, (1, c), 1) & (SLOT - 1)
    lane = lax.broadcasted_iota(jnp.int32, (c, DH), 1)
    gcol = gcol_ref[...]
    growk = grow_ref[0]
    probs = []
    for h in range(N_HEADS):
        hs = slice(DH * h, DH * (h + 1))
        ig_col, fl_col, ig_row, fl_row = _mlstm_gates(gcol, growk, pa_ref, ra_ref, h, c, SLOT_PAD, rowpos, colpos)
        n_rows = jnp.concatenate(
            [jnp.broadcast_to(n0_ref[j, h:h + 1, :], (SLOT, DH)) for j in range(nseq)], axis=0)
        probs.append(dict(h=h, hs=hs, q=q_ref[:, hs], k=k_ref[:, hs], v=v_ref[:, hs], ig_col=ig_col,
                          fl_col=fl_col, ig_row=ig_row, fl_row=fl_row, mp_col=m0_ref[:, h:h + 1],
                          n_rows=n_rows, c_list=[c0_ref[j, h] for j in range(nseq)]))
    mo = jnp.zeros((c, DH), F32)
    for p in _mlstm_chunks(probs, SLOT):
        h, hs = p["h"], p["hs"]
        for j in range(nseq):
            cout_ref[j, h] = p["new_c"][j]
            rs = slice(SLOT * j, SLOT * (j + 1))
            nout_ref[j, h:h + 1, :] = (p["al"][SLOT * j:SLOT * j + 1] * n0_ref[j, h:h + 1, :]
                                       + jnp.sum(p["kw"][rs], axis=0, keepdims=True))
        mo = jnp.where(lane == h, jnp.broadcast_to(p["ml_col"], (c, DH)), mo)
        _mlstm_out_store(p["hout"], o_ref[:, hs], nrm_ref[:, hs], mix_ref, (slice(None), hs))
    mout_ref[...] = mo


def _mlstm_decode(proj, grow, pa, ra, nrm, c0, n0, m0rows):
    rows = proj.shape[0]
    c = CHUNK
    hd = N_HEADS * DH
    nseq = c // SLOT
    qblk = (D_CONV + hd) // hd
    return pl.pallas_call(
        _mlstm_decode_kernel,
        grid=(rows // c,),
        in_specs=[
            pl.BlockSpec((c, hd), lambda i: (i, qblk)),
            pl.BlockSpec((c, hd), lambda i: (i, qblk + 1)),
            pl.BlockSpec((c, hd), lambda i: (i, qblk + 2)),
            pl.BlockSpec((c, hd), lambda i: (i, qblk + 3)),
            pl.BlockSpec((c, LANES), lambda i: (i, GATE_COL_BLOCK)),
            pl.BlockSpec((1, N_GATE, c), lambda i: (i, 0, 0)),
            pl.BlockSpec((8, LANES), lambda i: (0, 0)),
            pl.BlockSpec((N_GATE, LANES), lambda i: (0, 0)),
            pl.BlockSpec((1, hd), lambda i: (0, 0)),
            pl.BlockSpec((nseq, N_HEADS, DH, DH), lambda i: (i, 0, 0, 0)),
            pl.BlockSpec((nseq, N_HEADS, DH), lambda i: (i, 0, 0)),
            pl.BlockSpec((c, LANES), lambda i: (i, 0)),
        ],
        out_specs=[
            pl.BlockSpec((c, hd), lambda i: (i, 0)),
            pl.BlockSpec((nseq, N_HEADS, DH, DH), lambda i: (i, 0, 0, 0)),
            pl.BlockSpec((nseq, N_HEADS, DH), lambda i: (i, 0, 0)),
            pl.BlockSpec((c, LANES), lambda i: (i, 0)),
        ],
        out_shape=[jax.ShapeDtypeStruct((rows, hd), BF16),
                   jax.ShapeDtypeStruct(c0.shape, F32),
                   jax.ShapeDtypeStruct(n0.shape, F32),
                   jax.ShapeDtypeStruct((rows, LANES), F32)],
        compiler_params=_cparams(1),
        name="mlstm_decode",
    )(proj, proj, proj, proj, proj, grow, pa, ra, nrm, c0, n0, m0rows)


def _post_kernel(x_ref, ma_ref, mb_ref, p_ref, woa_ref, wob_ref, gf_ref, wg_ref, wu_ref, wd_ref,
                 gp_ref, wpg_ref, wpp_ref, gfin_ref, o_ref, acc_ref, *, final):
    x = x_ref[...] + (_dot(ma_ref[...], woa_ref[...]) + _dot(mb_ref[...], wob_ref[...]))
    ub = _rms(x, gf_ref[...]).astype(BF16)
    for j in range(D_FF // FF_CHUNK):
        sl = slice(j * FF_CHUNK, (j + 1) * FF_CHUNK)
        a = (jax.nn.silu(_dot(ub, wg_ref[:, sl])) * _dot(ub, wu_ref[:, sl])).astype(BF16)
        d = _dot(a, wd_ref[sl, :])
        if j == 0:
            acc_ref[...] = d
        else:
            acc_ref[...] += d
    x = x + acc_ref[...]
    gate = jax.nn.sigmoid(_dot(_rms(x, gp_ref[...]).astype(BF16), wpg_ref[...]))
    x = x + _dot(p_ref[...].astype(BF16), wpp_ref[...]) * gate
    if final:
        x = _rms(x, gfin_ref[...])
    o_ref[...] = x


def _post(x, ma, mb, p, woa, wob, gf, wg, wu, wd, gp, wpg, wpp, gfin, *, final):
    rows = x.shape[0]
    tm = ROW_TILE
    hd = N_HEADS * DH
    row = lambda w: pl.BlockSpec((tm, w), lambda i: (i, 0))
    whole = lambda a, b: pl.BlockSpec((a, b), lambda i: (0, 0), pipeline_mode=pl.Buffered(1))
    return pl.pallas_call(
        functools.partial(_post_kernel, final=final),
        grid=(rows // tm,),
        in_specs=[row(D_MODEL), row(hd), row(hd), row(D_PLE),
                  whole(hd, D_MODEL), whole(hd, D_MODEL), whole(1, D_MODEL),
                  whole(D_MODEL, D_FF), whole(D_MODEL, D_FF), whole(D_FF, D_MODEL),
                  whole(1, D_MODEL), whole(D_MODEL, D_MODEL), whole(D_PLE, D_MODEL), whole(1, D_MODEL)],
        out_specs=row(D_MODEL),
        out_shape=jax.ShapeDtypeStruct((rows, D_MODEL), F32),
        scratch_shapes=[pltpu.VMEM((tm, D_MODEL), F32)],
        compiler_params=_cparams(1),
        name="post",
    )(x, ma, mb, p, woa, wob, gf, wg, wu, wd, gp, wpg, wpp, gfin)


def _lane_vec(pairs, rows):
    v = jnp.zeros((LANES,), F32)
    for off, val in pairs:
        v = v.at[off:off + N_HEADS].set(val.astype(F32))
    return jnp.broadcast_to(v[None, :], (rows, LANES))


def _col_vec(pairs):
    v = jnp.zeros((N_GATE,), F32)
    for off, val in pairs:
        v = v.at[off:off + N_HEADS].set(val.astype(F32))
    return jnp.broadcast_to(v[:, None], (N_GATE, LANES))


def kernel(x_prompt, x_sample, p_prompt, p_sample, state_gdn, state_gdn_conv, state_mlstm_C, state_mlstm_n, state_mlstm_m, w_in, conv_w, gdn_a_log, gdn_dt_bias, gdn_norm, mlstm_i_bias, mlstm_f_bias, mlstm_norm, w_out, norm_mix, norm_ffn, w_gate, w_up, w_down, norm_ple, w_ple_gate, w_ple_proj, norm_final):
    depth = w_in.shape[0]
    nb, t, _ = x_prompt.shape
    ns, tdec, _ = x_sample.shape
    hd = N_HEADS * DH
    c = CHUNK
    pad = SLOT - tdec

    xp = x_prompt.reshape(nb * t, D_MODEL)
    xs = jnp.pad(x_sample, ((0, 0), (pad, 0), (0, 0))).reshape(ns * SLOT, D_MODEL)
    ps_all = jnp.pad(p_sample, ((0, 0), (0, 0), (pad, 0), (0, 0))).reshape(depth, ns * SLOT, D_PLE)
    pp_all = p_prompt.reshape(depth, nb * t, D_PLE)
    gfin = norm_final.reshape(1, D_MODEL)

    outs_p = [[] for _ in range(5)]
    outs_s = [[] for _ in range(5)]
    for i in range(depth):
        w = w_in[i]
        o = D_CONV + hd
        w_main = jnp.concatenate([w[:, :o], w[:, o + 2 * N_HEADS:o + 2 * N_HEADS + 4 * hd]], axis=1)
        w_g = jnp.concatenate([w[:, o:o + 2 * N_HEADS], w[:, o + 2 * N_HEADS + 4 * hd:]], axis=1)
        w_r = jnp.concatenate([w_main, w_g, jnp.zeros((D_MODEL, LANES - N_GATE), F32)], axis=1).astype(BF16)
        w_gt = w_g.T.astype(BF16)
        g_mix = norm_mix[i].reshape(1, D_MODEL)

        adds = [(N_HEADS, gdn_dt_bias[i]), (2 * N_HEADS, mlstm_i_bias[i]), (3 * N_HEADS, mlstm_f_bias[i])]
        pa = _lane_vec(adds, 8)
        pl_ = _lane_vec([(N_HEADS, gdn_a_log[i])], 8)
        ra = _col_vec(adds)
        rl = _col_vec([(N_HEADS, gdn_a_log[i])])
        cw = conv_w[i].astype(F32)
        gn = gdn_norm[i].reshape(1, DH).astype(F32)
        nrm = mlstm_norm[i].reshape(1, hd).astype(F32)

        proj_p, gt_p = _in_proj(xp, g_mix, w_r, w_gt)
        proj_s, gt_s = _in_proj(xs, g_mix, w_r, w_gt)
        proj_p3 = proj_p.reshape(nb, t, N_PROJ)
        proj_s3 = proj_s.reshape(1, ns * SLOT, N_PROJ)
        grow_p = gt_p.reshape(N_GATE, nb * t // c, c).transpose(1, 0, 2)
        grow_s = gt_s.reshape(N_GATE, ns * SLOT // c, c).transpose(1, 0, 2)

        prep_p = _gdn_prep(proj_p3, grow_p, None, cw, pa, pl_, ra, rl, seq_len=c, valid_lo=0, idt=BF16, tb=256)
        mixa_p, s_p = _gdn_scan_prompt(*prep_p, proj_p3, gn, ts=512)
        cprev = jnp.pad(state_gdn_conv[i].astype(F32), ((0, 0), (pad - (CONV_W - 1), SLOT - pad), (0, 0)))
        cprev3 = cprev.reshape(1, ns * SLOT, D_CONV)
        prep_s = _gdn_prep(proj_s3, grow_s, cprev3, cw, pa, pl_, ra, rl, seq_len=SLOT, valid_lo=pad, idt=F32, tb=c)
        mixa_s, s_s = _gdn_scan_decode(*prep_s, proj_s, gn, state_gdn[i].astype(F32))

        mixb_p, c_p, n_p, m_p = _mlstm_prompt(proj_p3, grow_p.reshape(nb, t // c, N_GATE, c), pa, ra, nrm,
                                              ts=512, bs=2)
        m0rows = jnp.pad(jnp.repeat(state_mlstm_m[i].astype(F32), SLOT, axis=0), ((0, 0), (0, LANES - N_HEADS)))
        mixb_s, c_s, n_s, m_s = _mlstm_decode(proj_s, grow_s, pa, ra, nrm, state_mlstm_C[i].astype(F32),
                                              state_mlstm_n[i].astype(F32), m0rows)

        wts = (w_out[i][:hd].astype(BF16), w_out[i][hd:].astype(BF16), norm_ffn[i].reshape(1, D_MODEL),
               w_gate[i].astype(BF16), w_up[i].astype(BF16), w_down[i].astype(BF16),
               norm_ple[i].reshape(1, D_MODEL), w_ple_gate[i].astype(BF16), w_ple_proj[i].astype(BF16), gfin)
        final = i == depth - 1
        xp = _post(xp, mixa_p.reshape(nb * t, hd), mixb_p.reshape(nb * t, hd), pp_all[i], *wts, final=final)
        xs = _post(xs, mixa_s, mixb_s, ps_all[i], *wts, final=final)

        outs_p[0].append(s_p)
        outs_p[1].append(proj_p3[:, t - (CONV_W - 1):, :D_CONV])
        outs_p[2].append(c_p)
        outs_p[3].append(n_p)
        outs_p[4].append(m_p[:, 0, :N_HEADS])
        outs_s[0].append(s_s)
        outs_s[1].append(proj_s.reshape(ns, SLOT, N_PROJ)[:, SLOT - (CONV_W - 1):, :D_CONV])
        outs_s[2].append(c_s)
        outs_s[3].append(n_s)
        outs_s[4].append(m_s.reshape(ns, SLOT, LANES)[:, SLOT - 1, :N_HEADS])

    y_prompt = xp.reshape(nb, t, D_MODEL)
    y_sample = xs.reshape(ns, SLOT, D_MODEL)[:, pad:, :]
    sp = [jnp.stack(a, axis=0) for a in outs_p]
    ss = [jnp.stack(a, axis=0) for a in outs_s]
    return (y_prompt, y_sample, *sp, *ss)
```

```python
import functools

import jax
import jax.numpy as jnp
from jax import lax
from jax.experimental import pallas as pl
from jax.experimental.pallas import tpu as pltpu

F32 = jnp.float32
BF16 = jnp.bfloat16

D_MODEL = 1024
N_HEADS = 4
DH = 128
D_CONV = 3 * N_HEADS * DH
D_FF = 2816
D_PLE = 256
CONV_W = 4
CHUNK = 64
EPS = 1e-6
NEG = -1e30
N_GATE = 16
LANES = 128
N_PROJ = D_CONV + 5 * N_HEADS * DH + LANES
GATE_COL_BLOCK = (N_PROJ - LANES) // LANES
SLOT = 8
SLOT_PAD = 4
V7X_VMEM_LIMIT_BYTES = 56 * 1024 * 1024
ROW_TILE = 512
FF_CHUNK = 256


def _cparams(n_axes):
    return pltpu.CompilerParams(dimension_semantics=("arbitrary",) * n_axes,
                                vmem_limit_bytes=V7X_VMEM_LIMIT_BYTES)


def _rms(x, g):
    return x * lax.rsqrt(jnp.mean(x * x, axis=-1, keepdims=True) + EPS) * g


def _softplus(x):
    return jnp.maximum(x, 0.0) + jnp.log1p(jnp.exp(-jnp.abs(x)))


def _dot(a, b):
    return jnp.dot(a, b, preferred_element_type=F32)


def _dot_nt(a, b):
    return lax.dot_general(a, b, (((1,), (1,)), ((), ())), preferred_element_type=F32)


def _split(a):
    hi = a.astype(BF16)
    lo = (a - hi.astype(F32)).astype(BF16)
    return hi, lo


def _split_all(xs):
    return [_split(x) for x in xs]


def _mm3_all(a_list, b_list):
    return [_dot(ah, bh) + (_dot(ah, bl) + _dot(al, bh)) for (ah, al), (bh, bl) in zip(a_list, b_list)]


def _unit_lower_inverse_all(l_list, ri, ci, seq_len):
    eye = jnp.where(ri == ci, 1.0, 0.0)
    n0 = [jnp.where((ri >> 3) == (ci >> 3), l, 0.0) for l in l_list]
    n0s = _split_all(n0)
    n2 = _mm3_all(n0s, n0s)
    n2s = _split_all(n2)
    n4 = _mm3_all(n2s, n2s)
    p = _mm3_all(_split_all([eye - a for a in n0]), _split_all([eye + a for a in n2]))
    d = _mm3_all(_split_all(p), _split_all([eye + a for a in n4]))
    ds = _split_all(d)
    s = 8
    while s < seq_len:
        sh = s.bit_length() - 1
        off = ((ri >> (sh + 1)) == (ci >> (sh + 1))) & ((ri >> sh) != (ci >> sh))
        de = _mm3_all(ds, _split_all([jnp.where(off, l, 0.0) for l in l_list]))
        ded = _mm3_all(_split_all(de), ds)
        d = [a - b for a, b in zip(d, ded)]
        ds = _split_all(d)
        s *= 2
    return ds


def _in_proj_kernel(x_ref, g_ref, w_ref, wgt_ref, proj_ref, gt_ref):
    hb = _rms(x_ref[...], g_ref[...]).astype(BF16)
    proj_ref[...] = _dot(hb, w_ref[...])
    gt_ref[...] = _dot_nt(wgt_ref[...], hb)


def _layer_spec(layer, *shape):
    zeros = (0,) * len(shape)
    return pl.BlockSpec((None,) + shape, lambda *_: (layer,) + zeros)


def _in_proj(x, g, w, wgt, layer):
    rows = x.shape[0]
    tm = ROW_TILE
    return pl.pallas_call(
        _in_proj_kernel,
        grid=(rows // tm,),
        in_specs=[
            pl.BlockSpec((tm, D_MODEL), lambda i: (i, 0)),
            _layer_spec(layer, 1, D_MODEL),
            _layer_spec(layer, D_MODEL, N_PROJ),
            _layer_spec(layer, N_GATE, D_MODEL),
        ],
        out_specs=[
            pl.BlockSpec((tm, N_PROJ), lambda i: (i, 0)),
            pl.BlockSpec((N_GATE, tm), lambda i: (0, i)),
        ],
        out_shape=[jax.ShapeDtypeStruct((rows, N_PROJ), F32),
                   jax.ShapeDtypeStruct((N_GATE, rows), F32)],
        compiler_params=_cparams(1),
        name="in_proj",
    )(x, g, w, wgt)


def _chunk_masks(seq_len):
    c = CHUNK
    ri = lax.broadcasted_iota(jnp.int32, (c, c), 0)
    ci = lax.broadcasted_iota(jnp.int32, (c, c), 1)
    if seq_len < c:
        sh = seq_len.bit_length() - 1
        same = (ri >> sh) == (ci >> sh)
    else:
        same = ri >= 0
    return ri, ci, same


def _gdn_prep_kernel(*refs, tb, seq_len, valid_lo, idt):
    has_cprev = valid_lo > 0
    if has_cprev:
        (u_ref, prev_ref, gcol_ref, grow_ref, cprev_ref, cw_ref, pa_ref, pl_ref, ra_ref, rl_ref,
         wkqg_ref, wv_ref, qk_ref, kdt_ref, el_ref) = refs
    else:
        (u_ref, prev_ref, gcol_ref, grow_ref, cw_ref, pa_ref, pl_ref, ra_ref, rl_ref,
         wkqg_ref, wv_ref, qk_ref, kdt_ref, el_ref) = refs
    c = CHUNK
    sh = seq_len.bit_length() - 1
    u = u_ref[0]
    rowpos = lax.broadcasted_iota(jnp.int32, (tb, 1), 0) & (seq_len - 1)
    if has_cprev:
        u = jnp.where((rowpos >= valid_lo - (CONV_W - 1)) & (rowpos < valid_lo), cprev_ref[0], u)
    xp = jnp.concatenate([jnp.where(pl.program_id(1) == 0, 0.0, prev_ref[0]), u], axis=0)
    cw = cw_ref[...]
    y = None
    for i in range(CONV_W):
        s = CONV_W - 1 - i
        xs = u if s == 0 else pltpu.roll(xp, s, 0)[8:8 + tb]
        t = xs * cw[i:i + 1, :]
        y = t if y is None else y + t
    y = jax.nn.silu(y)

    lane = lax.broadcasted_iota(jnp.int32, (1, LANES), 1)
    xg = gcol_ref[0] + pa_ref[0:1, :]
    gcolv = jnp.where(lane < N_HEADS, jax.nn.sigmoid(xg), -jnp.exp(pl_ref[0:1, :]) * _softplus(xg))
    if valid_lo > 0:
        gcolv = jnp.where(rowpos >= valid_lo, gcolv, 0.0)

    ri, ci, same = _chunk_masks(seq_len)
    causal = (ri >= ci) & same
    strict = (ri > ci) & same
    upper = (ri <= ci) & same
    colpos = lax.broadcasted_iota(jnp.int32, (1, c), 1) & (seq_len - 1)
    r8 = lax.broadcasted_iota(jnp.int32, (8, c), 0)
    c8 = lax.broadcasted_iota(jnp.int32, (8, c), 1)

    items = []
    for k in range(tb // c):
        r0 = k * c
        growv = -jnp.exp(rl_ref[:, 0:1]) * _softplus(grow_ref[k] + ra_ref[:, 0:1])
        if valid_lo > 0:
            growv = jnp.where(colpos >= valid_lo, growv, 0.0)
        for h in range(N_HEADS):
            qc = y[r0:r0 + c, DH * h:DH * (h + 1)]
            kc = y[r0:r0 + c, N_HEADS * DH + DH * h:N_HEADS * DH + DH * (h + 1)]
            g_col = gcolv[r0:r0 + c, N_HEADS + h:N_HEADS + h + 1]
            g_row = growv[N_HEADS + h:N_HEADS + h + 1, :]
            items.append(dict(
                k=k, h=h, hs=slice(DH * h, DH * (h + 1)), qc=qc, kc=kc,
                vc=y[r0:r0 + c, 2 * N_HEADS * DH + DH * h:2 * N_HEADS * DH + DH * (h + 1)],
                beta=gcolv[r0:r0 + c, h:h + 1],
                qss=jnp.sum(qc * qc, axis=-1, keepdims=True),
                kss=jnp.sum(kc * kc, axis=-1, keepdims=True),
                gc_col=jnp.sum(jnp.where(causal, g_row, 0.0), axis=1, keepdims=True),
                gc_row=jnp.sum(jnp.where(upper, g_col, 0.0), axis=0, keepdims=True),
                gl_col=jnp.sum(jnp.where(same, g_row, 0.0), axis=1, keepdims=True),
                gl8=jnp.sum(jnp.where((c8 >> sh) == r8, g_row, 0.0), axis=1, keepdims=True)))
    for it in items:
        k, hs, gc_col, beta = it["k"], it["hs"], it["gc_col"], it["beta"]
        qn = it["qc"] * lax.rsqrt(it["qss"] + EPS) * (DH ** -0.5)
        kn = it["kc"] * lax.rsqrt(it["kss"] + EPS)
        it["decay"] = jnp.where(causal, jnp.exp(jnp.where(causal, gc_col - it["gc_row"], 0.0)), 0.0)
        wkqg_ref[0, k, c:2 * c, hs] = (qn * jnp.exp(gc_col)).astype(idt)
        el_ref[0, k, :, hs] = jnp.broadcast_to(jnp.exp(it["gl8"]), (8, DH))
        it["kd"] = kn * jnp.exp(it["gl_col"] - gc_col)
        it["qb"] = qn.astype(BF16)
        it["kb"] = kn.astype(BF16)
        it["rhs_v"] = beta * it["vc"]
        it["rhs_k"] = (beta * jnp.exp(gc_col)) * kn
    for it in items:
        kdt_ref[0, it["k"], it["hs"], :] = it["kd"].T.astype(idt)
    kks = [_dot_nt(it["kb"], it["kb"]) for it in items]
    qks = [_dot_nt(it["qb"], it["kb"]) for it in items]
    for it, qk in zip(items, qks):
        qk_ref[0, it["k"], it["h"]] = (qk * it["decay"]).astype(idt)
    ls = [jnp.where(strict, it["beta"] * kk * it["decay"], 0.0) for it, kk in zip(items, kks)]
    tinvs = _unit_lower_inverse_all(ls, ri, ci, seq_len)
    rhs = _split_all([jnp.concatenate([it["rhs_v"], it["rhs_k"]], axis=1) for it in items])
    for it, sol in zip(items, _mm3_all(tinvs, rhs)):
        wv_ref[0, it["k"], :, it["hs"]] = sol[:, 0:DH]
        wkqg_ref[0, it["k"], 0:c, it["hs"]] = sol[:, DH:2 * DH].astype(idt)


def _gdn_prep(proj3, grow, cprev, cw, pa, pl_, ra, rl, layer, *, seq_len, valid_lo, idt, tb):
    nb, t, _ = proj3.shape
    c = CHUNK
    nt = t // tb
    kb = tb // c
    has_cprev = valid_lo > 0
    in_specs = [
        pl.BlockSpec((1, tb, D_CONV), lambda b, i: (b, i, 0)),
        pl.BlockSpec((1, 8, D_CONV), lambda b, i: (b, jnp.maximum(i * (tb // 8) - 1, 0), 0)),
        pl.BlockSpec((1, tb, LANES), lambda b, i: (b, i, GATE_COL_BLOCK)),
        pl.BlockSpec((kb, N_GATE, c), lambda b, i: (b * nt + i, 0, 0)),
    ]
    args = [proj3, proj3, proj3, grow]
    if has_cprev:
        assert nb == 1
        in_specs.append(pl.BlockSpec((1, tb, D_CONV), lambda b, i: (layer, i, 0)))
        args.append(cprev)
    in_specs += [
        _layer_spec(layer, CONV_W, D_CONV),
        _layer_spec(layer, 8, LANES),
        _layer_spec(layer, 8, LANES),
        _layer_spec(layer, N_GATE, LANES),
        _layer_spec(layer, N_GATE, LANES),
    ]
    args += [cw, pa, pl_, ra, rl]
    n = t // c
    hd = N_HEADS * DH
    out_shape = [
        jax.ShapeDtypeStruct((nb, n, 2 * c, hd), idt),
        jax.ShapeDtypeStruct((nb, n, c, hd), F32),
        jax.ShapeDtypeStruct((nb, n, N_HEADS, c, c), idt),
        jax.ShapeDtypeStruct((nb, n, hd, c), idt),
        jax.ShapeDtypeStruct((nb, n, 8, hd), F32),
    ]
    out_specs = [
        pl.BlockSpec((1, kb, 2 * c, hd), lambda b, i: (b, i, 0, 0)),
        pl.BlockSpec((1, kb, c, hd), lambda b, i: (b, i, 0, 0)),
        pl.BlockSpec((1, kb, N_HEADS, c, c), lambda b, i: (b, i, 0, 0, 0)),
        pl.BlockSpec((1, kb, hd, c), lambda b, i: (b, i, 0, 0)),
        pl.BlockSpec((1, kb, 8, hd), lambda b, i: (b, i, 0, 0)),
    ]
    return pl.pallas_call(
        functools.partial(_gdn_prep_kernel, tb=tb, seq_len=seq_len, valid_lo=valid_lo, idt=idt),
        grid=(nb, nt),
        in_specs=in_specs,
        out_specs=out_specs,
        out_shape=out_shape,
        compiler_params=_cparams(2),
        name="gdn_prep",
    )(*args)


def _gated_norm_store(o, z, gn, mix_ref, idx):
    mix_ref[idx] = (_rms(o, gn) * jax.nn.silu(z)).astype(BF16)


def _gdn_scan_prompt_kernel(wkqg_ref, wv_ref, qk_ref, kdt_ref, el_ref, z_ref, gn_ref,
                            mix_ref, sout_ref, s_scr, *, nc, bs):
    c = CHUNK

    @pl.when(pl.program_id(1) == 0)
    def _():
        s_scr[...] = jnp.zeros_like(s_scr)

    gn = gn_ref[...]
    units = [(b, h, slice(DH * h, DH * (h + 1))) for b in range(bs) for h in range(N_HEADS)]

    def body(n, carry):
        rows = pl.ds(pl.multiple_of(n * c, c), c)
        ss = [s_scr[b * N_HEADS + h] for b, h, _ in units]
        aa = [_dot(wkqg_ref[b, n, :, hs], s.astype(BF16)) for (b, _, hs), s in zip(units, ss)]
        ubs = [(wv_ref[b, n, :, hs] - a[0:c]).astype(BF16) for (b, _, hs), a in zip(units, aa)]
        ds = [_dot(kdt_ref[b, n, hs, :], ub) for (b, _, hs), ub in zip(units, ubs)]
        os_ = [_dot(qk_ref[b, n, h], ub) for (b, h, _), ub in zip(units, ubs)]
        for i, (b, h, hs) in enumerate(units):
            s_scr[b * N_HEADS + h] = el_ref[b, n, 0:1, hs] * ss[i] + ds[i]
            _gated_norm_store(aa[i][c:2 * c] + os_[i], z_ref[b, rows, hs], gn, mix_ref, (b, rows, hs))
        return carry

    lax.fori_loop(0, nc, body, 0)

    @pl.when(pl.program_id(1) == pl.num_programs(1) - 1)
    def _():
        for b in range(bs):
            sout_ref[b] = s_scr[b * N_HEADS:(b + 1) * N_HEADS]


def _gdn_scan_prompt(wkqg, wv, qk, kdt, el, proj3, gn, layer, *, ts, bs):
    nb, n, _, hd = wkqg.shape
    c = CHUNK
    t = n * c
    nc = ts // c
    zblk = D_CONV // hd
    return pl.pallas_call(
        functools.partial(_gdn_scan_prompt_kernel, nc=nc, bs=bs),
        grid=(nb // bs, t // ts),
        in_specs=[
            pl.BlockSpec((bs, nc, 2 * c, hd), lambda b, i: (b, i, 0, 0)),
            pl.BlockSpec((bs, nc, c, hd), lambda b, i: (b, i, 0, 0)),
            pl.BlockSpec((bs, nc, N_HEADS, c, c), lambda b, i: (b, i, 0, 0, 0)),
            pl.BlockSpec((bs, nc, hd, c), lambda b, i: (b, i, 0, 0)),
            pl.BlockSpec((bs, nc, 8, hd), lambda b, i: (b, i, 0, 0)),
            pl.BlockSpec((bs, ts, hd), lambda b, i: (b, i, zblk)),
            _layer_spec(layer, 1, DH),
        ],
        out_specs=[
            pl.BlockSpec((bs, ts, hd), lambda b, i: (b, i, 0)),
            pl.BlockSpec((bs, N_HEADS, DH, DH), lambda b, i: (b, 0, 0, 0)),
        ],
        out_shape=[jax.ShapeDtypeStruct((nb, t, hd), BF16),
                   jax.ShapeDtypeStruct((nb, N_HEADS, DH, DH), F32)],
        scratch_shapes=[pltpu.VMEM((bs * N_HEADS, DH, DH), F32)],
        compiler_params=_cparams(2),
        name="gdn_scan_prompt",
    )(wkqg, wv, qk, kdt, el, proj3, gn)


def _gdn_scan_decode_kernel(*refs, n_prev):
    (wkqg_ref, wv_ref, qk_ref, kdt_ref, el_ref, z_ref, gn_ref, s0_ref), rest = refs[:8], refs[8:]
    prev_refs, (mix_ref, sout_ref) = rest[:n_prev], rest[n_prev:]
    if n_prev:
        for l, p_ref in enumerate(prev_refs):
            sout_ref[l] = p_ref[...]
        sout_ref = sout_ref.at[n_prev]
    c = CHUNK
    nseq = c // SLOT
    rowseq = lax.broadcasted_iota(jnp.int32, (c, 1), 0) >> 3
    gn = gn_ref[...]
    heads = [slice(DH * h, DH * (h + 1)) for h in range(N_HEADS)]
    aa = []
    for h, hs in enumerate(heads):
        w = wkqg_ref[0, 0, :, hs]
        row = []
        for j in range(nseq):
            wj = jnp.concatenate([w[SLOT * j:SLOT * (j + 1)], w[c + SLOT * j:c + SLOT * (j + 1)]],
                                 axis=0).astype(BF16)
            row.append(_dot(wj, s0_ref[j, h].astype(BF16)))
        aa.append(row)
    us = [wv_ref[0, 0, :, hs] - jnp.concatenate([a[0:SLOT] for a in aa[h]], axis=0)
          for h, hs in enumerate(heads)]
    os_ = [_dot(qk_ref[0, 0, h].astype(BF16), us[h].astype(BF16)) for h in range(N_HEADS)]
    for h, hs in enumerate(heads):
        kdt = kdt_ref[0, 0, hs, :].astype(BF16)
        for j in range(nseq):
            uj = jnp.where(rowseq == j, us[h], 0.0).astype(BF16)
            sout_ref[j, h] = el_ref[0, 0, j:j + 1, hs] * s0_ref[j, h] + _dot(kdt, uj)
    for h, hs in enumerate(heads):
        o = jnp.concatenate([a[SLOT:2 * SLOT] for a in aa[h]], axis=0) + os_[h]
        _gated_norm_store(o, z_ref[:, hs], gn, mix_ref, (slice(None), hs))


def _state_specs(layer, prev, state_shape, nseq):
    tail = state_shape[2:]
    zeros = (0,) * len(tail)
    in_specs = [pl.BlockSpec((None, nseq) + tail, lambda i: (layer, i) + zeros)]
    in_specs += [pl.BlockSpec((nseq,) + tail, lambda i: (i,) + zeros) for _ in prev]
    if prev:
        out_spec = pl.BlockSpec((len(prev) + 1, nseq) + tail, lambda i: (0, i) + zeros)
        out_shape = jax.ShapeDtypeStruct((len(prev) + 1,) + state_shape[1:], F32)
    else:
        out_spec = pl.BlockSpec((nseq,) + tail, lambda i: (i,) + zeros)
        out_shape = jax.ShapeDtypeStruct(state_shape[1:], F32)
    return in_specs, out_spec, out_shape


def _gdn_scan_decode(wkqg, wv, qk, kdt, el, proj, gn, s0_all, prev, layer):
    _, n, _, hd = wkqg.shape
    c = CHUNK
    nseq = c // SLOT
    zblk = D_CONV // hd
    st_in, st_out, st_shape = _state_specs(layer, prev, s0_all.shape, nseq)
    return pl.pallas_call(
        functools.partial(_gdn_scan_decode_kernel, n_prev=len(prev)),
        grid=(n,),
        in_specs=[
            pl.BlockSpec((1, 1, 2 * c, hd), lambda i: (0, i, 0, 0)),
            pl.BlockSpec((1, 1, c, hd), lambda i: (0, i, 0, 0)),
            pl.BlockSpec((1, 1, N_HEADS, c, c), lambda i: (0, i, 0, 0, 0)),
            pl.BlockSpec((1, 1, hd, c), lambda i: (0, i, 0, 0)),
            pl.BlockSpec((1, 1, 8, hd), lambda i: (0, i, 0, 0)),
            pl.BlockSpec((c, hd), lambda i: (i, zblk)),
            _layer_spec(layer, 1, DH),
        ] + st_in,
        out_specs=[pl.BlockSpec((c, hd), lambda i: (i, 0)), st_out],
        out_shape=[jax.ShapeDtypeStruct((n * c, hd), BF16), st_shape],
        compiler_params=_cparams(1),
        name="gdn_scan_decode",
    )(wkqg, wv, qk, kdt, el, proj, gn, s0_all, *prev)


def _mlstm_gates(gcol, growk, pa_ref, ra_ref, h, rows, valid_lo, rowpos, colpos):
    xg = gcol + pa_ref[0:1, :]
    ig_col = xg[:, 2 * N_HEADS + h:2 * N_HEADS + h + 1]
    fl_col = -_softplus(-xg[:, 3 * N_HEADS + h:3 * N_HEADS + h + 1])
    xr = growk + ra_ref[:, 0:1]
    ig_row = xr[2 * N_HEADS + h:2 * N_HEADS + h + 1, :]
    fl_row = -_softplus(-xr[3 * N_HEADS + h:3 * N_HEADS + h + 1, :])
    if valid_lo > 0:
        ig_col = jnp.where(rowpos >= valid_lo, ig_col, NEG)
        fl_col = jnp.where(rowpos >= valid_lo, fl_col, 0.0)
        ig_row = jnp.where(colpos >= valid_lo, ig_row, NEG)
        fl_row = jnp.where(colpos >= valid_lo, fl_row, 0.0)
    return ig_col, fl_col, ig_row, fl_row


def _mlstm_chunks(probs, seq_len):
    c = CHUNK
    nseq = c // seq_len
    ri, ci, same = _chunk_masks(seq_len)
    causal = (ri >= ci) & same
    upper = (ri <= ci) & same
    seq_end = ci == (ri | (seq_len - 1))
    rowseq = lax.broadcasted_iota(jnp.int32, (c, 1), 0) >> (seq_len.bit_length() - 1)
    zpad = jnp.zeros((16 - seq_len, DH), F32) if nseq > 1 else None
    for p in probs:
        p["f_col"] = jnp.sum(jnp.where(causal, p["fl_row"], 0.0), axis=1, keepdims=True)
        p["f_row"] = jnp.sum(jnp.where(upper, p["fl_col"], 0.0), axis=0, keepdims=True)
        p["qn"] = jnp.sum(p["q"] * p["n_rows"], axis=1, keepdims=True)
        p["km"] = p["k"] * (DH ** -0.5)
        p["qb"] = p["q"].astype(BF16)
        p["kb"] = p["km"].astype(BF16)
        p["vb"] = p["v"].astype(BF16)
    for p in probs:
        p["b_col"] = p["ig_col"] - p["f_col"]
        p["b_row"] = p["ig_row"] - p["f_row"]
        p["bmax_col"] = jnp.max(jnp.where(causal, p["b_row"], -jnp.inf), axis=1, keepdims=True)
        if nseq > 1:
            p["fl_end"] = jnp.sum(jnp.where(seq_end, p["f_row"], 0.0), axis=1, keepdims=True)
            p["bmax_end"] = jnp.max(jnp.where(same, p["b_row"], -jnp.inf), axis=1, keepdims=True)
    for p in probs:
        f_col, mp_col = p["f_col"], p["mp_col"]
        m_col = f_col + jnp.maximum(mp_col, p["bmax_col"])
        p["m_col"] = m_col
        p["a_col"] = jnp.exp(f_col + mp_col - m_col)
        p["dexp"] = jnp.where(causal, jnp.exp(jnp.where(causal, (f_col + p["b_row"]) - m_col, 0.0)), 0.0)
        if nseq == 1:
            fl_end, ml, mp = f_col[c - 1:c], m_col[c - 1:c], mp_col[c - 1:c]
            p["ml_col"] = jnp.broadcast_to(ml, (c, 1))
        else:
            fl_end, mp = p["fl_end"], mp_col
            ml = fl_end + jnp.maximum(mp_col, p["bmax_end"])
            p["ml_col"] = ml
        p["al"] = jnp.broadcast_to(jnp.exp(fl_end + mp - ml), (c, DH))
        p["kw"] = p["km"] * jnp.exp(fl_end + p["b_col"] - ml)
    for p in probs:
        p["kwt"] = p["kw"].T.astype(BF16)
    for p in probs:
        p["qk"] = _dot_nt(p["qb"], p["kb"])
    for p in probs:
        if nseq == 1:
            p["qc"] = _dot(p["qb"], p["c_list"][0].astype(BF16))
            p["upd"] = [_dot(p["kwt"], p["vb"])]
        else:
            parts, upd = [], []
            for j in range(nseq):
                qj = jnp.concatenate([p["q"][seq_len * j:seq_len * (j + 1)], zpad], axis=0).astype(BF16)
                parts.append(_dot(qj, p["c_list"][j].astype(BF16))[0:seq_len])
                upd.append(_dot(p["kwt"], jnp.where(rowseq == j, p["v"], 0.0).astype(BF16)))
            p["qc"] = jnp.concatenate(parts, axis=0)
            p["upd"] = upd
    for p in probs:
        p["w"] = p["dexp"] * p["qk"]
        p["wsum"] = jnp.sum(p["w"], axis=1, keepdims=True)
    for p in probs:
        p["wv"] = _dot(p["w"].astype(BF16), p["vb"])
    for p in probs:
        num = p["a_col"] * p["qc"] + p["wv"]
        den = p["a_col"] * p["qn"] + p["wsum"]
        p["hout"] = num / jnp.maximum(jnp.abs(den), jnp.exp(-p["m_col"]))
        p["new_c"] = [p["al"][seq_len * j:seq_len * j + 1] * p["c_list"][j] + p["upd"][j]
                      for j in range(nseq)]
    return probs


def _mlstm_out_store(hout, og, gnorm, mix_ref, idx):
    mix_ref[idx] = _rms(hout * jax.nn.sigmoid(og), gnorm).astype(BF16)


def _mlstm_prompt_kernel(q_ref, k_ref, v_ref, o_ref, gcol_ref, grow_ref, pa_ref, ra_ref, nrm_ref,
                         mix_ref, cout_ref, nout_ref, mout_ref, c_scr, n_scr, m_scr, *, nc, bs):
    c = CHUNK

    @pl.when(pl.program_id(1) == 0)
    def _():
        c_scr[...] = jnp.zeros_like(c_scr)
        n_scr[...] = jnp.zeros_like(n_scr)
        m_scr[...] = jnp.zeros_like(m_scr)

    def body(n, carry):
        r0 = pl.multiple_of(n * c, c)
        rows = pl.ds(r0, c)
        probs = []
        for b in range(bs):
            gcol = gcol_ref[b, rows, :]
            growk = grow_ref[b, n]
            for h in range(N_HEADS):
                hs = slice(DH * h, DH * (h + 1))
                ig_col, fl_col, ig_row, fl_row = _mlstm_gates(gcol, growk, pa_ref, ra_ref, h, c, 0, None, None)
                st = b * N_HEADS + h
                probs.append(dict(b=b, hs=hs, st=st, q=q_ref[b, rows, hs], k=k_ref[b, rows, hs],
                                  v=v_ref[b, rows, hs], ig_col=ig_col, fl_col=fl_col, ig_row=ig_row,
                                  fl_row=fl_row, mp_col=m_scr[st][:, 0:1], n_rows=n_scr[st],
                                  c_list=[c_scr[st]]))
        for p in _mlstm_chunks(probs, c):
            st = p["st"]
            c_scr[st] = p["new_c"][0]
            n_scr[st] = jnp.broadcast_to(
                p["al"][0:1] * p["n_rows"][0:1] + jnp.sum(p["kw"], axis=0, keepdims=True), (c, DH))
            m_scr[st] = jnp.broadcast_to(p["ml_col"], (c, DH))
            _mlstm_out_store(p["hout"], o_ref[p["b"], rows, p["hs"]], nrm_ref[:, p["hs"]], mix_ref,
                             (p["b"], rows, p["hs"]))
        return carry

    lax.fori_loop(0, nc, body, 0)

    @pl.when(pl.program_id(1) == pl.num_programs(1) - 1)
    def _():
        lane = lax.broadcasted_iota(jnp.int32, (8, DH), 1)
        for b in range(bs):
            mo = jnp.zeros((8, DH), F32)
            for h in range(N_HEADS):
                st = b * N_HEADS + h
                cout_ref[b, h] = c_scr[st]
                nout_ref[b, h:h + 1, :] = n_scr[st][0:1]
                mo = jnp.where(lane == h, m_scr[st][0:8], mo)
            mout_ref[b] = mo


def _mlstm_prompt(proj3, grow4, pa, ra, nrm, layer, *, ts, bs):
    nb, t, _ = proj3.shape
    c = CHUNK
    hd = N_HEADS * DH
    nc = ts // c
    qblk = (D_CONV + hd) // hd
    return pl.pallas_call(
        functools.partial(_mlstm_prompt_kernel, nc=nc, bs=bs),
        grid=(nb // bs, t // ts),
        in_specs=[
            pl.BlockSpec((bs, ts, hd), lambda b, i: (b, i, qblk)),
            pl.BlockSpec((bs, ts, hd), lambda b, i: (b, i, qblk + 1)),
            pl.BlockSpec((bs, ts, hd), lambda b, i: (b, i, qblk + 2)),
            pl.BlockSpec((bs, ts, hd), lambda b, i: (b, i, qblk + 3)),
            pl.BlockSpec((bs, ts, LANES), lambda b, i: (b, i, GATE_COL_BLOCK)),
            pl.BlockSpec((bs, nc, N_GATE, c), lambda b, i: (b, i, 0, 0)),
            _layer_spec(layer, 8, LANES),
            _layer_spec(layer, N_GATE, LANES),
            _layer_spec(layer, 1, hd),
        ],
        out_specs=[
            pl.BlockSpec((bs, ts, hd), lambda b, i: (b, i, 0)),
            pl.BlockSpec((bs, N_HEADS, DH, DH), lambda b, i: (b, 0, 0, 0)),
            pl.BlockSpec((bs, N_HEADS, DH), lambda b, i: (b, 0, 0)),
            pl.BlockSpec((bs, 8, DH), lambda b, i: (b, 0, 0)),
        ],
        out_shape=[jax.ShapeDtypeStruct((nb, t, hd), BF16),
                   jax.ShapeDtypeStruct((nb, N_HEADS, DH, DH), F32),
                   jax.ShapeDtypeStruct((nb, N_HEADS, DH), F32),
                   jax.ShapeDtypeStruct((nb, 8, DH), F32)],
        scratch_shapes=[pltpu.VMEM((bs * N_HEADS, DH, DH), F32),
                        pltpu.VMEM((bs * N_HEADS, c, DH), F32),
                        pltpu.VMEM((bs * N_HEADS, c, DH), F32)],
        compiler_params=_cparams(2),
        name="mlstm_prompt",
    )(proj3, proj3, proj3, proj3, proj3, grow4, pa, ra, nrm)


def _mlstm_decode_kernel(*refs, n_prev):
    (q_ref, k_ref, v_ref, o_ref, gcol_ref, grow_ref, pa_ref, ra_ref, nrm_ref, n0_ref, m0_ref,
     c0_ref), rest = refs[:12], refs[12:]
    prev_refs, (mix_ref, nout_ref, mout_ref, cout_ref) = rest[:n_prev], rest[n_prev:]
    if n_prev:
        for l, p_ref in enumerate(prev_refs):
            cout_ref[l] = p_ref[...]
        cout_ref = cout_ref.at[n_prev]
    c = CHUNK
    nseq = c // SLOT
    rowpos = lax.broadcasted_iota(jnp.int32, (c, 1), 0) & (SLOT - 1)
    colpos = lax.broadcasted_iota(jnp.int32, (1, c), 1) & (SLOT - 1)
    lane = lax.broadcasted_iota(jnp.int32, (c, DH), 1)
    gcol = gcol_ref[...]
    growk = grow_ref[0]
    probs = []
    for h in range(N_HEADS):
        hs = slice(DH * h, DH * (h + 1))
        ig_col, fl_col, ig_row, fl_row = _mlstm_gates(gcol, growk, pa_ref, ra_ref, h, c, SLOT_PAD, rowpos, colpos)
        n_rows = jnp.concatenate(
            [jnp.broadcast_to(n0_ref[j, h:h + 1, :], (SLOT, DH)) for j in range(nseq)], axis=0)
        probs.append(dict(h=h, hs=hs, q=q_ref[:, hs], k=k_ref[:, hs], v=v_ref[:, hs], ig_col=ig_col,
                          fl_col=fl_col, ig_row=ig_row, fl_row=fl_row, mp_col=m0_ref[:, h:h + 1],
                          n_rows=n_rows, c_list=[c0_ref[j, h] for j in range(nseq)]))
    mo = jnp.zeros((c, DH), F32)
    for p in _mlstm_chunks(probs, SLOT):
        h, hs = p["h"], p["hs"]
        for j in range(nseq):
            cout_ref[j, h] = p["new_c"][j]
            rs = slice(SLOT * j, SLOT * (j + 1))
            nout_ref[j, h:h + 1, :] = (p["al"][SLOT * j:SLOT * j + 1] * n0_ref[j, h:h + 1, :]
                                       + jnp.sum(p["kw"][rs], axis=0, keepdims=True))
        mo = jnp.where(lane == h, jnp.broadcast_to(p["ml_col"], (c, DH)), mo)
        _mlstm_out_store(p["hout"], o_ref[:, hs], nrm_ref[:, hs], mix_ref, (slice(None), hs))
    mout_ref[...] = mo


def _mlstm_decode(proj, grow, pa, ra, nrm, c0_all, n0_all, m0rows_all, prev, layer):
    rows = proj.shape[0]
    c = CHUNK
    hd = N_HEADS * DH
    nseq = c // SLOT
    qblk = (D_CONV + hd) // hd
    st_in, st_out, st_shape = _state_specs(layer, prev, c0_all.shape, nseq)
    return pl.pallas_call(
        functools.partial(_mlstm_decode_kernel, n_prev=len(prev)),
        grid=(rows // c,),
        in_specs=[
            pl.BlockSpec((c, hd), lambda i: (i, qblk)),
            pl.BlockSpec((c, hd), lambda i: (i, qblk + 1)),
            pl.BlockSpec((c, hd), lambda i: (i, qblk + 2)),
            pl.BlockSpec((c, hd), lambda i: (i, qblk + 3)),
            pl.BlockSpec((c, LANES), lambda i: (i, GATE_COL_BLOCK)),
            pl.BlockSpec((1, N_GATE, c), lambda i: (i, 0, 0)),
            _layer_spec(layer, 8, LANES),
            _layer_spec(layer, N_GATE, LANES),
            _layer_spec(layer, 1, hd),
            pl.BlockSpec((None, nseq, N_HEADS, DH), lambda i: (layer, i, 0, 0)),
            pl.BlockSpec((None, c, LANES), lambda i: (layer, i, 0)),
        ] + st_in,
        out_specs=[
            pl.BlockSpec((c, hd), lambda i: (i, 0)),
            pl.BlockSpec((nseq, N_HEADS, DH), lambda i: (i, 0, 0)),
            pl.BlockSpec((c, LANES), lambda i: (i, 0)),
            st_out,
        ],
        out_shape=[jax.ShapeDtypeStruct((rows, hd), BF16),
                   jax.ShapeDtypeStruct(n0_all.shape[1:], F32),
                   jax.ShapeDtypeStruct((rows, LANES), F32),
                   st_shape],
        compiler_params=_cparams(1),
        name="mlstm_decode",
    )(proj, proj, proj, proj, proj, grow, pa, ra, nrm, n0_all, m0rows_all, c0_all, *prev)


def _post_kernel(x_ref, ma_ref, mb_ref, p_ref, woa_ref, wob_ref, gf_ref, wg_ref, wu_ref, wd_ref,
                 gp_ref, wpg_ref, wpp_ref, gfin_ref, o_ref, acc_ref, *, final):
    x = x_ref[...] + (_dot(ma_ref[...], woa_ref[...]) + _dot(mb_ref[...], wob_ref[...]))
    ub = _rms(x, gf_ref[...]).astype(BF16)
    for j in range(D_FF // FF_CHUNK):
        sl = slice(j * FF_CHUNK, (j + 1) * FF_CHUNK)
        a = (jax.nn.silu(_dot(ub, wg_ref[:, sl])) * _dot(ub, wu_ref[:, sl])).astype(BF16)
        d = _dot(a, wd_ref[sl, :])
        if j == 0:
            acc_ref[...] = d
        else:
            acc_ref[...] += d
    x = x + acc_ref[...]
    gate = jax.nn.sigmoid(_dot(_rms(x, gp_ref[...]).astype(BF16), wpg_ref[...]))
    x = x + _dot(p_ref[...].astype(BF16), wpp_ref[...]) * gate
    if final:
        x = _rms(x, gfin_ref[...])
    o_ref[...] = x


def _post(x, ma, mb, p_all, wo, gf, wg, wu, wd, gp, wpg, wpp, gfin, layer, *, final):
    rows = x.shape[0]
    tm = ROW_TILE
    hd = N_HEADS * DH
    row = lambda w: pl.BlockSpec((tm, w), lambda i: (i, 0))

    def whole(a, b, blk=0):
        return pl.BlockSpec((None, a, b), lambda i: (layer, blk, 0), pipeline_mode=pl.Buffered(1))

    return pl.pallas_call(
        functools.partial(_post_kernel, final=final),
        grid=(rows // tm,),
        in_specs=[row(D_MODEL), row(hd), row(hd),
                  pl.BlockSpec((None, tm, D_PLE), lambda i: (layer, i, 0)),
                  whole(hd, D_MODEL, 0), whole(hd, D_MODEL, 1), whole(1, D_MODEL),
                  whole(D_MODEL, D_FF), whole(D_MODEL, D_FF), whole(D_FF, D_MODEL),
                  whole(1, D_MODEL), whole(D_MODEL, D_MODEL), whole(D_PLE, D_MODEL),
                  pl.BlockSpec((1, D_MODEL), lambda i: (0, 0))],
        out_specs=row(D_MODEL),
        out_shape=jax.ShapeDtypeStruct((rows, D_MODEL), F32),
        scratch_shapes=[pltpu.VMEM((tm, D_MODEL), F32)],
        compiler_params=_cparams(1),
        name="post",
    )(x, ma, mb, p_all, wo, wo, gf, wg, wu, wd, gp, wpg, wpp, gfin)


def _gate_vec(pairs, depth):
    v = jnp.zeros((depth, N_GATE), F32)
    for off, val in pairs:
        v = v.at[:, off:off + N_HEADS].set(val.astype(F32))
    return v


def _lane_form(v):
    d = v.shape[0]
    return jnp.broadcast_to(jnp.pad(v, ((0, 0), (0, LANES - N_GATE)))[:, None, :], (d, 8, LANES))


def _row_form(v):
    d = v.shape[0]
    return jnp.broadcast_to(v[:, :, None], (d, N_GATE, LANES))


def kernel(x_prompt, x_sample, p_prompt, p_sample, state_gdn, state_gdn_conv, state_mlstm_C, state_mlstm_n, state_mlstm_m, w_in, conv_w, gdn_a_log, gdn_dt_bias, gdn_norm, mlstm_i_bias, mlstm_f_bias, mlstm_norm, w_out, norm_mix, norm_ffn, w_gate, w_up, w_down, norm_ple, w_ple_gate, w_ple_proj, norm_final):
    depth = w_in.shape[0]
    nb, t, _ = x_prompt.shape
    ns, tdec, _ = x_sample.shape
    hd = N_HEADS * DH
    c = CHUNK
    pad = SLOT - tdec

    xp = x_prompt.reshape(nb * t, D_MODEL)
    xs = jnp.pad(x_sample, ((0, 0), (pad, 0), (0, 0))).reshape(ns * SLOT, D_MODEL)
    ps_all = jnp.pad(p_sample, ((0, 0), (0, 0), (pad, 0), (0, 0))).reshape(depth, ns * SLOT, D_PLE)
    pp_all = p_prompt.reshape(depth, nb * t, D_PLE)
    gfin = norm_final.reshape(1, D_MODEL)

    o = D_CONV + hd
    w_g = jnp.concatenate([w_in[:, :, o:o + 2 * N_HEADS], w_in[:, :, o + 2 * N_HEADS + 4 * hd:]], axis=2)
    w_r = jnp.concatenate([w_in[:, :, :o], w_in[:, :, o + 2 * N_HEADS:o + 2 * N_HEADS + 4 * hd], w_g,
                           jnp.zeros((depth, D_MODEL, LANES - N_GATE), F32)], axis=2).astype(BF16)
    w_gt = jnp.swapaxes(w_g, 1, 2).astype(BF16)
    adds = _gate_vec([(N_HEADS, gdn_dt_bias), (2 * N_HEADS, mlstm_i_bias), (3 * N_HEADS, mlstm_f_bias)], depth)
    alog = _gate_vec([(N_HEADS, gdn_a_log)], depth)
    pa, pl_, ra, rl = _lane_form(adds), _lane_form(alog), _row_form(adds), _row_form(alog)
    g_mix = norm_mix.reshape(depth, 1, D_MODEL)
    cw = conv_w.astype(F32)
    gn = gdn_norm.reshape(depth, 1, DH).astype(F32)
    nrm = mlstm_norm.reshape(depth, 1, hd).astype(F32)
    wts = (w_out.astype(BF16), norm_ffn.reshape(depth, 1, D_MODEL), w_gate.astype(BF16), w_up.astype(BF16),
           w_down.astype(BF16), norm_ple.reshape(depth, 1, D_MODEL), w_ple_gate.astype(BF16),
           w_ple_proj.astype(BF16), gfin)
    cprev = jnp.pad(state_gdn_conv.astype(F32), ((0, 0), (0, 0), (pad - (CONV_W - 1), SLOT - pad), (0, 0)))
    cprev = cprev.reshape(depth, ns * SLOT, D_CONV)
    m0rows = jnp.pad(jnp.repeat(state_mlstm_m.astype(F32), SLOT, axis=1), ((0, 0), (0, 0), (0, LANES - N_HEADS)))
    s0_all, c0_all, n0_all = state_gdn.astype(F32), state_mlstm_C.astype(F32), state_mlstm_n.astype(F32)

    outs_p = [[] for _ in range(5)]
    outs_s = [[] for _ in range(5)]
    for i in range(depth):
        final = i == depth - 1
        proj_p, gt_p = _in_proj(xp, g_mix, w_r, w_gt, i)
        proj_s, gt_s = _in_proj(xs, g_mix, w_r, w_gt, i)
        proj_p3 = proj_p.reshape(nb, t, N_PROJ)
        proj_s3 = proj_s.reshape(1, ns * SLOT, N_PROJ)
        grow_p = gt_p.reshape(N_GATE, nb * t // c, c).transpose(1, 0, 2)
        grow_s = gt_s.reshape(N_GATE, ns * SLOT // c, c).transpose(1, 0, 2)

        prep_p = _gdn_prep(proj_p3, grow_p, None, cw, pa, pl_, ra, rl, i, seq_len=c, valid_lo=0, idt=BF16, tb=256)
        mixa_p, s_p = _gdn_scan_prompt(*prep_p, proj_p3, gn, i, ts=512, bs=2)
        prep_s = _gdn_prep(proj_s3, grow_s, cprev, cw, pa, pl_, ra, rl, i, seq_len=SLOT, valid_lo=pad, idt=F32, tb=c)
        mixa_s, s_s = _gdn_scan_decode(*prep_s, proj_s, gn, s0_all, outs_s[0] if final else [], i)

        mixb_p, c_p, n_p, m_p = _mlstm_prompt(proj_p3, grow_p.reshape(nb, t // c, N_GATE, c), pa, ra, nrm, i,
                                              ts=512, bs=4)
        mixb_s, n_s, m_s, c_s = _mlstm_decode(proj_s, grow_s, pa, ra, nrm, c0_all, n0_all, m0rows,
                                              outs_s[2] if final else [], i)

        xp = _post(xp, mixa_p.reshape(nb * t, hd), mixb_p.reshape(nb * t, hd), pp_all, *wts, i, final=final)
        xs = _post(xs, mixa_s, mixb_s, ps_all, *wts, i, final=final)

        outs_p[0].append(s_p)
        outs_p[1].append(proj_p3[:, t - (CONV_W - 1):, :D_CONV])
        outs_p[2].append(c_p)
        outs_p[3].append(n_p)
        outs_p[4].append(m_p[:, 0, :N_HEADS])
        outs_s[0].append(s_s)
        outs_s[1].append(proj_s.reshape(ns, SLOT, N_PROJ)[:, SLOT - (CONV_W - 1):, :D_CONV])
        outs_s[2].append(c_s)
        outs_s[3].append(n_s)
        outs_s[4].append(m_s.reshape(ns, SLOT, LANES)[:, SLOT - 1, :N_HEADS])

    y_prompt = xp.reshape(nb, t, D_MODEL)
    y_sample = xs.reshape(ns, SLOT, D_MODEL)[:, pad:, :]
    sp = [jnp.stack(a, axis=0) for a in outs_p]
    ss = [outs_s[j][-1] if j in (0, 2) else jnp.stack(outs_s[j], axis=0) for j in range(5)]
    return (y_prompt, y_sample, *sp, *ss)
```

```python
import functools

import jax
import jax.numpy as jnp
from jax import lax
from jax.experimental import pallas as pl
from jax.experimental.pallas import tpu as pltpu

F32 = jnp.float32
BF16 = jnp.bfloat16

D_MODEL = 1024
N_HEADS = 4
DH = 128
D_CONV = 3 * N_HEADS * DH
D_FF = 2816
D_PLE = 256
CONV_W = 4
CHUNK = 64
EPS = 1e-6
NEG = -1e30
N_GATE = 16
LANES = 128
N_PROJ = D_CONV + 5 * N_HEADS * DH + LANES
GATE_COL_BLOCK = (N_PROJ - LANES) // LANES
SLOT = 8
SLOT_PAD = 4
V7X_VMEM_LIMIT_BYTES = 56 * 1024 * 1024
ROW_TILE = 512
FF_CHUNK = 256


def _cparams(n_axes):
    return pltpu.CompilerParams(dimension_semantics=("arbitrary",) * n_axes,
                                vmem_limit_bytes=V7X_VMEM_LIMIT_BYTES)


def _rms(x, g):
    return x * lax.rsqrt(jnp.mean(x * x, axis=-1, keepdims=True) + EPS) * g


def _softplus(x):
    return jnp.maximum(x, 0.0) + jnp.log1p(jnp.exp(-jnp.abs(x)))


def _dot(a, b):
    return jnp.dot(a, b, preferred_element_type=F32)


def _dot_nt(a, b):
    return lax.dot_general(a, b, (((1,), (1,)), ((), ())), preferred_element_type=F32)


def _split(a):
    hi = a.astype(BF16)
    lo = (a - hi.astype(F32)).astype(BF16)
    return hi, lo


def _split_all(xs):
    return [_split(x) for x in xs]


def _mm3_all(a_list, b_list):
    out = []
    for (ah, al), (bh, bl) in zip(a_list, b_list):
        m = ah.shape[0]
        r = _dot(jnp.concatenate([ah, al], axis=0), bh)
        out.append(r[0:m] + (_dot(ah, bl) + r[m:2 * m]))
    return out


class _PairMasks:
    def __init__(self, seq_len):
        c = CHUNK
        self.r = lax.broadcasted_iota(jnp.int32, (c, 2 * c), 0)
        lane = lax.broadcasted_iota(jnp.int32, (c, 2 * c), 1)
        self.cc = lane & (c - 1)
        self.lo = lane < c
        if seq_len < c:
            sh = seq_len.bit_length() - 1
            same = (self.r >> sh) == (self.cc >> sh)
        else:
            same = self.r >= 0
        self.causal = (self.r >= self.cc) & same
        self.strict_f = jnp.where((self.r > self.cc) & same, 1.0, 0.0)
        self.eye = jnp.where(self.r == self.cc, 1.0, 0.0)
        self.lo_b = jnp.where(self.lo, 1.0, 0.0).astype(BF16)
        self.hi_b = jnp.where(self.lo, 0.0, 1.0).astype(BF16)

    def blockdiag(self, sp):
        return tuple(jnp.concatenate([x * self.lo_b, x * self.hi_b], axis=0) for x in sp)


def _unit_lower_inverse_pairs(l_list, pm, seq_len):
    def bd_all(sps):
        return [pm.blockdiag(sp) for sp in sps]

    blk8 = jnp.where((pm.r >> 3) == (pm.cc >> 3), 1.0, 0.0)
    n0 = [l * blk8 for l in l_list]
    n0s = _split_all(n0)
    n2 = _mm3_all(n0s, bd_all(n0s))
    n2s = _split_all(n2)
    n4 = _mm3_all(n2s, bd_all(n2s))
    p = _mm3_all(_split_all([pm.eye - a for a in n0]), bd_all(_split_all([pm.eye + a for a in n2])))
    d = _mm3_all(_split_all(p), bd_all(_split_all([pm.eye + a for a in n4])))
    ds = _split_all(d)
    s = 8
    while s < seq_len:
        sh = s.bit_length() - 1
        off = jnp.where(((pm.r >> (sh + 1)) == (pm.cc >> (sh + 1))) & ((pm.r >> sh) != (pm.cc >> sh)), 1.0, 0.0)
        de = _mm3_all(ds, bd_all(_split_all([l * off for l in l_list])))
        ded = _mm3_all(_split_all(de), bd_all(ds))
        d = [a - b for a, b in zip(d, ded)]
        ds = _split_all(d)
        s *= 2
    return ds


def _in_proj_kernel(x_ref, g_ref, w_ref, wgt_ref, proj_ref, gt_ref):
    hb = _rms(x_ref[...], g_ref[...]).astype(BF16)
    proj_ref[...] = _dot(hb, w_ref[...])
    gt_ref[...] = _dot_nt(wgt_ref[...], hb)


def _layer_spec(layer, *shape):
    zeros = (0,) * len(shape)
    return pl.BlockSpec((None,) + shape, lambda *_: (layer,) + zeros)


def _in_proj(x, g, w, wgt, layer):
    rows = x.shape[0]
    tm = ROW_TILE
    return pl.pallas_call(
        _in_proj_kernel,
        grid=(rows // tm,),
        in_specs=[
            pl.BlockSpec((tm, D_MODEL), lambda i: (i, 0)),
            _layer_spec(layer, 1, D_MODEL),
            _layer_spec(layer, D_MODEL, N_PROJ),
            _layer_spec(layer, N_GATE, D_MODEL),
        ],
        out_specs=[
            pl.BlockSpec((tm, N_PROJ), lambda i: (i, 0)),
            pl.BlockSpec((N_GATE, tm), lambda i: (0, i)),
        ],
        out_shape=[jax.ShapeDtypeStruct((rows, N_PROJ), F32),
                   jax.ShapeDtypeStruct((N_GATE, rows), F32)],
        compiler_params=_cparams(1),
        name="in_proj",
    )(x, g, w, wgt)


def _chunk_masks(seq_len):
    c = CHUNK
    ri = lax.broadcasted_iota(jnp.int32, (c, c), 0)
    ci = lax.broadcasted_iota(jnp.int32, (c, c), 1)
    if seq_len < c:
        sh = seq_len.bit_length() - 1
        same = (ri >> sh) == (ci >> sh)
    else:
        same = ri >= 0
    return ri, ci, same


def _gdn_prep_kernel(*refs, tb, seq_len, valid_lo, idt):
    has_cprev = valid_lo > 0
    if has_cprev:
        (u_ref, prev_ref, gcol_ref, grow_ref, cprev_ref, cw_ref, pa_ref, pl_ref, ra_ref, rl_ref,
         wkqg_ref, wv_ref, qk_ref, kdt_ref, el_ref) = refs
    else:
        (u_ref, prev_ref, gcol_ref, grow_ref, cw_ref, pa_ref, pl_ref, ra_ref, rl_ref,
         wkqg_ref, wv_ref, qk_ref, kdt_ref, el_ref) = refs
    c = CHUNK
    sh = seq_len.bit_length() - 1
    u = u_ref[0]
    rowpos = lax.broadcasted_iota(jnp.int32, (tb, 1), 0) & (seq_len - 1)
    if has_cprev:
        u = jnp.where((rowpos >= valid_lo - (CONV_W - 1)) & (rowpos < valid_lo), cprev_ref[0], u)
    xp = jnp.concatenate([jnp.where(pl.program_id(1) == 0, 0.0, prev_ref[0]), u], axis=0)
    cw = cw_ref[...]
    y = None
    for i in range(CONV_W):
        s = CONV_W - 1 - i
        xs = u if s == 0 else pltpu.roll(xp, s, 0)[8:8 + tb]
        t = xs * cw[i:i + 1, :]
        y = t if y is None else y + t
    y = jax.nn.silu(y)

    lane = lax.broadcasted_iota(jnp.int32, (1, LANES), 1)
    xg = gcol_ref[0] + pa_ref[0:1, :]
    gcolv = jnp.where(lane < N_HEADS, jax.nn.sigmoid(xg), -jnp.exp(pl_ref[0:1, :]) * _softplus(xg))
    if valid_lo > 0:
        gcolv = jnp.where(rowpos >= valid_lo, gcolv, 0.0)

    ri, ci, same = _chunk_masks(seq_len)
    causal = (ri >= ci) & same
    upper = (ri <= ci) & same
    colpos = lax.broadcasted_iota(jnp.int32, (1, c), 1) & (seq_len - 1)
    r8 = lax.broadcasted_iota(jnp.int32, (8, c), 0)
    c8 = lax.broadcasted_iota(jnp.int32, (8, c), 1)

    items = []
    for k in range(tb // c):
        r0 = k * c
        growv = -jnp.exp(rl_ref[:, 0:1]) * _softplus(grow_ref[k] + ra_ref[:, 0:1])
        if valid_lo > 0:
            growv = jnp.where(colpos >= valid_lo, growv, 0.0)
        for h in range(N_HEADS):
            qc = y[r0:r0 + c, DH * h:DH * (h + 1)]
            kc = y[r0:r0 + c, N_HEADS * DH + DH * h:N_HEADS * DH + DH * (h + 1)]
            g_col = gcolv[r0:r0 + c, N_HEADS + h:N_HEADS + h + 1]
            g_row = growv[N_HEADS + h:N_HEADS + h + 1, :]
            items.append(dict(
                k=k, h=h, hs=slice(DH * h, DH * (h + 1)), qc=qc, kc=kc,
                vc=y[r0:r0 + c, 2 * N_HEADS * DH + DH * h:2 * N_HEADS * DH + DH * (h + 1)],
                beta=gcolv[r0:r0 + c, h:h + 1],
                qss=jnp.sum(qc * qc, axis=-1, keepdims=True),
                kss=jnp.sum(kc * kc, axis=-1, keepdims=True),
                gc_col=jnp.sum(jnp.where(causal, g_row, 0.0), axis=1, keepdims=True),
                gc_row=jnp.sum(jnp.where(upper, g_col, 0.0), axis=0, keepdims=True),
                gl_col=jnp.sum(jnp.where(same, g_row, 0.0), axis=1, keepdims=True),
                gl8=jnp.sum(jnp.where((c8 >> sh) == r8, g_row, 0.0), axis=1, keepdims=True)))
    for it in items:
        k, hs, gc_col, beta = it["k"], it["hs"], it["gc_col"], it["beta"]
        qn = it["qc"] * lax.rsqrt(it["qss"] + EPS) * (DH ** -0.5)
        kn = it["kc"] * lax.rsqrt(it["kss"] + EPS)
        wkqg_ref[0, k, c:2 * c, hs] = (qn * jnp.exp(gc_col)).astype(idt)
        el_ref[0, k, :, hs] = jnp.broadcast_to(jnp.exp(it["gl8"]), (8, DH))
        it["kd"] = kn * jnp.exp(it["gl_col"] - gc_col)
        it["qb"] = qn.astype(BF16)
        it["kb"] = kn.astype(BF16)
        it["rhs"] = jnp.concatenate([beta * it["vc"], (beta * jnp.exp(gc_col)) * kn], axis=1)
    for it in items:
        kdt_ref[0, it["k"], it["hs"], :] = it["kd"].T.astype(idt)
    pm = _PairMasks(seq_len)
    zb = jnp.zeros((c, DH), BF16)
    pairs = []
    for a, b in zip(items[0::2], items[1::2]):
        gcp = jnp.where(pm.lo, a["gc_col"], b["gc_col"])
        grp = jnp.concatenate([a["gc_row"], b["gc_row"]], axis=1)
        decay = jnp.where(pm.causal, jnp.exp(jnp.where(pm.causal, gcp - grp, 0.0)), 0.0)
        kbd = jnp.concatenate([jnp.concatenate([a["kb"], zb], axis=1),
                               jnp.concatenate([zb, b["kb"]], axis=1)], axis=0)
        kq = jnp.concatenate([jnp.concatenate([a["kb"], b["kb"]], axis=1),
                              jnp.concatenate([a["qb"], b["qb"]], axis=1)], axis=0)
        pairs.append(dict(a=a, b=b, decay=decay, kbd=kbd, kq=kq, beta=jnp.where(pm.lo, a["beta"], b["beta"])))
    kqs = [_dot_nt(p["kq"], p["kbd"]) for p in pairs]
    for p, kq in zip(pairs, kqs):
        qk_ref[0, p["a"]["k"], p["a"]["h"] // 2] = (kq[c:2 * c] * p["decay"]).astype(idt)
    ls = [p["beta"] * kq[0:c] * p["decay"] * pm.strict_f for p, kq in zip(pairs, kqs)]
    tinvs = _unit_lower_inverse_pairs(ls, pm, seq_len)
    for p, (th, tl) in zip(pairs, tinvs):
        rh, rl_ = _split(jnp.concatenate([p["a"]["rhs"], p["b"]["rhs"]], axis=0))
        ta_h, tb_h = th * pm.lo_b, th * pm.hi_b
        r1 = _dot(jnp.concatenate([ta_h, tl * pm.lo_b, tb_h, tl * pm.hi_b], axis=0), rh)
        r2 = _dot(jnp.concatenate([ta_h, tb_h], axis=0), rl_)
        p["sol"] = (r1[0:c] + (r2[0:c] + r1[c:2 * c]), r1[2 * c:3 * c] + (r2[c:2 * c] + r1[3 * c:4 * c]))
    for p in pairs:
        for it, sol in zip((p["a"], p["b"]), p["sol"]):
            wv_ref[0, it["k"], :, it["hs"]] = sol[:, 0:DH]
            wkqg_ref[0, it["k"], 0:c, it["hs"]] = sol[:, DH:2 * DH].astype(idt)


def _gdn_prep(proj3, grow, cprev, cw, pa, pl_, ra, rl, layer, *, seq_len, valid_lo, idt, tb):
    nb, t, _ = proj3.shape
    c = CHUNK
    nt = t // tb
    kb = tb // c
    has_cprev = valid_lo > 0
    in_specs = [
        pl.BlockSpec((1, tb, D_CONV), lambda b, i: (b, i, 0)),
        pl.BlockSpec((1, 8, D_CONV), lambda b, i: (b, jnp.maximum(i * (tb // 8) - 1, 0), 0)),
        pl.BlockSpec((1, tb, LANES), lambda b, i: (b, i, GATE_COL_BLOCK)),
        pl.BlockSpec((kb, N_GATE, c), lambda b, i: (b * nt + i, 0, 0)),
    ]
    args = [proj3, proj3, proj3, grow]
    if has_cprev:
        assert nb == 1
        in_specs.append(pl.BlockSpec((1, tb, D_CONV), lambda b, i: (layer, i, 0)))
        args.append(cprev)
    in_specs += [
        _layer_spec(layer, CONV_W, D_CONV),
        _layer_spec(layer, 8, LANES),
        _layer_spec(layer, 8, LANES),
        _layer_spec(layer, N_GATE, LANES),
        _layer_spec(layer, N_GATE, LANES),
    ]
    args += [cw, pa, pl_, ra, rl]
    n = t // c
    hd = N_HEADS * DH
    out_shape = [
        jax.ShapeDtypeStruct((nb, n, 2 * c, hd), idt),
        jax.ShapeDtypeStruct((nb, n, c, hd), F32),
        jax.ShapeDtypeStruct((nb, n, N_HEADS // 2, c, 2 * c), idt),
        jax.ShapeDtypeStruct((nb, n, hd, c), idt),
        jax.ShapeDtypeStruct((nb, n, 8, hd), F32),
    ]
    out_specs = [
        pl.BlockSpec((1, kb, 2 * c, hd), lambda b, i: (b, i, 0, 0)),
        pl.BlockSpec((1, kb, c, hd), lambda b, i: (b, i, 0, 0)),
        pl.BlockSpec((1, kb, N_HEADS // 2, c, 2 * c), lambda b, i: (b, i, 0, 0, 0)),
        pl.BlockSpec((1, kb, hd, c), lambda b, i: (b, i, 0, 0)),
        pl.BlockSpec((1, kb, 8, hd), lambda b, i: (b, i, 0, 0)),
    ]
    return pl.pallas_call(
        functools.partial(_gdn_prep_kernel, tb=tb, seq_len=seq_len, valid_lo=valid_lo, idt=idt),
        grid=(nb, nt),
        in_specs=in_specs,
        out_specs=out_specs,
        out_shape=out_shape,
        compiler_params=_cparams(2),
        name="gdn_prep",
    )(*args)


def _lane_half_masks(dtype):
    lo = lax.broadcasted_iota(jnp.int32, (CHUNK, 2 * CHUNK), 1) < CHUNK
    return jnp.where(lo, 1.0, 0.0).astype(dtype), jnp.where(lo, 0.0, 1.0).astype(dtype)


def _gated_norm_store(o, z, gn, mix_ref, idx):
    mix_ref[idx] = (_rms(o, gn) * jax.nn.silu(z)).astype(BF16)


def _gdn_scan_prompt_kernel(wkqg_ref, wv_ref, qk_ref, kdt_ref, el_ref, z_ref, gn_ref,
                            mix_ref, sout_ref, s_scr, *, nc, bs):
    c = CHUNK

    @pl.when(pl.program_id(1) == 0)
    def _():
        s_scr[...] = jnp.zeros_like(s_scr)

    gn = gn_ref[...]
    units = [(b, h, slice(DH * h, DH * (h + 1))) for b in range(bs) for h in range(N_HEADS)]
    half_b = _lane_half_masks(BF16)

    def body(n, carry):
        rows = pl.ds(pl.multiple_of(n * c, c), c)
        ss = [s_scr[b * N_HEADS + h] for b, h, _ in units]
        aa = [_dot(wkqg_ref[b, n, :, hs], s.astype(BF16)) for (b, _, hs), s in zip(units, ss)]
        ubs = [(wv_ref[b, n, :, hs] - a[0:c]).astype(BF16) for (b, _, hs), a in zip(units, aa)]
        ds = [_dot(kdt_ref[b, n, hs, :], ub) for (b, _, hs), ub in zip(units, ubs)]
        os_ = []
        for j in range(len(units) // 2):
            b, h, _ = units[2 * j]
            qkp = qk_ref[b, n, h // 2]
            o2 = _dot(jnp.concatenate([qkp * half_b[0], qkp * half_b[1]], axis=0),
                      jnp.concatenate(ubs[2 * j:2 * j + 2], axis=0))
            os_ += [o2[0:c], o2[c:2 * c]]
        for i, (b, h, hs) in enumerate(units):
            s_scr[b * N_HEADS + h] = el_ref[b, n, 0:1, hs] * ss[i] + ds[i]
            _gated_norm_store(aa[i][c:2 * c] + os_[i], z_ref[b, rows, hs], gn, mix_ref, (b, rows, hs))
        return carry

    lax.fori_loop(0, nc, body, 0)

    @pl.when(pl.program_id(1) == pl.num_programs(1) - 1)
    def _():
        for b in range(bs):
            sout_ref[b] = s_scr[b * N_HEADS:(b + 1) * N_HEADS]


def _gdn_scan_prompt(wkqg, wv, qk, kdt, el, proj3, gn, layer, *, ts, bs):
    nb, n, _, hd = wkqg.shape
    c = CHUNK
    t = n * c
    nc = ts // c
    zblk = D_CONV // hd
    return pl.pallas_call(
        functools.partial(_gdn_scan_prompt_kernel, nc=nc, bs=bs),
        grid=(nb // bs, t // ts),
        in_specs=[
            pl.BlockSpec((bs, nc, 2 * c, hd), lambda b, i: (b, i, 0, 0)),
            pl.BlockSpec((bs, nc, c, hd), lambda b, i: (b, i, 0, 0)),
            pl.BlockSpec((bs, nc, N_HEADS // 2, c, 2 * c), lambda b, i: (b, i, 0, 0, 0)),
            pl.BlockSpec((bs, nc, hd, c), lambda b, i: (b, i, 0, 0)),
            pl.BlockSpec((bs, nc, 8, hd), lambda b, i: (b, i, 0, 0)),
            pl.BlockSpec((bs, ts, hd), lambda b, i: (b, i, zblk)),
            _layer_spec(layer, 1, DH),
        ],
        out_specs=[
            pl.BlockSpec((bs, ts, hd), lambda b, i: (b, i, 0)),
            pl.BlockSpec((bs, N_HEADS, DH, DH), lambda b, i: (b, 0, 0, 0)),
        ],
        out_shape=[jax.ShapeDtypeStruct((nb, t, hd), BF16),
                   jax.ShapeDtypeStruct((nb, N_HEADS, DH, DH), F32)],
        scratch_shapes=[pltpu.VMEM((bs * N_HEADS, DH, DH), F32)],
        compiler_params=_cparams(2),
        name="gdn_scan_prompt",
    )(wkqg, wv, qk, kdt, el, proj3, gn)


def _gdn_scan_decode_kernel(*refs, n_prev):
    (wkqg_ref, wv_ref, qk_ref, kdt_ref, el_ref, z_ref, gn_ref, s0_ref), rest = refs[:8], refs[8:]
    prev_refs, (mix_ref, sout_ref) = rest[:n_prev], rest[n_prev:]
    if n_prev:
        for l, p_ref in enumerate(prev_refs):
            sout_ref[l] = p_ref[...]
        sout_ref = sout_ref.at[n_prev]
    c = CHUNK
    nseq = c // SLOT
    rowseq = lax.broadcasted_iota(jnp.int32, (c, 1), 0) >> 3
    gn = gn_ref[...]
    heads = [slice(DH * h, DH * (h + 1)) for h in range(N_HEADS)]
    aa = []
    for h, hs in enumerate(heads):
        w = wkqg_ref[0, 0, :, hs]
        row = []
        for j in range(nseq):
            wj = jnp.concatenate([w[SLOT * j:SLOT * (j + 1)], w[c + SLOT * j:c + SLOT * (j + 1)]],
                                 axis=0).astype(BF16)
            row.append(_dot(wj, s0_ref[j, h].astype(BF16)))
        aa.append(row)
    us = [wv_ref[0, 0, :, hs] - jnp.concatenate([a[0:SLOT] for a in aa[h]], axis=0)
          for h, hs in enumerate(heads)]
    half_f = _lane_half_masks(F32)
    os_ = []
    for j in range(N_HEADS // 2):
        qkp = qk_ref[0, 0, j]
        o2 = _dot(jnp.concatenate([qkp * half_f[0], qkp * half_f[1]], axis=0).astype(BF16),
                  jnp.concatenate(us[2 * j:2 * j + 2], axis=0).astype(BF16))
        os_ += [o2[0:c], o2[c:2 * c]]
    for h, hs in enumerate(heads):
        kdt = kdt_ref[0, 0, hs, :].astype(BF16)
        for j in range(nseq):
            uj = jnp.where(rowseq == j, us[h], 0.0).astype(BF16)
            sout_ref[j, h] = el_ref[0, 0, j:j + 1, hs] * s0_ref[j, h] + _dot(kdt, uj)
    for h, hs in enumerate(heads):
        o = jnp.concatenate([a[SLOT:2 * SLOT] for a in aa[h]], axis=0) + os_[h]
        _gated_norm_store(o, z_ref[:, hs], gn, mix_ref, (slice(None), hs))


def _state_specs(layer, prev, state_shape, nseq):
    tail = state_shape[2:]
    zeros = (0,) * len(tail)
    in_specs = [pl.BlockSpec((None, nseq) + tail, lambda i: (layer, i) + zeros)]
    in_specs += [pl.BlockSpec((nseq,) + tail, lambda i: (i,) + zeros) for _ in prev]
    if prev:
        out_spec = pl.BlockSpec((len(prev) + 1, nseq) + tail, lambda i: (0, i) + zeros)
        out_shape = jax.ShapeDtypeStruct((len(prev) + 1,) + state_shape[1:], F32)
    else:
        out_spec = pl.BlockSpec((nseq,) + tail, lambda i: (i,) + zeros)
        out_shape = jax.ShapeDtypeStruct(state_shape[1:], F32)
    return in_specs, out_spec, out_shape


def _gdn_scan_decode(wkqg, wv, qk, kdt, el, proj, gn, s0_all, prev, layer):
    _, n, _, hd = wkqg.shape
    c = CHUNK
    nseq = c // SLOT
    zblk = D_CONV // hd
    st_in, st_out, st_shape = _state_specs(layer, prev, s0_all.shape, nseq)
    return pl.pallas_call(
        functools.partial(_gdn_scan_decode_kernel, n_prev=len(prev)),
        grid=(n,),
        in_specs=[
            pl.BlockSpec((1, 1, 2 * c, hd), lambda i: (0, i, 0, 0)),
            pl.BlockSpec((1, 1, c, hd), lambda i: (0, i, 0, 0)),
            pl.BlockSpec((1, 1, N_HEADS // 2, c, 2 * c), lambda i: (0, i, 0, 0, 0)),
            pl.BlockSpec((1, 1, hd, c), lambda i: (0, i, 0, 0)),
            pl.BlockSpec((1, 1, 8, hd), lambda i: (0, i, 0, 0)),
            pl.BlockSpec((c, hd), lambda i: (i, zblk)),
            _layer_spec(layer, 1, DH),
        ] + st_in,
        out_specs=[pl.BlockSpec((c, hd), lambda i: (i, 0)), st_out],
        out_shape=[jax.ShapeDtypeStruct((n * c, hd), BF16), st_shape],
        compiler_params=_cparams(1),
        name="gdn_scan_decode",
    )(wkqg, wv, qk, kdt, el, proj, gn, s0_all, *prev)


def _mlstm_gates(gcol, growk, pa_ref, ra_ref, h, rows, valid_lo, rowpos, colpos):
    xg = gcol + pa_ref[0:1, :]
    ig_col = xg[:, 2 * N_HEADS + h:2 * N_HEADS + h + 1]
    fl_col = -_softplus(-xg[:, 3 * N_HEADS + h:3 * N_HEADS + h + 1])
    xr = growk + ra_ref[:, 0:1]
    ig_row = xr[2 * N_HEADS + h:2 * N_HEADS + h + 1, :]
    fl_row = -_softplus(-xr[3 * N_HEADS + h:3 * N_HEADS + h + 1, :])
    if valid_lo > 0:
        ig_col = jnp.where(rowpos >= valid_lo, ig_col, NEG)
        fl_col = jnp.where(rowpos >= valid_lo, fl_col, 0.0)
        ig_row = jnp.where(colpos >= valid_lo, ig_row, NEG)
        fl_row = jnp.where(colpos >= valid_lo, fl_row, 0.0)
    return ig_col, fl_col, ig_row, fl_row


def _mlstm_chunks(probs, seq_len):
    c = CHUNK
    nseq = c // seq_len
    ri, ci, same = _chunk_masks(seq_len)
    causal = (ri >= ci) & same
    upper = (ri <= ci) & same
    seq_end = ci == (ri | (seq_len - 1))
    rowseq = lax.broadcasted_iota(jnp.int32, (c, 1), 0) >> (seq_len.bit_length() - 1)
    zpad = jnp.zeros((16 - seq_len, DH), F32) if nseq > 1 else None
    for p in probs:
        p["f_col"] = jnp.sum(jnp.where(causal, p["fl_row"], 0.0), axis=1, keepdims=True)
        p["f_row"] = jnp.sum(jnp.where(upper, p["fl_col"], 0.0), axis=0, keepdims=True)
        p["qn"] = jnp.sum(p["q"] * p["n_rows"], axis=1, keepdims=True)
        p["km"] = p["k"] * (DH ** -0.5)
        p["qb"] = p["q"].astype(BF16)
        p["kb"] = p["km"].astype(BF16)
        p["vb"] = p["v"].astype(BF16)
    for p in probs:
        p["b_col"] = p["ig_col"] - p["f_col"]
        p["b_row"] = p["ig_row"] - p["f_row"]
        p["bmax_col"] = jnp.max(jnp.where(causal, p["b_row"], -jnp.inf), axis=1, keepdims=True)
        if nseq > 1:
            p["fl_end"] = jnp.sum(jnp.where(seq_end, p["f_row"], 0.0), axis=1, keepdims=True)
            p["bmax_end"] = jnp.max(jnp.where(same, p["b_row"], -jnp.inf), axis=1, keepdims=True)
    for p in probs:
        f_col, mp_col = p["f_col"], p["mp_col"]
        m_col = f_col + jnp.maximum(mp_col, p["bmax_col"])
        p["m_col"] = m_col
        p["a_col"] = jnp.exp(f_col + mp_col - m_col)
        p["dexp"] = jnp.where(causal, jnp.exp(jnp.where(causal, (f_col + p["b_row"]) - m_col, 0.0)), 0.0)
        if nseq == 1:
            fl_end, ml, mp = f_col[c - 1:c], m_col[c - 1:c], mp_col[c - 1:c]
            p["ml_col"] = jnp.broadcast_to(ml, (c, 1))
        else:
            fl_end, mp = p["fl_end"], mp_col
            ml = fl_end + jnp.maximum(mp_col, p["bmax_end"])
            p["ml_col"] = ml
        p["al"] = jnp.broadcast_to(jnp.exp(fl_end + mp - ml), (c, DH))
        p["kw"] = p["km"] * jnp.exp(fl_end + p["b_col"] - ml)
    for p in probs:
        p["kwt"] = p["kw"].T.astype(BF16)
    for p in probs:
        p["qk"] = _dot_nt(p["qb"], p["kb"])
    for p in probs:
        if nseq == 1:
            p["qc"] = _dot(p["qb"], p["c_list"][0].astype(BF16))
            p["upd"] = [_dot(p["kwt"], p["vb"])]
        else:
            parts, upd = [], []
            for j in range(nseq):
                qj = jnp.concatenate([p["q"][seq_len * j:seq_len * (j + 1)], zpad], axis=0).astype(BF16)
                parts.append(_dot(qj, p["c_list"][j].astype(BF16))[0:seq_len])
                upd.append(_dot(p["kwt"], jnp.where(rowseq == j, p["v"], 0.0).astype(BF16)))
            p["qc"] = jnp.concatenate(parts, axis=0)
            p["upd"] = upd
    for p in probs:
        p["w"] = p["dexp"] * p["qk"]
        p["wsum"] = jnp.sum(p["w"], axis=1, keepdims=True)
    for p in probs:
        p["wv"] = _dot(p["w"].astype(BF16), p["vb"])
    for p in probs:
        num = p["a_col"] * p["qc"] + p["wv"]
        den = p["a_col"] * p["qn"] + p["wsum"]
        p["hout"] = num / jnp.maximum(jnp.abs(den), jnp.exp(-p["m_col"]))
        p["new_c"] = [p["al"][seq_len * j:seq_len * j + 1] * p["c_list"][j] + p["upd"][j]
                      for j in range(nseq)]
    return probs


def _mlstm_out_store(hout, og, gnorm, mix_ref, idx):
    mix_ref[idx] = _rms(hout * jax.nn.sigmoid(og), gnorm).astype(BF16)


def _mlstm_prompt_kernel(q_ref, k_ref, v_ref, o_ref, gcol_ref, grow_ref, pa_ref, ra_ref, nrm_ref,
                         mix_ref, cout_ref, nout_ref, mout_ref, c_scr, n_scr, m_scr, *, nc, bs):
    c = CHUNK

    @pl.when(pl.program_id(1) == 0)
    def _():
        c_scr[...] = jnp.zeros_like(c_scr)
        n_scr[...] = jnp.zeros_like(n_scr)
        m_scr[...] = jnp.zeros_like(m_scr)

    def body(n, carry):
        r0 = pl.multiple_of(n * c, c)
        rows = pl.ds(r0, c)
        probs = []
        for b in range(bs):
            gcol = gcol_ref[b, rows, :]
            growk = grow_ref[b, n]
            for h in range(N_HEADS):
                hs = slice(DH * h, DH * (h + 1))
                ig_col, fl_col, ig_row, fl_row = _mlstm_gates(gcol, growk, pa_ref, ra_ref, h, c, 0, None, None)
                st = b * N_HEADS + h
                probs.append(dict(b=b, hs=hs, st=st, q=q_ref[b, rows, hs], k=k_ref[b, rows, hs],
                                  v=v_ref[b, rows, hs], ig_col=ig_col, fl_col=fl_col, ig_row=ig_row,
                                  fl_row=fl_row, mp_col=m_scr[st][:, 0:1], n_rows=n_scr[st],
                                  c_list=[c_scr[st]]))
        for p in _mlstm_chunks(probs, c):
            st = p["st"]
            c_scr[st] = p["new_c"][0]
            n_scr[st] = jnp.broadcast_to(
                p["al"][0:1] * p["n_rows"][0:1] + jnp.sum(p["kw"], axis=0, keepdims=True), (c, DH))
            m_scr[st] = jnp.broadcast_to(p["ml_col"], (c, DH))
            _mlstm_out_store(p["hout"], o_ref[p["b"], rows, p["hs"]], nrm_ref[:, p["hs"]], mix_ref,
                             (p["b"], rows, p["hs"]))
        return carry

    lax.fori_loop(0, nc, body, 0)

    @pl.when(pl.program_id(1) == pl.num_programs(1) - 1)
    def _():
        lane = lax.broadcasted_iota(jnp.int32, (8, DH), 1)
        for b in range(bs):
            mo = jnp.zeros((8, DH), F32)
            for h in range(N_HEADS):
                st = b * N_HEADS + h
                cout_ref[b, h] = c_scr[st]
                nout_ref[b, h:h + 1, :] = n_scr[st][0:1]
                mo = jnp.where(lane == h, m_scr[st][0:8], mo)
            mout_ref[b] = mo


def _mlstm_prompt(proj3, grow4, pa, ra, nrm, layer, *, ts, bs):
    nb, t, _ = proj3.shape
    c = CHUNK
    hd = N_HEADS * DH
    nc = ts // c
    qblk = (D_CONV + hd) // hd
    return pl.pallas_call(
        functools.partial(_mlstm_prompt_kernel, nc=nc, bs=bs),
        grid=(nb // bs, t // ts),
        in_specs=[
            pl.BlockSpec((bs, ts, hd), lambda b, i: (b, i, qblk)),
            pl.BlockSpec((bs, ts, hd), lambda b, i: (b, i, qblk + 1)),
            pl.BlockSpec((bs, ts, hd), lambda b, i: (b, i, qblk + 2)),
            pl.BlockSpec((bs, ts, hd), lambda b, i: (b, i, qblk + 3)),
            pl.BlockSpec((bs, ts, LANES), lambda b, i: (b, i, GATE_COL_BLOCK)),
            pl.BlockSpec((bs, nc, N_GATE, c), lambda b, i: (b, i, 0, 0)),
            _layer_spec(layer, 8, LANES),
            _layer_spec(layer, N_GATE, LANES),
            _layer_spec(layer, 1, hd),
        ],
        out_specs=[
            pl.BlockSpec((bs, ts, hd), lambda b, i: (b, i, 0)),
            pl.BlockSpec((bs, N_HEADS, DH, DH), lambda b, i: (b, 0, 0, 0)),
            pl.BlockSpec((bs, N_HEADS, DH), lambda b, i: (b, 0, 0)),
            pl.BlockSpec((bs, 8, DH), lambda b, i: (b, 0, 0)),
        ],
        out_shape=[jax.ShapeDtypeStruct((nb, t, hd), BF16),
                   jax.ShapeDtypeStruct((nb, N_HEADS, DH, DH), F32),
                   jax.ShapeDtypeStruct((nb, N_HEADS, DH), F32),
                   jax.ShapeDtypeStruct((nb, 8, DH), F32)],
        scratch_shapes=[pltpu.VMEM((bs * N_HEADS, DH, DH), F32),
                        pltpu.VMEM((bs * N_HEADS, c, DH), F32),
                        pltpu.VMEM((bs * N_HEADS, c, DH), F32)],
        compiler_params=_cparams(2),
        name="mlstm_prompt",
    )(proj3, proj3, proj3, proj3, proj3, grow4, pa, ra, nrm)


def _mlstm_decode_kernel(*refs, n_prev):
    (q_ref, k_ref, v_ref, o_ref, gcol_ref, grow_ref, pa_ref, ra_ref, nrm_ref, n0_ref, m0_ref,
     c0_ref), rest = refs[:12], refs[12:]
    prev_refs, (mix_ref, nout_ref, mout_ref, cout_ref) = rest[:n_prev], rest[n_prev:]
    if n_prev:
        for l, p_ref in enumerate(prev_refs):
            cout_ref[l] = p_ref[...]
        cout_ref = cout_ref.at[n_prev]
    c = CHUNK
    nseq = c // SLOT
    rowpos = lax.broadcasted_iota(jnp.int32, (c, 1), 0) & (SLOT - 1)
    colpos = lax.broadcasted_iota(jnp.int32, (1, c), 1) & (SLOT - 1)
    lane = lax.broadcasted_iota(jnp.int32, (c, DH), 1)
    gcol = gcol_ref[...]
    growk = grow_ref[0]
    probs = []
    for h in range(N_HEADS):
        hs = slice(DH * h, DH * (h + 1))
        ig_col, fl_col, ig_row, fl_row = _mlstm_gates(gcol, growk, pa_ref, ra_ref, h, c, SLOT_PAD, rowpos, colpos)
        n_rows = jnp.concatenate(
            [jnp.broadcast_to(n0_ref[j, h:h + 1, :], (SLOT, DH)) for j in range(nseq)], axis=0)
        probs.append(dict(h=h, hs=hs, q=q_ref[:, hs], k=k_ref[:, hs], v=v_ref[:, hs], ig_col=ig_col,
                          fl_col=fl_col, ig_row=ig_row, fl_row=fl_row, mp_col=m0_ref[:, h:h + 1],
                          n_rows=n_rows, c_list=[c0_ref[j, h] for j in range(nseq)]))
    mo = jnp.zeros((c, DH), F32)
    for p in _mlstm_chunks(probs, SLOT):
        h, hs = p["h"], p["hs"]
        for j in range(nseq):
            cout_ref[j, h] = p["new_c"][j]
            rs = slice(SLOT * j, SLOT * (j + 1))
            nout_ref[j, h:h + 1, :] = (p["al"][SLOT * j:SLOT * j + 1] * n0_ref[j, h:h + 1, :]
                                       + jnp.sum(p["kw"][rs], axis=0, keepdims=True))
        mo = jnp.where(lane == h, jnp.broadcast_to(p["ml_col"], (c, DH)), mo)
        _mlstm_out_store(p["hout"], o_ref[:, hs], nrm_ref[:, hs], mix_ref, (slice(None), hs))
    mout_ref[...] = mo


def _mlstm_decode(proj, grow, pa, ra, nrm, c0_all, n0_all, m0rows_all, prev, layer):
    rows = proj.shape[0]
    c = CHUNK
    hd = N_HEADS * DH
    nseq = c // SLOT
    qblk = (D_CONV + hd) // hd
    st_in, st_out, st_shape = _state_specs(layer, prev, c0_all.shape, nseq)
    return pl.pallas_call(
        functools.partial(_mlstm_decode_kernel, n_prev=len(prev)),
        grid=(rows // c,),
        in_specs=[
            pl.BlockSpec((c, hd), lambda i: (i, qblk)),
            pl.BlockSpec((c, hd), lambda i: (i, qblk + 1)),
            pl.BlockSpec((c, hd), lambda i: (i, qblk + 2)),
            pl.BlockSpec((c, hd), lambda i: (i, qblk + 3)),
            pl.BlockSpec((c, LANES), lambda i: (i, GATE_COL_BLOCK)),
            pl.BlockSpec((1, N_GATE, c), lambda i: (i, 0, 0)),
            _layer_spec(layer, 8, LANES),
            _layer_spec(layer, N_GATE, LANES),
            _layer_spec(layer, 1, hd),
            pl.BlockSpec((None, nseq, N_HEADS, DH), lambda i: (layer, i, 0, 0)),
            pl.BlockSpec((None, c, LANES), lambda i: (layer, i, 0)),
        ] + st_in,
        out_specs=[
            pl.BlockSpec((c, hd), lambda i: (i, 0)),
            pl.BlockSpec((nseq, N_HEADS, DH), lambda i: (i, 0, 0)),
            pl.BlockSpec((c, LANES), lambda i: (i, 0)),
            st_out,
        ],
        out_shape=[jax.ShapeDtypeStruct((rows, hd), BF16),
                   jax.ShapeDtypeStruct(n0_all.shape[1:], F32),
                   jax.ShapeDtypeStruct((rows, LANES), F32),
                   st_shape],
        compiler_params=_cparams(1),
        name="mlstm_decode",
    )(proj, proj, proj, proj, proj, grow, pa, ra, nrm, n0_all, m0rows_all, c0_all, *prev)


def _post_kernel(x_ref, ma_ref, mb_ref, p_ref, woa_ref, wob_ref, gf_ref, wg_ref, wu_ref, wd_ref,
                 gp_ref, wpg_ref, wpp_ref, gfin_ref, o_ref, acc_ref, *, final):
    x = x_ref[...] + (_dot(ma_ref[...], woa_ref[...]) + _dot(mb_ref[...], wob_ref[...]))
    ub = _rms(x, gf_ref[...]).astype(BF16)
    for j in range(D_FF // FF_CHUNK):
        sl = slice(j * FF_CHUNK, (j + 1) * FF_CHUNK)
        a = (jax.nn.silu(_dot(ub, wg_ref[:, sl])) * _dot(ub, wu_ref[:, sl])).astype(BF16)
        d = _dot(a, wd_ref[sl, :])
        if j == 0:
            acc_ref[...] = d
        else:
            acc_ref[...] += d
    x = x + acc_ref[...]
    gate = jax.nn.sigmoid(_dot(_rms(x, gp_ref[...]).astype(BF16), wpg_ref[...]))
    x = x + _dot(p_ref[...].astype(BF16), wpp_ref[...]) * gate
    if final:
        x = _rms(x, gfin_ref[...])
    o_ref[...] = x


def _post(x, ma, mb, p_all, wo, gf, wg, wu, wd, gp, wpg, wpp, gfin, layer, *, final):
    rows = x.shape[0]
    tm = ROW_TILE
    hd = N_HEADS * DH
    row = lambda w: pl.BlockSpec((tm, w), lambda i: (i, 0))

    def whole(a, b, blk=0):
        return pl.BlockSpec((None, a, b), lambda i: (layer, blk, 0), pipeline_mode=pl.Buffered(1))

    return pl.pallas_call(
        functools.partial(_post_kernel, final=final),
        grid=(rows // tm,),
        in_specs=[row(D_MODEL), row(hd), row(hd),
                  pl.BlockSpec((None, tm, D_PLE), lambda i: (layer, i, 0)),
                  whole(hd, D_MODEL, 0), whole(hd, D_MODEL, 1), whole(1, D_MODEL),
                  whole(D_MODEL, D_FF), whole(D_MODEL, D_FF), whole(D_FF, D_MODEL),
                  whole(1, D_MODEL), whole(D_MODEL, D_MODEL), whole(D_PLE, D_MODEL),
                  pl.BlockSpec((1, D_MODEL), lambda i: (0, 0))],
        out_specs=row(D_MODEL),
        out_shape=jax.ShapeDtypeStruct((rows, D_MODEL), F32),
        scratch_shapes=[pltpu.VMEM((tm, D_MODEL), F32)],
        compiler_params=_cparams(1),
        name="post",
    )(x, ma, mb, p_all, wo, wo, gf, wg, wu, wd, gp, wpg, wpp, gfin)


def _gate_vec(pairs, depth):
    v = jnp.zeros((depth, N_GATE), F32)
    for off, val in pairs:
        v = v.at[:, off:off + N_HEADS].set(val.astype(F32))
    return v


def _lane_form(v):
    d = v.shape[0]
    return jnp.broadcast_to(jnp.pad(v, ((0, 0), (0, LANES - N_GATE)))[:, None, :], (d, 8, LANES))


def _row_form(v):
    d = v.shape[0]
    return jnp.broadcast_to(v[:, :, None], (d, N_GATE, LANES))


def kernel(x_prompt, x_sample, p_prompt, p_sample, state_gdn, state_gdn_conv, state_mlstm_C, state_mlstm_n, state_mlstm_m, w_in, conv_w, gdn_a_log, gdn_dt_bias, gdn_norm, mlstm_i_bias, mlstm_f_bias, mlstm_norm, w_out, norm_mix, norm_ffn, w_gate, w_up, w_down, norm_ple, w_ple_gate, w_ple_proj, norm_final):
    depth = w_in.shape[0]
    nb, t, _ = x_prompt.shape
    ns, tdec, _ = x_sample.shape
    hd = N_HEADS * DH
    c = CHUNK
    pad = SLOT - tdec

    xp = x_prompt.reshape(nb * t, D_MODEL)
    xs = jnp.pad(x_sample, ((0, 0), (pad, 0), (0, 0))).reshape(ns * SLOT, D_MODEL)
    ps_all = jnp.pad(p_sample, ((0, 0), (0, 0), (pad, 0), (0, 0))).reshape(depth, ns * SLOT, D_PLE)
    pp_all = p_prompt.reshape(depth, nb * t, D_PLE)
    gfin = norm_final.reshape(1, D_MODEL)

    o = D_CONV + hd
    w_g = jnp.concatenate([w_in[:, :, o:o + 2 * N_HEADS], w_in[:, :, o + 2 * N_HEADS + 4 * hd:]], axis=2)
    w_r = jnp.concatenate([w_in[:, :, :o], w_in[:, :, o + 2 * N_HEADS:o + 2 * N_HEADS + 4 * hd], w_g,
                           jnp.zeros((depth, D_MODEL, LANES - N_GATE), F32)], axis=2).astype(BF16)
    w_gt = jnp.swapaxes(w_g, 1, 2).astype(BF16)
    adds = _gate_vec([(N_HEADS, gdn_dt_bias), (2 * N_HEADS, mlstm_i_bias), (3 * N_HEADS, mlstm_f_bias)], depth)
    alog = _gate_vec([(N_HEADS, gdn_a_log)], depth)
    pa, pl_, ra, rl = _lane_form(adds), _lane_form(alog), _row_form(adds), _row_form(alog)
    g_mix = norm_mix.reshape(depth, 1, D_MODEL)
    cw = conv_w.astype(F32)
    gn = gdn_norm.reshape(depth, 1, DH).astype(F32)
    nrm = mlstm_norm.reshape(depth, 1, hd).astype(F32)
    wts = (w_out.astype(BF16), norm_ffn.reshape(depth, 1, D_MODEL), w_gate.astype(BF16), w_up.astype(BF16),
           w_down.astype(BF16), norm_ple.reshape(depth, 1, D_MODEL), w_ple_gate.astype(BF16),
           w_ple_proj.astype(BF16), gfin)
    cprev = jnp.pad(state_gdn_conv.astype(F32), ((0, 0), (0, 0), (pad - (CONV_W - 1), SLOT - pad), (0, 0)))
    cprev = cprev.reshape(depth, ns * SLOT, D_CONV)
    m0rows = jnp.pad(jnp.repeat(state_mlstm_m.astype(F32), SLOT, axis=1), ((0, 0), (0, 0), (0, LANES - N_HEADS)))
    s0_all, c0_all, n0_all = state_gdn.astype(F32), state_mlstm_C.astype(F32), state_mlstm_n.astype(F32)

    outs_p = [[] for _ in range(5)]
    outs_s = [[] for _ in range(5)]
    for i in range(depth):
        final = i == depth - 1
        proj_p, gt_p = _in_proj(xp, g_mix, w_r, w_gt, i)
        proj_s, gt_s = _in_proj(xs, g_mix, w_r, w_gt, i)
        proj_p3 = proj_p.reshape(nb, t, N_PROJ)
        proj_s3 = proj_s.reshape(1, ns * SLOT, N_PROJ)
        grow_p = gt_p.reshape(N_GATE, nb * t // c, c).transpose(1, 0, 2)
        grow_s = gt_s.reshape(N_GATE, ns * SLOT // c, c).transpose(1, 0, 2)

        prep_p = _gdn_prep(proj_p3, grow_p, None, cw, pa, pl_, ra, rl, i, seq_len=c, valid_lo=0, idt=BF16, tb=512)
        mixa_p, s_p = _gdn_scan_prompt(*prep_p, proj_p3, gn, i, ts=512, bs=2)
        prep_s = _gdn_prep(proj_s3, grow_s, cprev, cw, pa, pl_, ra, rl, i, seq_len=SLOT, valid_lo=pad, idt=F32, tb=c)
        mixa_s, s_s = _gdn_scan_decode(*prep_s, proj_s, gn, s0_all, outs_s[0] if final else [], i)

        mixb_p, c_p, n_p, m_p = _mlstm_prompt(proj_p3, grow_p.reshape(nb, t // c, N_GATE, c), pa, ra, nrm, i,
                                              ts=512, bs=4)
        mixb_s, n_s, m_s, c_s = _mlstm_decode(proj_s, grow_s, pa, ra, nrm, c0_all, n0_all, m0rows,
                                              outs_s[2] if final else [], i)

        xp = _post(xp, mixa_p.reshape(nb * t, hd), mixb_p.reshape(nb * t, hd), pp_all, *wts, i, final=final)
        xs = _post(xs, mixa_s, mixb_s, ps_all, *wts, i, final=final)

        outs_p[0].append(s_p)
        outs_p[1].append(proj_p3[:, t - (CONV_W - 1):, :D_CONV])
        outs_p[2].append(c_p)
        outs_p[3].append(n_p)
        outs_p[4].append(m_p[:, 0, :N_HEADS])
        outs_s[0].append(s_s)
        outs_s[1].append(proj_s.reshape(ns, SLOT, N_PROJ)[:, SLOT - (CONV_W - 1):, :D_CONV])
        outs_s[2].append(c_s)
        outs_s[3].append(n_s)
        outs_s[4].append(m_s.reshape(ns, SLOT, LANES)[:, SLOT - 1, :N_HEADS])

    y_prompt = xp.reshape(nb, t, D_MODEL)
    y_sample = xs.reshape(ns, SLOT, D_MODEL)[:, pad:, :]
    sp = [jnp.stack(a, axis=0) for a in outs_p]
    ss = [outs_s[j][-1] if j in (0, 2) else jnp.stack(outs_s[j], axis=0) for j in range(5)]
    return (y_prompt, y_sample, *sp, *ss)
```

```python
import functools

import jax
import jax.numpy as jnp
from jax import lax
from jax.experimental import pallas as pl
from jax.experimental.pallas import tpu as pltpu

F32 = jnp.float32
BF16 = jnp.bfloat16

D_MODEL = 1024
N_HEADS = 4
DH = 128
D_CONV = 3 * N_HEADS * DH
D_FF = 2816
D_PLE = 256
CONV_W = 4
CHUNK = 64
EPS = 1e-6
NEG = -1e30
N_GATE = 16
LANES = 128
N_PROJ = D_CONV + 5 * N_HEADS * DH + LANES
GATE_COL_BLOCK = (N_PROJ - LANES) // LANES
SLOT = 8
SLOT_PAD = 4
V7X_VMEM_LIMIT_BYTES = 56 * 1024 * 1024
ROW_TILE = 512
FF_CHUNK = 256


def _cparams(n_axes):
    return pltpu.CompilerParams(dimension_semantics=("arbitrary",) * n_axes,
                                vmem_limit_bytes=V7X_VMEM_LIMIT_BYTES)


def _rms(x, g):
    return x * lax.rsqrt(jnp.mean(x * x, axis=-1, keepdims=True) + EPS) * g


def _softplus(x):
    return jnp.maximum(x, 0.0) + jnp.log1p(jnp.exp(-jnp.abs(x)))


def _dot(a, b):
    return jnp.dot(a, b, preferred_element_type=F32)


def _dot_nt(a, b):
    return lax.dot_general(a, b, (((1,), (1,)), ((), ())), preferred_element_type=F32)


def _split(a):
    hi = a.astype(BF16)
    lo = (a - hi.astype(F32)).astype(BF16)
    return hi, lo


def _split_all(xs):
    return [_split(x) for x in xs]


def _mm3_all(a_list, b_list):
    out = []
    for (ah, al), (bh, bl) in zip(a_list, b_list):
        m = ah.shape[0]
        r = _dot(jnp.concatenate([ah, al], axis=0), bh)
        out.append(r[0:m] + (_dot(ah, bl) + r[m:2 * m]))
    return out


class _PairMasks:
    def __init__(self, seq_len):
        c = CHUNK
        self.r = lax.broadcasted_iota(jnp.int32, (c, 2 * c), 0)
        lane = lax.broadcasted_iota(jnp.int32, (c, 2 * c), 1)
        self.cc = lane & (c - 1)
        self.lo = lane < c
        if seq_len < c:
            sh = seq_len.bit_length() - 1
            same = (self.r >> sh) == (self.cc >> sh)
        else:
            same = self.r >= 0
        self.causal = (self.r >= self.cc) & same
        self.strict_f = jnp.where((self.r > self.cc) & same, 1.0, 0.0)
        self.eye = jnp.where(self.r == self.cc, 1.0, 0.0)
        self.lo_b = jnp.where(self.lo, 1.0, 0.0).astype(BF16)
        self.hi_b = jnp.where(self.lo, 0.0, 1.0).astype(BF16)

    def blockdiag(self, sp):
        return tuple(jnp.concatenate([x * self.lo_b, x * self.hi_b], axis=0) for x in sp)


def _unit_lower_inverse_pairs(l_list, pm, seq_len):
    def bd_all(sps):
        return [pm.blockdiag(sp) for sp in sps]

    blk8 = jnp.where((pm.r >> 3) == (pm.cc >> 3), 1.0, 0.0)
    n0 = [l * blk8 for l in l_list]
    n0s = _split_all(n0)
    n2 = _mm3_all(n0s, bd_all(n0s))
    n2s = _split_all(n2)
    n4 = _mm3_all(n2s, bd_all(n2s))
    p = _mm3_all(_split_all([pm.eye - a for a in n0]), bd_all(_split_all([pm.eye + a for a in n2])))
    d = _mm3_all(_split_all(p), bd_all(_split_all([pm.eye + a for a in n4])))
    ds = _split_all(d)
    s = 8
    while s < seq_len:
        sh = s.bit_length() - 1
        off = jnp.where(((pm.r >> (sh + 1)) == (pm.cc >> (sh + 1))) & ((pm.r >> sh) != (pm.cc >> sh)), 1.0, 0.0)
        de = _mm3_all(ds, bd_all(_split_all([l * off for l in l_list])))
        ded = _mm3_all(_split_all(de), bd_all(ds))
        d = [a - b for a, b in zip(d, ded)]
        ds = _split_all(d)
        s *= 2
    return ds


def _in_proj_kernel(x_ref, g_ref, w_ref, proj_ref, gt_ref):
    hb = _rms(x_ref[...], g_ref[...]).astype(BF16)
    proj_ref[...] = _dot(hb, w_ref[...])
    gt_ref[...] = proj_ref[:, N_PROJ - LANES:N_PROJ].T[0:N_GATE, :]


def _layer_spec(layer, *shape):
    zeros = (0,) * len(shape)
    return pl.BlockSpec((None,) + shape, lambda *_: (layer,) + zeros)


def _in_proj(x, g, w, layer):
    rows = x.shape[0]
    tm = ROW_TILE
    return pl.pallas_call(
        _in_proj_kernel,
        grid=(rows // tm,),
        in_specs=[
            pl.BlockSpec((tm, D_MODEL), lambda i: (i, 0)),
            _layer_spec(layer, 1, D_MODEL),
            _layer_spec(layer, D_MODEL, N_PROJ),
        ],
        out_specs=[
            pl.BlockSpec((tm, N_PROJ), lambda i: (i, 0)),
            pl.BlockSpec((N_GATE, tm), lambda i: (0, i)),
        ],
        out_shape=[jax.ShapeDtypeStruct((rows, N_PROJ), F32),
                   jax.ShapeDtypeStruct((N_GATE, rows), F32)],
        compiler_params=_cparams(1),
        name="in_proj",
    )(x, g, w)


def _chunk_masks(seq_len):
    c = CHUNK
    ri = lax.broadcasted_iota(jnp.int32, (c, c), 0)
    ci = lax.broadcasted_iota(jnp.int32, (c, c), 1)
    if seq_len < c:
        sh = seq_len.bit_length() - 1
        same = (ri >> sh) == (ci >> sh)
    else:
        same = ri >= 0
    return ri, ci, same


def _gdn_prep_kernel(*refs, tb, seq_len, valid_lo, idt):
    has_cprev = valid_lo > 0
    if has_cprev:
        (u_ref, prev_ref, gcol_ref, grow_ref, cprev_ref, cw_ref, pa_ref, pl_ref, ra_ref, rl_ref,
         wkqg_ref, wv_ref, qk_ref, kdt_ref, el_ref) = refs
    else:
        (u_ref, prev_ref, gcol_ref, grow_ref, cw_ref, pa_ref, pl_ref, ra_ref, rl_ref,
         wkqg_ref, wv_ref, qk_ref, kdt_ref, el_ref) = refs
    c = CHUNK
    sh = seq_len.bit_length() - 1
    u = u_ref[0]
    rowpos = lax.broadcasted_iota(jnp.int32, (tb, 1), 0) & (seq_len - 1)
    if has_cprev:
        u = jnp.where((rowpos >= valid_lo - (CONV_W - 1)) & (rowpos < valid_lo), cprev_ref[0], u)
    xp = jnp.concatenate([jnp.where(pl.program_id(1) == 0, 0.0, prev_ref[0]), u], axis=0)
    cw = cw_ref[...]
    y = None
    for i in range(CONV_W):
        s = CONV_W - 1 - i
        xs = u if s == 0 else pltpu.roll(xp, s, 0)[8:8 + tb]
        t = xs * cw[i:i + 1, :]
        y = t if y is None else y + t
    y = jax.nn.silu(y)

    lane = lax.broadcasted_iota(jnp.int32, (1, LANES), 1)
    xg = gcol_ref[0] + pa_ref[0:1, :]
    gcolv = jnp.where(lane < N_HEADS, jax.nn.sigmoid(xg), -jnp.exp(pl_ref[0:1, :]) * _softplus(xg))
    if valid_lo > 0:
        gcolv = jnp.where(rowpos >= valid_lo, gcolv, 0.0)

    ri, ci, same = _chunk_masks(seq_len)
    causal = (ri >= ci) & same
    upper = (ri <= ci) & same
    colpos = lax.broadcasted_iota(jnp.int32, (1, c), 1) & (seq_len - 1)
    r8 = lax.broadcasted_iota(jnp.int32, (8, c), 0)
    c8 = lax.broadcasted_iota(jnp.int32, (8, c), 1)

    items = []
    for k in range(tb // c):
        r0 = k * c
        growv = -jnp.exp(rl_ref[:, 0:1]) * _softplus(grow_ref[k] + ra_ref[:, 0:1])
        if valid_lo > 0:
            growv = jnp.where(colpos >= valid_lo, growv, 0.0)
        for h in range(N_HEADS):
            qc = y[r0:r0 + c, DH * h:DH * (h + 1)]
            kc = y[r0:r0 + c, N_HEADS * DH + DH * h:N_HEADS * DH + DH * (h + 1)]
            g_col = gcolv[r0:r0 + c, N_HEADS + h:N_HEADS + h + 1]
            g_row = growv[N_HEADS + h:N_HEADS + h + 1, :]
            items.append(dict(
                k=k, h=h, hs=slice(DH * h, DH * (h + 1)), qc=qc, kc=kc,
                vc=y[r0:r0 + c, 2 * N_HEADS * DH + DH * h:2 * N_HEADS * DH + DH * (h + 1)],
                beta=gcolv[r0:r0 + c, h:h + 1],
                qss=jnp.sum(qc * qc, axis=-1, keepdims=True),
                kss=jnp.sum(kc * kc, axis=-1, keepdims=True),
                gc_col=jnp.sum(jnp.where(causal, g_row, 0.0), axis=1, keepdims=True),
                gc_row=jnp.sum(jnp.where(upper, g_col, 0.0), axis=0, keepdims=True),
                gl_col=jnp.sum(jnp.where(same, g_row, 0.0), axis=1, keepdims=True),
                gl8=jnp.sum(jnp.where((c8 >> sh) == r8, g_row, 0.0), axis=1, keepdims=True)))
    for it in items:
        k, hs, gc_col, beta = it["k"], it["hs"], it["gc_col"], it["beta"]
        qn = it["qc"] * lax.rsqrt(it["qss"] + EPS) * (DH ** -0.5)
        kn = it["kc"] * lax.rsqrt(it["kss"] + EPS)
        wkqg_ref[0, k, c:2 * c, hs] = (qn * jnp.exp(gc_col)).astype(idt)
        el_ref[0, k, :, hs] = jnp.broadcast_to(jnp.exp(it["gl8"]), (8, DH))
        it["kd"] = kn * jnp.exp(it["gl_col"] - gc_col)
        it["qb"] = qn.astype(BF16)
        it["kb"] = kn.astype(BF16)
        it["rhs"] = jnp.concatenate([beta * it["vc"], (beta * jnp.exp(gc_col)) * kn], axis=1)
    for a, b in zip(items[0::2], items[1::2]):
        kdt_ref[0, a["k"], a["h"] // 2] = jnp.concatenate([a["kd"], b["kd"]], axis=0).T.astype(idt)
    pm = _PairMasks(seq_len)
    zb = jnp.zeros((c, DH), BF16)
    pairs = []
    for a, b in zip(items[0::2], items[1::2]):
        gcp = jnp.where(pm.lo, a["gc_col"], b["gc_col"])
        grp = jnp.concatenate([a["gc_row"], b["gc_row"]], axis=1)
        decay = jnp.where(pm.causal, jnp.exp(jnp.where(pm.causal, gcp - grp, 0.0)), 0.0)
        kbd = jnp.concatenate([jnp.concatenate([a["kb"], zb], axis=1),
                               jnp.concatenate([zb, b["kb"]], axis=1)], axis=0)
        kq = jnp.concatenate([jnp.concatenate([a["kb"], b["kb"]], axis=1),
                              jnp.concatenate([a["qb"], b["qb"]], axis=1)], axis=0)
        pairs.append(dict(a=a, b=b, decay=decay, kbd=kbd, kq=kq, beta=jnp.where(pm.lo, a["beta"], b["beta"])))
    kqs = [_dot_nt(p["kq"], p["kbd"]) for p in pairs]
    for p, kq in zip(pairs, kqs):
        qk_ref[0, p["a"]["k"], p["a"]["h"] // 2] = (kq[c:2 * c] * p["decay"]).astype(idt)
    ls = [p["beta"] * kq[0:c] * p["decay"] * pm.strict_f for p, kq in zip(pairs, kqs)]
    tinvs = _unit_lower_inverse_pairs(ls, pm, seq_len)
    for p, (th, tl) in zip(pairs, tinvs):
        rh, rl_ = _split(jnp.concatenate([p["a"]["rhs"], p["b"]["rhs"]], axis=0))
        ta_h, tb_h = th * pm.lo_b, th * pm.hi_b
        r1 = _dot(jnp.concatenate([ta_h, tl * pm.lo_b, tb_h, tl * pm.hi_b], axis=0), rh)
        r2 = _dot(jnp.concatenate([ta_h, tb_h], axis=0), rl_)
        p["sol"] = (r1[0:c] + (r2[0:c] + r1[c:2 * c]), r1[2 * c:3 * c] + (r2[c:2 * c] + r1[3 * c:4 * c]))
    for p in pairs:
        for it, sol in zip((p["a"], p["b"]), p["sol"]):
            wv_ref[0, it["k"], :, it["hs"]] = sol[:, 0:DH]
            wkqg_ref[0, it["k"], 0:c, it["hs"]] = sol[:, DH:2 * DH].astype(idt)


def _gdn_prep(proj3, grow, cprev, cw, pa, pl_, ra, rl, layer, *, seq_len, valid_lo, idt, tb):
    nb, t, _ = proj3.shape
    c = CHUNK
    nt = t // tb
    kb = tb // c
    has_cprev = valid_lo > 0
    in_specs = [
        pl.BlockSpec((1, tb, D_CONV), lambda b, i: (b, i, 0)),
        pl.BlockSpec((1, 8, D_CONV), lambda b, i: (b, jnp.maximum(i * (tb // 8) - 1, 0), 0)),
        pl.BlockSpec((1, tb, LANES), lambda b, i: (b, i, GATE_COL_BLOCK)),
        pl.BlockSpec((kb, N_GATE, c), lambda b, i: (b * nt + i, 0, 0)),
    ]
    args = [proj3, proj3, proj3, grow]
    if has_cprev:
        assert nb == 1
        in_specs.append(pl.BlockSpec((1, tb, D_CONV), lambda b, i: (layer, i, 0)))
        args.append(cprev)
    in_specs += [
        _layer_spec(layer, CONV_W, D_CONV),
        _layer_spec(layer, 8, LANES),
        _layer_spec(layer, 8, LANES),
        _layer_spec(layer, N_GATE, LANES),
        _layer_spec(layer, N_GATE, LANES),
    ]
    args += [cw, pa, pl_, ra, rl]
    n = t // c
    hd = N_HEADS * DH
    out_shape = [
        jax.ShapeDtypeStruct((nb, n, 2 * c, hd), idt),
        jax.ShapeDtypeStruct((nb, n, c, hd), F32),
        jax.ShapeDtypeStruct((nb, n, N_HEADS // 2, c, 2 * c), idt),
        jax.ShapeDtypeStruct((nb, n, N_HEADS // 2, DH, 2 * c), idt),
        jax.ShapeDtypeStruct((nb, n, 8, hd), F32),
    ]
    out_specs = [
        pl.BlockSpec((1, kb, 2 * c, hd), lambda b, i: (b, i, 0, 0)),
        pl.BlockSpec((1, kb, c, hd), lambda b, i: (b, i, 0, 0)),
        pl.BlockSpec((1, kb, N_HEADS // 2, c, 2 * c), lambda b, i: (b, i, 0, 0, 0)),
        pl.BlockSpec((1, kb, N_HEADS // 2, DH, 2 * c), lambda b, i: (b, i, 0, 0, 0)),
        pl.BlockSpec((1, kb, 8, hd), lambda b, i: (b, i, 0, 0)),
    ]
    return pl.pallas_call(
        functools.partial(_gdn_prep_kernel, tb=tb, seq_len=seq_len, valid_lo=valid_lo, idt=idt),
        grid=(nb, nt),
        in_specs=in_specs,
        out_specs=out_specs,
        out_shape=out_shape,
        compiler_params=_cparams(2),
        name="gdn_prep",
    )(*args)


def _lane_half_masks(dtype):
    lo = lax.broadcasted_iota(jnp.int32, (CHUNK, 2 * CHUNK), 1) < CHUNK
    return jnp.where(lo, 1.0, 0.0).astype(dtype), jnp.where(lo, 0.0, 1.0).astype(dtype)


def _gated_norm_store(o, z, gn, mix_ref, idx):
    mix_ref[idx] = (_rms(o, gn) * jax.nn.silu(z)).astype(BF16)


def _gdn_scan_prompt_kernel(wkqg_ref, wv_ref, qk_ref, kdt_ref, el_ref, z_ref, gn_ref,
                            mix_ref, sout_ref, s_scr, *, nc, bs):
    c = CHUNK

    @pl.when(pl.program_id(1) == 0)
    def _():
        s_scr[...] = jnp.zeros_like(s_scr)

    gn = gn_ref[...]
    units = [(b, h, slice(DH * h, DH * (h + 1))) for b in range(bs) for h in range(N_HEADS)]
    half_b = _lane_half_masks(BF16)
    half2_b = tuple(jnp.concatenate([m, m], axis=0) for m in half_b)

    def body(n, carry):
        rows = pl.ds(pl.multiple_of(n * c, c), c)
        ss = [s_scr[b * N_HEADS + h] for b, h, _ in units]
        aa = [_dot(wkqg_ref[b, n, :, hs], s.astype(BF16)) for (b, _, hs), s in zip(units, ss)]
        ubs = [(wv_ref[b, n, :, hs] - a[0:c]).astype(BF16) for (b, _, hs), a in zip(units, aa)]
        os_, ds = [], []
        for j in range(len(units) // 2):
            b, h, _ = units[2 * j]
            qkp, kdp = qk_ref[b, n, h // 2], kdt_ref[b, n, h // 2]
            r = _dot(jnp.concatenate([qkp * half_b[0], qkp * half_b[1], kdp * half2_b[0], kdp * half2_b[1]],
                                     axis=0),
                     jnp.concatenate(ubs[2 * j:2 * j + 2], axis=0))
            os_ += [r[0:c], r[c:2 * c]]
            ds += [r[2 * c:2 * c + DH], r[2 * c + DH:2 * c + 2 * DH]]
        for i, (b, h, hs) in enumerate(units):
            s_scr[b * N_HEADS + h] = el_ref[b, n, 0:1, hs] * ss[i] + ds[i]
            _gated_norm_store(aa[i][c:2 * c] + os_[i], z_ref[b, rows, hs], gn, mix_ref, (b, rows, hs))
        return carry

    lax.fori_loop(0, nc, body, 0)

    @pl.when(pl.program_id(1) == pl.num_programs(1) - 1)
    def _():
        for b in range(bs):
            sout_ref[b] = s_scr[b * N_HEADS:(b + 1) * N_HEADS]


def _gdn_scan_prompt(wkqg, wv, qk, kdt, el, proj3, gn, layer, *, ts, bs):
    nb, n, _, hd = wkqg.shape
    c = CHUNK
    t = n * c
    nc = ts // c
    zblk = D_CONV // hd
    return pl.pallas_call(
        functools.partial(_gdn_scan_prompt_kernel, nc=nc, bs=bs),
        grid=(nb // bs, t // ts),
        in_specs=[
            pl.BlockSpec((bs, nc, 2 * c, hd), lambda b, i: (b, i, 0, 0)),
            pl.BlockSpec((bs, nc, c, hd), lambda b, i: (b, i, 0, 0)),
            pl.BlockSpec((bs, nc, N_HEADS // 2, c, 2 * c), lambda b, i: (b, i, 0, 0, 0)),
            pl.BlockSpec((bs, nc, N_HEADS // 2, DH, 2 * c), lambda b, i: (b, i, 0, 0, 0)),
            pl.BlockSpec((bs, nc, 8, hd), lambda b, i: (b, i, 0, 0)),
            pl.BlockSpec((bs, ts, hd), lambda b, i: (b, i, zblk)),
            _layer_spec(layer, 1, DH),
        ],
        out_specs=[
            pl.BlockSpec((bs, ts, hd), lambda b, i: (b, i, 0)),
            pl.BlockSpec((bs, N_HEADS, DH, DH), lambda b, i: (b, 0, 0, 0)),
        ],
        out_shape=[jax.ShapeDtypeStruct((nb, t, hd), BF16),
                   jax.ShapeDtypeStruct((nb, N_HEADS, DH, DH), F32)],
        scratch_shapes=[pltpu.VMEM((bs * N_HEADS, DH, DH), F32)],
        compiler_params=_cparams(2),
        name="gdn_scan_prompt",
    )(wkqg, wv, qk, kdt, el, proj3, gn)


def _gdn_scan_decode_kernel(*refs, n_prev):
    (wkqg_ref, wv_ref, qk_ref, kdt_ref, el_ref, z_ref, gn_ref, s0_ref), rest = refs[:8], refs[8:]
    prev_refs, (mix_ref, sout_ref) = rest[:n_prev], rest[n_prev:]
    if n_prev:
        for l, p_ref in enumerate(prev_refs):
            sout_ref[l] = p_ref[...]
        sout_ref = sout_ref.at[n_prev]
    c = CHUNK
    nseq = c // SLOT
    rowseq = lax.broadcasted_iota(jnp.int32, (c, 1), 0) >> 3
    gn = gn_ref[...]
    heads = [slice(DH * h, DH * (h + 1)) for h in range(N_HEADS)]
    aa = []
    for h, hs in enumerate(heads):
        w = wkqg_ref[0, 0, :, hs]
        row = []
        for j in range(nseq):
            wj = jnp.concatenate([w[SLOT * j:SLOT * (j + 1)], w[c + SLOT * j:c + SLOT * (j + 1)]],
                                 axis=0).astype(BF16)
            row.append(_dot(wj, s0_ref[j, h].astype(BF16)))
        aa.append(row)
    us = [wv_ref[0, 0, :, hs] - jnp.concatenate([a[0:SLOT] for a in aa[h]], axis=0)
          for h, hs in enumerate(heads)]
    half_f = _lane_half_masks(F32)
    os_ = []
    for j in range(N_HEADS // 2):
        qkp = qk_ref[0, 0, j]
        o2 = _dot(jnp.concatenate([qkp * half_f[0], qkp * half_f[1]], axis=0).astype(BF16),
                  jnp.concatenate(us[2 * j:2 * j + 2], axis=0).astype(BF16))
        os_ += [o2[0:c], o2[c:2 * c]]
    half2_f = tuple(jnp.concatenate([m, m], axis=0) for m in half_f)
    for p in range(N_HEADS // 2):
        kdp = kdt_ref[0, 0, p]
        kd2 = jnp.concatenate([kdp * half2_f[0], kdp * half2_f[1]], axis=0).astype(BF16)
        for j in range(nseq):
            uj = jnp.concatenate([jnp.where(rowseq == j, us[2 * p + i], 0.0) for i in range(2)], axis=0)
            r = _dot(kd2, uj.astype(BF16))
            for i in range(2):
                h = 2 * p + i
                sout_ref[j, h] = (el_ref[0, 0, j:j + 1, heads[h]] * s0_ref[j, h] + r[DH * i:DH * (i + 1)])
    for h, hs in enumerate(heads):
        o = jnp.concatenate([a[SLOT:2 * SLOT] for a in aa[h]], axis=0) + os_[h]
        _gated_norm_store(o, z_ref[:, hs], gn, mix_ref, (slice(None), hs))


def _state_specs(layer, prev, state_shape, nseq):
    tail = state_shape[2:]
    zeros = (0,) * len(tail)
    in_specs = [pl.BlockSpec((None, nseq) + tail, lambda i: (layer, i) + zeros)]
    in_specs += [pl.BlockSpec((nseq,) + tail, lambda i: (i,) + zeros) for _ in prev]
    if prev:
        out_spec = pl.BlockSpec((len(prev) + 1, nseq) + tail, lambda i: (0, i) + zeros)
        out_shape = jax.ShapeDtypeStruct((len(prev) + 1,) + state_shape[1:], F32)
    else:
        out_spec = pl.BlockSpec((nseq,) + tail, lambda i: (i,) + zeros)
        out_shape = jax.ShapeDtypeStruct(state_shape[1:], F32)
    return in_specs, out_spec, out_shape


def _gdn_scan_decode(wkqg, wv, qk, kdt, el, proj, gn, s0_all, prev, layer):
    _, n, _, hd = wkqg.shape
    c = CHUNK
    nseq = c // SLOT
    zblk = D_CONV // hd
    st_in, st_out, st_shape = _state_specs(layer, prev, s0_all.shape, nseq)
    return pl.pallas_call(
        functools.partial(_gdn_scan_decode_kernel, n_prev=len(prev)),
        grid=(n,),
        in_specs=[
            pl.BlockSpec((1, 1, 2 * c, hd), lambda i: (0, i, 0, 0)),
            pl.BlockSpec((1, 1, c, hd), lambda i: (0, i, 0, 0)),
            pl.BlockSpec((1, 1, N_HEADS // 2, c, 2 * c), lambda i: (0, i, 0, 0, 0)),
            pl.BlockSpec((1, 1, N_HEADS // 2, DH, 2 * c), lambda i: (0, i, 0, 0, 0)),
            pl.BlockSpec((1, 1, 8, hd), lambda i: (0, i, 0, 0)),
            pl.BlockSpec((c, hd), lambda i: (i, zblk)),
            _layer_spec(layer, 1, DH),
        ] + st_in,
        out_specs=[pl.BlockSpec((c, hd), lambda i: (i, 0)), st_out],
        out_shape=[jax.ShapeDtypeStruct((n * c, hd), BF16), st_shape],
        compiler_params=_cparams(1),
        name="gdn_scan_decode",
    )(wkqg, wv, qk, kdt, el, proj, gn, s0_all, *prev)


def _mlstm_gates(gcol, growk, pa_ref, ra_ref, h, rows, valid_lo, rowpos, colpos):
    xg = gcol + pa_ref[0:1, :]
    ig_col = xg[:, 2 * N_HEADS + h:2 * N_HEADS + h + 1]
    fl_col = -_softplus(-xg[:, 3 * N_HEADS + h:3 * N_HEADS + h + 1])
    xr = growk + ra_ref[:, 0:1]
    ig_row = xr[2 * N_HEADS + h:2 * N_HEADS + h + 1, :]
    fl_row = -_softplus(-xr[3 * N_HEADS + h:3 * N_HEADS + h + 1, :])
    if valid_lo > 0:
        ig_col = jnp.where(rowpos >= valid_lo, ig_col, NEG)
        fl_col = jnp.where(rowpos >= valid_lo, fl_col, 0.0)
        ig_row = jnp.where(colpos >= valid_lo, ig_row, NEG)
        fl_row = jnp.where(colpos >= valid_lo, fl_row, 0.0)
    return ig_col, fl_col, ig_row, fl_row


def _mlstm_chunks(probs, seq_len):
    c = CHUNK
    nseq = c // seq_len
    ri, ci, same = _chunk_masks(seq_len)
    causal = (ri >= ci) & same
    upper = (ri <= ci) & same
    seq_end = ci == (ri | (seq_len - 1))
    rowseq = lax.broadcasted_iota(jnp.int32, (c, 1), 0) >> (seq_len.bit_length() - 1)
    zpad = jnp.zeros((16 - seq_len, DH), F32) if nseq > 1 else None
    for p in probs:
        p["f_col"] = jnp.sum(jnp.where(causal, p["fl_row"], 0.0), axis=1, keepdims=True)
        p["f_row"] = jnp.sum(jnp.where(upper, p["fl_col"], 0.0), axis=0, keepdims=True)
        p["km"] = p["k"] * (DH ** -0.5)
        p["qb"] = p["q"].astype(BF16)
        p["kb"] = p["km"].astype(BF16)
        p["vb"] = p["v"].astype(BF16)
        if nseq == 1:
            p["qn"] = _dot_nt(p["qb"], p["n_rows"].astype(BF16))[:, 0:1]
        else:
            p["qn"] = jnp.sum(p["q"] * p["n_rows"], axis=1, keepdims=True)
    for p in probs:
        p["b_col"] = p["ig_col"] - p["f_col"]
        p["b_row"] = p["ig_row"] - p["f_row"]
        p["bmax_col"] = jnp.max(jnp.where(causal, p["b_row"], -jnp.inf), axis=1, keepdims=True)
        if nseq > 1:
            p["fl_end"] = jnp.sum(jnp.where(seq_end, p["f_row"], 0.0), axis=1, keepdims=True)
            p["bmax_end"] = jnp.max(jnp.where(same, p["b_row"], -jnp.inf), axis=1, keepdims=True)
    for p in probs:
        f_col, mp_col = p["f_col"], p["mp_col"]
        m_col = f_col + jnp.maximum(mp_col, p["bmax_col"])
        p["m_col"] = m_col
        p["a_col"] = jnp.exp(f_col + mp_col - m_col)
        p["dexp"] = jnp.where(causal, jnp.exp(jnp.where(causal, (f_col - m_col) + p["b_row"], 0.0)), 0.0)
        if nseq == 1:
            fl_end, ml, mp = f_col[c - 1:c], m_col[c - 1:c], mp_col[c - 1:c]
            p["ml_col"] = jnp.broadcast_to(ml, (c, 1))
            p["al"] = jnp.broadcast_to(jnp.exp(fl_end + mp - ml), (1, DH))
        else:
            fl_end, mp = p["fl_end"], mp_col
            ml = fl_end + jnp.maximum(mp_col, p["bmax_end"])
            p["ml_col"] = ml
            p["al"] = jnp.broadcast_to(jnp.exp(fl_end + mp - ml), (c, DH))
        p["kw"] = p["km"] * jnp.exp(fl_end + p["b_col"] - ml)
    for p in probs:
        p["kwt"] = p["kw"].T.astype(BF16)
    for p in probs:
        p["qk"] = _dot_nt(p["qb"], p["kb"])
    for p in probs:
        if nseq == 1:
            p["qc"] = _dot(p["qb"], p["c_list"][0].astype(BF16))
            p["upd"] = [_dot(p["kwt"], p["vb"])]
        else:
            parts, upd = [], []
            for j in range(nseq):
                qj = jnp.concatenate([p["q"][seq_len * j:seq_len * (j + 1)], zpad], axis=0).astype(BF16)
                parts.append(_dot(qj, p["c_list"][j].astype(BF16))[0:seq_len])
                upd.append(_dot(p["kwt"], jnp.where(rowseq == j, p["v"], 0.0).astype(BF16)))
            p["qc"] = jnp.concatenate(parts, axis=0)
            p["upd"] = upd
    if nseq == 1:
        ones_b = jnp.ones((c, LANES), BF16)
        for p in probs:
            p["w_hl"] = _split(p["dexp"] * p["qk"])
        for p in probs:
            r = _dot(jnp.concatenate(p["w_hl"], axis=0), jnp.concatenate([p["vb"], ones_b], axis=1))
            p["wv"] = r[0:c, 0:DH]
            p["wsum"] = r[0:c, DH:DH + 1] + r[c:2 * c, DH:DH + 1]
    else:
        for p in probs:
            p["w"] = p["dexp"] * p["qk"]
            p["wsum"] = jnp.sum(p["w"], axis=1, keepdims=True)
        for p in probs:
            p["wv"] = _dot(p["w"].astype(BF16), p["vb"])
    for p in probs:
        num = p["a_col"] * p["qc"] + p["wv"]
        den = p["a_col"] * p["qn"] + p["wsum"]
        p["hout"] = num / jnp.maximum(jnp.abs(den), jnp.exp(-p["m_col"]))
        p["new_c"] = [p["al"][seq_len * j:seq_len * j + 1] * p["c_list"][j] + p["upd"][j]
                      for j in range(nseq)]
    return probs


def _mlstm_out_store(hout, og, gnorm, mix_ref, idx):
    x = hout * jax.nn.sigmoid(og)
    ssq = _dot(jnp.concatenate(_split(x * x), axis=0), jnp.ones((DH, DH), BF16))
    ms = (ssq[0:CHUNK] + ssq[CHUNK:2 * CHUNK]) * (1.0 / DH)
    mix_ref[idx] = (x * lax.rsqrt(ms + EPS) * gnorm).astype(BF16)


def _mlstm_prompt_kernel(q_ref, k_ref, v_ref, o_ref, gcol_ref, grow_ref, pa_ref, ra_ref, nrm_ref,
                         mix_ref, cout_ref, nout_ref, mout_ref, c_scr, n_scr, m_scr, *, nc, bs):
    c = CHUNK

    @pl.when(pl.program_id(1) == 0)
    def _():
        c_scr[...] = jnp.zeros_like(c_scr)
        n_scr[...] = jnp.zeros_like(n_scr)
        m_scr[...] = jnp.zeros_like(m_scr)

    def body(n, carry):
        r0 = pl.multiple_of(n * c, c)
        rows = pl.ds(r0, c)
        probs = []
        for b in range(bs):
            gcol = gcol_ref[b, rows, :]
            growk = grow_ref[b, n]
            for h in range(N_HEADS):
                hs = slice(DH * h, DH * (h + 1))
                ig_col, fl_col, ig_row, fl_row = _mlstm_gates(gcol, growk, pa_ref, ra_ref, h, c, 0, None, None)
                st = b * N_HEADS + h
                probs.append(dict(b=b, hs=hs, st=st, q=q_ref[b, rows, hs], k=k_ref[b, rows, hs],
                                  v=v_ref[b, rows, hs], ig_col=ig_col, fl_col=fl_col, ig_row=ig_row,
                                  fl_row=fl_row, mp_col=m_scr[st][:, 0:1], n_rows=n_scr[st],
                                  c_list=[c_scr[st]]))
        for p in _mlstm_chunks(probs, c):
            st = p["st"]
            c_scr[st] = p["new_c"][0]
            n_scr[st] = jnp.broadcast_to(
                p["al"][0:1] * p["n_rows"][0:1] + jnp.sum(p["kw"], axis=0, keepdims=True), (c, DH))
            m_scr[st] = jnp.broadcast_to(p["ml_col"], (c, DH))
            _mlstm_out_store(p["hout"], o_ref[p["b"], rows, p["hs"]], nrm_ref[:, p["hs"]], mix_ref,
                             (p["b"], rows, p["hs"]))
        return carry

    lax.fori_loop(0, nc, body, 0)

    @pl.when(pl.program_id(1) == pl.num_programs(1) - 1)
    def _():
        lane = lax.broadcasted_iota(jnp.int32, (8, DH), 1)
        for b in range(bs):
            mo = jnp.zeros((8, DH), F32)
            for h in range(N_HEADS):
                st = b * N_HEADS + h
                cout_ref[b, h] = c_scr[st]
                nout_ref[b, h:h + 1, :] = n_scr[st][0:1]
                mo = jnp.where(lane == h, m_scr[st][0:8], mo)
            mout_ref[b] = mo


def _mlstm_prompt(proj3, grow4, pa, ra, nrm, layer, *, ts, bs):
    nb, t, _ = proj3.shape
    c = CHUNK
    hd = N_HEADS * DH
    nc = ts // c
    qblk = (D_CONV + hd) // hd
    return pl.pallas_call(
        functools.partial(_mlstm_prompt_kernel, nc=nc, bs=bs),
        grid=(nb // bs, t // ts),
        in_specs=[
            pl.BlockSpec((bs, ts, hd), lambda b, i: (b, i, qblk)),
            pl.BlockSpec((bs, ts, hd), lambda b, i: (b, i, qblk + 1)),
            pl.BlockSpec((bs, ts, hd), lambda b, i: (b, i, qblk + 2)),
            pl.BlockSpec((bs, ts, hd), lambda b, i: (b, i, qblk + 3)),
            pl.BlockSpec((bs, ts, LANES), lambda b, i: (b, i, GATE_COL_BLOCK)),
            pl.BlockSpec((bs, nc, N_GATE, c), lambda b, i: (b, i, 0, 0)),
            _layer_spec(layer, 8, LANES),
            _layer_spec(layer, N_GATE, LANES),
            _layer_spec(layer, 1, hd),
        ],
        out_specs=[
            pl.BlockSpec((bs, ts, hd), lambda b, i: (b, i, 0)),
            pl.BlockSpec((bs, N_HEADS, DH, DH), lambda b, i: (b, 0, 0, 0)),
            pl.BlockSpec((bs, N_HEADS, DH), lambda b, i: (b, 0, 0)),
            pl.BlockSpec((bs, 8, DH), lambda b, i: (b, 0, 0)),
        ],
        out_shape=[jax.ShapeDtypeStruct((nb, t, hd), BF16),
                   jax.ShapeDtypeStruct((nb, N_HEADS, DH, DH), F32),
                   jax.ShapeDtypeStruct((nb, N_HEADS, DH), F32),
                   jax.ShapeDtypeStruct((nb, 8, DH), F32)],
        scratch_shapes=[pltpu.VMEM((bs * N_HEADS, DH, DH), F32),
                        pltpu.VMEM((bs * N_HEADS, c, DH), F32),
                        pltpu.VMEM((bs * N_HEADS, c, DH), F32)],
        compiler_params=_cparams(2),
        name="mlstm_prompt",
    )(proj3, proj3, proj3, proj3, proj3, grow4, pa, ra, nrm)


def _mlstm_decode_kernel(*refs, n_prev):
    (q_ref, k_ref, v_ref, o_ref, gcol_ref, grow_ref, pa_ref, ra_ref, nrm_ref, n0_ref, m0_ref,
     c0_ref), rest = refs[:12], refs[12:]
    prev_refs, (mix_ref, nout_ref, mout_ref, cout_ref) = rest[:n_prev], rest[n_prev:]
    if n_prev:
        for l, p_ref in enumerate(prev_refs):
            cout_ref[l] = p_ref[...]
        cout_ref = cout_ref.at[n_prev]
    c = CHUNK
    nseq = c // SLOT
    rowpos = lax.broadcasted_iota(jnp.int32, (c, 1), 0) & (SLOT - 1)
    colpos = lax.broadcasted_iota(jnp.int32, (1, c), 1) & (SLOT - 1)
    lane = lax.broadcasted_iota(jnp.int32, (c, DH), 1)
    gcol = gcol_ref[...]
    growk = grow_ref[0]
    probs = []
    for h in range(N_HEADS):
        hs = slice(DH * h, DH * (h + 1))
        ig_col, fl_col, ig_row, fl_row = _mlstm_gates(gcol, growk, pa_ref, ra_ref, h, c, SLOT_PAD, rowpos, colpos)
        n_rows = jnp.concatenate(
            [jnp.broadcast_to(n0_ref[j, h:h + 1, :], (SLOT, DH)) for j in range(nseq)], axis=0)
        probs.append(dict(h=h, hs=hs, q=q_ref[:, hs], k=k_ref[:, hs], v=v_ref[:, hs], ig_col=ig_col,
                          fl_col=fl_col, ig_row=ig_row, fl_row=fl_row, mp_col=m0_ref[:, h:h + 1],
                          n_rows=n_rows, c_list=[c0_ref[j, h] for j in range(nseq)]))
    mo = jnp.zeros((c, DH), F32)
    for p in _mlstm_chunks(probs, SLOT):
        h, hs = p["h"], p["hs"]
        for j in range(nseq):
            cout_ref[j, h] = p["new_c"][j]
            rs = slice(SLOT * j, SLOT * (j + 1))
            nout_ref[j, h:h + 1, :] = (p["al"][SLOT * j:SLOT * j + 1] * n0_ref[j, h:h + 1, :]
                                       + jnp.sum(p["kw"][rs], axis=0, keepdims=True))
        mo = jnp.where(lane == h, jnp.broadcast_to(p["ml_col"], (c, DH)), mo)
        _mlstm_out_store(p["hout"], o_ref[:, hs], nrm_ref[:, hs], mix_ref, (slice(None), hs))
    mout_ref[...] = mo


def _mlstm_decode(proj, grow, pa, ra, nrm, c0_all, n0_all, m0rows_all, prev, layer):
    rows = proj.shape[0]
    c = CHUNK
    hd = N_HEADS * DH
    nseq = c // SLOT
    qblk = (D_CONV + hd) // hd
    st_in, st_out, st_shape = _state_specs(layer, prev, c0_all.shape, nseq)
    return pl.pallas_call(
        functools.partial(_mlstm_decode_kernel, n_prev=len(prev)),
        grid=(rows // c,),
        in_specs=[
            pl.BlockSpec((c, hd), lambda i: (i, qblk)),
            pl.BlockSpec((c, hd), lambda i: (i, qblk + 1)),
            pl.BlockSpec((c, hd), lambda i: (i, qblk + 2)),
            pl.BlockSpec((c, hd), lambda i: (i, qblk + 3)),
            pl.BlockSpec((c, LANES), lambda i: (i, GATE_COL_BLOCK)),
            pl.BlockSpec((1, N_GATE, c), lambda i: (i, 0, 0)),
            _layer_spec(layer, 8, LANES),
            _layer_spec(layer, N_GATE, LANES),
            _layer_spec(layer, 1, hd),
            pl.BlockSpec((None, nseq, N_HEADS, DH), lambda i: (layer, i, 0, 0)),
            pl.BlockSpec((None, c, LANES), lambda i: (layer, i, 0)),
        ] + st_in,
        out_specs=[
            pl.BlockSpec((c, hd), lambda i: (i, 0)),
            pl.BlockSpec((nseq, N_HEADS, DH), lambda i: (i, 0, 0)),
            pl.BlockSpec((c, LANES), lambda i: (i, 0)),
            st_out,
        ],
        out_shape=[jax.ShapeDtypeStruct((rows, hd), BF16),
                   jax.ShapeDtypeStruct(n0_all.shape[1:], F32),
                   jax.ShapeDtypeStruct((rows, LANES), F32),
                   st_shape],
        compiler_params=_cparams(1),
        name="mlstm_decode",
    )(proj, proj, proj, proj, proj, grow, pa, ra, nrm, n0_all, m0rows_all, c0_all, *prev)


def _post_kernel(x_ref, ma_ref, mb_ref, p_ref, woa_ref, wob_ref, gf_ref, wg_ref, wu_ref, wd_ref,
                 gp_ref, wpg_ref, wpp_ref, gfin_ref, o_ref, acc_ref, *, final):
    x = x_ref[...] + (_dot(ma_ref[...], woa_ref[...]) + _dot(mb_ref[...], wob_ref[...]))
    ub = _rms(x, gf_ref[...]).astype(BF16)
    for j in range(D_FF // FF_CHUNK):
        sl = slice(j * FF_CHUNK, (j + 1) * FF_CHUNK)
        a = (jax.nn.silu(_dot(ub, wg_ref[:, sl])) * _dot(ub, wu_ref[:, sl])).astype(BF16)
        d = _dot(a, wd_ref[sl, :])
        if j == 0:
            acc_ref[...] = d
        else:
            acc_ref[...] += d
    x = x + acc_ref[...]
    gate = jax.nn.sigmoid(_dot(_rms(x, gp_ref[...]).astype(BF16), wpg_ref[...]))
    x = x + _dot(p_ref[...].astype(BF16), wpp_ref[...]) * gate
    if final:
        x = _rms(x, gfin_ref[...])
    o_ref[...] = x


def _post(x, ma, mb, p_all, wo, gf, wg, wu, wd, gp, wpg, wpp, gfin, layer, *, final):
    rows = x.shape[0]
    tm = ROW_TILE
    hd = N_HEADS * DH
    row = lambda w: pl.BlockSpec((tm, w), lambda i: (i, 0))

    def whole(a, b, blk=0):
        return pl.BlockSpec((None, a, b), lambda i: (layer, blk, 0), pipeline_mode=pl.Buffered(1))

    return pl.pallas_call(
        functools.partial(_post_kernel, final=final),
        grid=(rows // tm,),
        in_specs=[row(D_MODEL), row(hd), row(hd),
                  pl.BlockSpec((None, tm, D_PLE), lambda i: (layer, i, 0)),
                  whole(hd, D_MODEL, 0), whole(hd, D_MODEL, 1), whole(1, D_MODEL),
                  whole(D_MODEL, D_FF), whole(D_MODEL, D_FF), whole(D_FF, D_MODEL),
                  whole(1, D_MODEL), whole(D_MODEL, D_MODEL), whole(D_PLE, D_MODEL),
                  pl.BlockSpec((1, D_MODEL), lambda i: (0, 0))],
        out_specs=row(D_MODEL),
        out_shape=jax.ShapeDtypeStruct((rows, D_MODEL), F32),
        scratch_shapes=[pltpu.VMEM((tm, D_MODEL), F32)],
        compiler_params=_cparams(1),
        name="post",
    )(x, ma, mb, p_all, wo, wo, gf, wg, wu, wd, gp, wpg, wpp, gfin)


def _gate_vec(pairs, depth):
    v = jnp.zeros((depth, N_GATE), F32)
    for off, val in pairs:
        v = v.at[:, off:off + N_HEADS].set(val.astype(F32))
    return v


def _lane_form(v):
    d = v.shape[0]
    return jnp.broadcast_to(jnp.pad(v, ((0, 0), (0, LANES - N_GATE)))[:, None, :], (d, 8, LANES))


def _row_form(v):
    d = v.shape[0]
    return jnp.broadcast_to(v[:, :, None], (d, N_GATE, LANES))


def kernel(x_prompt, x_sample, p_prompt, p_sample, state_gdn, state_gdn_conv, state_mlstm_C, state_mlstm_n, state_mlstm_m, w_in, conv_w, gdn_a_log, gdn_dt_bias, gdn_norm, mlstm_i_bias, mlstm_f_bias, mlstm_norm, w_out, norm_mix, norm_ffn, w_gate, w_up, w_down, norm_ple, w_ple_gate, w_ple_proj, norm_final):
    depth = w_in.shape[0]
    nb, t, _ = x_prompt.shape
    ns, tdec, _ = x_sample.shape
    hd = N_HEADS * DH
    c = CHUNK
    pad = SLOT - tdec

    xp = x_prompt.reshape(nb * t, D_MODEL)
    xs = jnp.pad(x_sample, ((0, 0), (pad, 0), (0, 0))).reshape(ns * SLOT, D_MODEL)
    ps_all = jnp.pad(p_sample, ((0, 0), (0, 0), (pad, 0), (0, 0))).reshape(depth, ns * SLOT, D_PLE)
    pp_all = p_prompt.reshape(depth, nb * t, D_PLE)
    gfin = norm_final.reshape(1, D_MODEL)

    o = D_CONV + hd
    w_g = jnp.concatenate([w_in[:, :, o:o + 2 * N_HEADS], w_in[:, :, o + 2 * N_HEADS + 4 * hd:]], axis=2)
    w_r = jnp.concatenate([w_in[:, :, :o], w_in[:, :, o + 2 * N_HEADS:o + 2 * N_HEADS + 4 * hd], w_g,
                           jnp.zeros((depth, D_MODEL, LANES - N_GATE), F32)], axis=2).astype(BF16)
    adds = _gate_vec([(N_HEADS, gdn_dt_bias), (2 * N_HEADS, mlstm_i_bias), (3 * N_HEADS, mlstm_f_bias)], depth)
    alog = _gate_vec([(N_HEADS, gdn_a_log)], depth)
    pa, pl_, ra, rl = _lane_form(adds), _lane_form(alog), _row_form(adds), _row_form(alog)
    g_mix = norm_mix.reshape(depth, 1, D_MODEL)
    cw = conv_w.astype(F32)
    gn = gdn_norm.reshape(depth, 1, DH).astype(F32)
    nrm = mlstm_norm.reshape(depth, 1, hd).astype(F32)
    wts = (w_out.astype(BF16), norm_ffn.reshape(depth, 1, D_MODEL), w_gate.astype(BF16), w_up.astype(BF16),
           w_down.astype(BF16), norm_ple.reshape(depth, 1, D_MODEL), w_ple_gate.astype(BF16),
           w_ple_proj.astype(BF16), gfin)
    cprev = jnp.pad(state_gdn_conv.astype(F32), ((0, 0), (0, 0), (pad - (CONV_W - 1), SLOT - pad), (0, 0)))
    cprev = cprev.reshape(depth, ns * SLOT, D_CONV)
    m0rows = jnp.pad(jnp.repeat(state_mlstm_m.astype(F32), SLOT, axis=1), ((0, 0), (0, 0), (0, LANES - N_HEADS)))
    s0_all, c0_all, n0_all = state_gdn.astype(F32), state_mlstm_C.astype(F32), state_mlstm_n.astype(F32)

    outs_p = [[] for _ in range(5)]
    outs_s = [[] for _ in range(5)]
    for i in range(depth):
        final = i == depth - 1
        proj_p, gt_p = _in_proj(xp, g_mix, w_r, i)
        proj_s, gt_s = _in_proj(xs, g_mix, w_r, i)
        proj_p3 = proj_p.reshape(nb, t, N_PROJ)
        proj_s3 = proj_s.reshape(1, ns * SLOT, N_PROJ)
        grow_p = gt_p.reshape(N_GATE, nb * t // c, c).transpose(1, 0, 2)
        grow_s = gt_s.reshape(N_GATE, ns * SLOT // c, c).transpose(1, 0, 2)

        prep_p = _gdn_prep(proj_p3, grow_p, None, cw, pa, pl_, ra, rl, i, seq_len=c, valid_lo=0, idt=BF16, tb=512)
        mixa_p, s_p = _gdn_scan_prompt(*prep_p, proj_p3, gn, i, ts=512, bs=2)
        prep_s = _gdn_prep(proj_s3, grow_s, cprev, cw, pa, pl_, ra, rl, i, seq_len=SLOT, valid_lo=pad, idt=F32, tb=c)
        mixa_s, s_s = _gdn_scan_decode(*prep_s, proj_s, gn, s0_all, outs_s[0] if final else [], i)

        mixb_p, c_p, n_p, m_p = _mlstm_prompt(proj_p3, grow_p.reshape(nb, t // c, N_GATE, c), pa, ra, nrm, i,
                                              ts=512, bs=4)
        mixb_s, n_s, m_s, c_s = _mlstm_decode(proj_s, grow_s, pa, ra, nrm, c0_all, n0_all, m0rows,
                                              outs_s[2] if final else [], i)

        xp = _post(xp, mixa_p.reshape(nb * t, hd), mixb_p.reshape(nb * t, hd), pp_all, *wts, i, final=final)
        xs = _post(xs, mixa_s, mixb_s, ps_all, *wts, i, final=final)

        outs_p[0].append(s_p)
        outs_p[1].append(proj_p3[:, t - (CONV_W - 1):, :D_CONV])
        outs_p[2].append(c_p)
        outs_p[3].append(n_p)
        outs_p[4].append(m_p[:, 0, :N_HEADS])
        outs_s[0].append(s_s)
        outs_s[1].append(proj_s.reshape(ns, SLOT, N_PROJ)[:, SLOT - (CONV_W - 1):, :D_CONV])
        outs_s[2].append(c_s)
        outs_s[3].append(n_s)
        outs_s[4].append(m_s.reshape(ns, SLOT, LANES)[:, SLOT - 1, :N_HEADS])

    y_prompt = xp.reshape(nb, t, D_MODEL)
    y_sample = xs.reshape(ns, SLOT, D_MODEL)[:, pad:, :]
    sp = [jnp.stack(a, axis=0) for a in outs_p]
    ss = [outs_s[j][-1] if j in (0, 2) else jnp.stack(outs_s[j], axis=0) for j in range(5)]
    return (y_prompt, y_sample, *sp, *ss)
```

```python
import functools

import jax
import jax.numpy as jnp
from jax import lax
from jax.experimental import pallas as pl
from jax.experimental.pallas import tpu as pltpu

F32 = jnp.float32
BF16 = jnp.bfloat16

D_MODEL = 1024
N_HEADS = 4
DH = 128
D_CONV = 3 * N_HEADS * DH
D_FF = 2816
D_PLE = 256
CONV_W = 4
CHUNK = 64
EPS = 1e-6
NEG = -1e30
N_GATE = 16
LANES = 128
N_PROJ = D_CONV + 5 * N_HEADS * DH + LANES
GATE_COL_BLOCK = (N_PROJ - LANES) // LANES
SLOT = 8
SLOT_PAD = 4
V7X_VMEM_LIMIT_BYTES = 56 * 1024 * 1024
ROW_TILE = 512
FF_CHUNK = 256


def _cparams(n_axes):
    return pltpu.CompilerParams(dimension_semantics=("arbitrary",) * n_axes,
                                vmem_limit_bytes=V7X_VMEM_LIMIT_BYTES)


def _rms(x, g):
    return x * lax.rsqrt(jnp.mean(x * x, axis=-1, keepdims=True) + EPS) * g


def _softplus(x):
    return jnp.maximum(x, 0.0) + jnp.log1p(jnp.exp(-jnp.abs(x)))


def _dot(a, b):
    return jnp.dot(a, b, preferred_element_type=F32)


def _dot_nt(a, b):
    return lax.dot_general(a, b, (((1,), (1,)), ((), ())), preferred_element_type=F32)


def _split(a):
    hi = a.astype(BF16)
    lo = (a - hi.astype(F32)).astype(BF16)
    return hi, lo


def _split_all(xs):
    return [_split(x) for x in xs]


def _mm3_all(a_list, b_list):
    out = []
    for (ah, al), (bh, bl) in zip(a_list, b_list):
        m = ah.shape[0]
        r = _dot(jnp.concatenate([ah, al], axis=0), bh)
        out.append(r[0:m] + (_dot(ah, bl) + r[m:2 * m]))
    return out


class _PairMasks:
    def __init__(self, seq_len):
        c = CHUNK
        self.r = lax.broadcasted_iota(jnp.int32, (c, 2 * c), 0)
        lane = lax.broadcasted_iota(jnp.int32, (c, 2 * c), 1)
        self.cc = lane & (c - 1)
        self.lo = lane < c
        if seq_len < c:
            sh = seq_len.bit_length() - 1
            same = (self.r >> sh) == (self.cc >> sh)
        else:
            same = self.r >= 0
        self.causal = (self.r >= self.cc) & same
        self.strict_f = jnp.where((self.r > self.cc) & same, 1.0, 0.0)
        self.eye = jnp.where(self.r == self.cc, 1.0, 0.0)
        self.lo_b = jnp.where(self.lo, 1.0, 0.0).astype(BF16)
        self.hi_b = jnp.where(self.lo, 0.0, 1.0).astype(BF16)

    def blockdiag(self, sp):
        return tuple(jnp.concatenate([x * self.lo_b, x * self.hi_b], axis=0) for x in sp)


def _unit_lower_inverse_pairs(l_list, pm, seq_len):
    def bd_all(sps):
        return [pm.blockdiag(sp) for sp in sps]

    blk8 = jnp.where((pm.r >> 3) == (pm.cc >> 3), 1.0, 0.0)
    n0 = [l * blk8 for l in l_list]
    n0s = _split_all(n0)
    n2 = _mm3_all(n0s, bd_all(n0s))
    n2s = _split_all(n2)
    n4 = _mm3_all(n2s, bd_all(n2s))
    p = _mm3_all(_split_all([pm.eye - a for a in n0]), bd_all(_split_all([pm.eye + a for a in n2])))
    d = _mm3_all(_split_all(p), bd_all(_split_all([pm.eye + a for a in n4])))
    ds = _split_all(d)
    s = 8
    while s < seq_len:
        sh = s.bit_length() - 1
        off = jnp.where(((pm.r >> (sh + 1)) == (pm.cc >> (sh + 1))) & ((pm.r >> sh) != (pm.cc >> sh)), 1.0, 0.0)
        de = _mm3_all(ds, bd_all(_split_all([l * off for l in l_list])))
        ded = _mm3_all(_split_all(de), bd_all(ds))
        d = [a - b for a, b in zip(d, ded)]
        ds = _split_all(d)
        s *= 2
    return ds


def _in_proj_kernel(x_ref, g_ref, wt_ref, proj_ref, gt_ref):
    hb = _rms(x_ref[...], g_ref[...]).astype(BF16)
    proj_ref[...] = _dot_nt(hb, wt_ref[...])
    gt = proj_ref[:, N_PROJ - LANES:N_PROJ].T
    for j in range(gt_ref.shape[0]):
        gt_ref[j] = gt[0:N_GATE, CHUNK * j:CHUNK * (j + 1)]


def _layer_spec(layer, *shape):
    zeros = (0,) * len(shape)
    return pl.BlockSpec((None,) + shape, lambda *_: (layer,) + zeros)


def _in_proj(x, g, w, layer):
    rows = x.shape[0]
    tm = ROW_TILE
    return pl.pallas_call(
        _in_proj_kernel,
        grid=(rows // tm,),
        in_specs=[
            pl.BlockSpec((tm, D_MODEL), lambda i: (i, 0)),
            _layer_spec(layer, 1, D_MODEL),
            _layer_spec(layer, N_PROJ, D_MODEL),
        ],
        out_specs=[
            pl.BlockSpec((tm, N_PROJ), lambda i: (i, 0)),
            pl.BlockSpec((tm // CHUNK, N_GATE, CHUNK), lambda i: (i, 0, 0)),
        ],
        out_shape=[jax.ShapeDtypeStruct((rows, N_PROJ), F32),
                   jax.ShapeDtypeStruct((rows // CHUNK, N_GATE, CHUNK), F32)],
        compiler_params=_cparams(1),
        name="in_proj",
    )(x, g, w)


def _chunk_masks(seq_len):
    c = CHUNK
    ri = lax.broadcasted_iota(jnp.int32, (c, c), 0)
    ci = lax.broadcasted_iota(jnp.int32, (c, c), 1)
    if seq_len < c:
        sh = seq_len.bit_length() - 1
        same = (ri >> sh) == (ci >> sh)
    else:
        same = ri >= 0
    return ri, ci, same


def _gdn_prep_kernel(*refs, tb, seq_len, valid_lo, idt):
    has_cprev = valid_lo > 0
    if has_cprev:
        (u_ref, prev_ref, gcol_ref, grow_ref, cprev_ref, cw_ref, pa_ref, pl_ref, ra_ref, rl_ref,
         wkqg_ref, wv_ref, qk_ref, kdt_ref, el_ref) = refs
    else:
        (u_ref, prev_ref, gcol_ref, grow_ref, cw_ref, pa_ref, pl_ref, ra_ref, rl_ref,
         wkqg_ref, wv_ref, qk_ref, kdt_ref, el_ref) = refs
    c = CHUNK
    sh = seq_len.bit_length() - 1
    u = u_ref[0]
    rowpos = lax.broadcasted_iota(jnp.int32, (tb, 1), 0) & (seq_len - 1)
    if has_cprev:
        u = jnp.where((rowpos >= valid_lo - (CONV_W - 1)) & (rowpos < valid_lo), cprev_ref[0], u)
    xp = jnp.concatenate([jnp.where(pl.program_id(1) == 0, 0.0, prev_ref[0]), u], axis=0)
    cw = cw_ref[...]
    y = None
    for i in range(CONV_W):
        s = CONV_W - 1 - i
        xs = u if s == 0 else pltpu.roll(xp, s, 0)[8:8 + tb]
        t = xs * cw[i:i + 1, :]
        y = t if y is None else y + t
    y = jax.nn.silu(y)

    lane = lax.broadcasted_iota(jnp.int32, (1, LANES), 1)
    xg = gcol_ref[0] + pa_ref[0:1, :]
    gcolv = jnp.where(lane < N_HEADS, jax.nn.sigmoid(xg), -jnp.exp(pl_ref[0:1, :]) * _softplus(xg))
    if valid_lo > 0:
        gcolv = jnp.where(rowpos >= valid_lo, gcolv, 0.0)

    ri, ci, same = _chunk_masks(seq_len)
    causal = (ri >= ci) & same
    upper = (ri <= ci) & same
    colpos = lax.broadcasted_iota(jnp.int32, (1, c), 1) & (seq_len - 1)
    r8 = lax.broadcasted_iota(jnp.int32, (8, c), 0)
    c8 = lax.broadcasted_iota(jnp.int32, (8, c), 1)

    items = []
    for k in range(tb // c):
        r0 = k * c
        growv = -jnp.exp(rl_ref[:, 0:1]) * _softplus(grow_ref[k] + ra_ref[:, 0:1])
        if valid_lo > 0:
            growv = jnp.where(colpos >= valid_lo, growv, 0.0)
        for h in range(N_HEADS):
            qc = y[r0:r0 + c, DH * h:DH * (h + 1)]
            kc = y[r0:r0 + c, N_HEADS * DH + DH * h:N_HEADS * DH + DH * (h + 1)]
            g_col = gcolv[r0:r0 + c, N_HEADS + h:N_HEADS + h + 1]
            g_row = growv[N_HEADS + h:N_HEADS + h + 1, :]
            items.append(dict(
                k=k, h=h, hs=slice(DH * h, DH * (h + 1)), qc=qc, kc=kc,
                vc=y[r0:r0 + c, 2 * N_HEADS * DH + DH * h:2 * N_HEADS * DH + DH * (h + 1)],
                beta=gcolv[r0:r0 + c, h:h + 1],
                qss=jnp.sum(qc * qc, axis=-1, keepdims=True),
                kss=jnp.sum(kc * kc, axis=-1, keepdims=True),
                gc_col=jnp.sum(jnp.where(causal, g_row, 0.0), axis=1, keepdims=True),
                gc_row=jnp.sum(jnp.where(upper, g_col, 0.0), axis=0, keepdims=True),
                gl_col=jnp.sum(jnp.where(same, g_row, 0.0), axis=1, keepdims=True),
                gl8=jnp.sum(jnp.where((c8 >> sh) == r8, g_row, 0.0), axis=1, keepdims=True)))
    for it in items:
        k, hs, gc_col, beta = it["k"], it["hs"], it["gc_col"], it["beta"]
        qn = it["qc"] * lax.rsqrt(it["qss"] + EPS) * (DH ** -0.5)
        kn = it["kc"] * lax.rsqrt(it["kss"] + EPS)
        wkqg_ref[0, k, c:2 * c, hs] = (qn * jnp.exp(gc_col)).astype(idt)
        el_ref[0, k, :, hs] = jnp.broadcast_to(jnp.exp(it["gl8"]), (8, DH))
        it["kd"] = kn * jnp.exp(it["gl_col"] - gc_col)
        it["qb"] = qn.astype(BF16)
        it["kb"] = kn.astype(BF16)
        it["rhs"] = jnp.concatenate([beta * it["vc"], (beta * jnp.exp(gc_col)) * kn], axis=1)
    for a, b in zip(items[0::2], items[1::2]):
        kdt_ref[0, a["k"], a["h"] // 2] = jnp.concatenate([a["kd"], b["kd"]], axis=0).T.astype(idt)
    pm = _PairMasks(seq_len)
    zb = jnp.zeros((c, DH), BF16)
    pairs = []
    for a, b in zip(items[0::2], items[1::2]):
        gcp = jnp.where(pm.lo, a["gc_col"], b["gc_col"])
        grp = jnp.concatenate([a["gc_row"], b["gc_row"]], axis=1)
        decay = jnp.where(pm.causal, jnp.exp(jnp.where(pm.causal, gcp - grp, 0.0)), 0.0)
        kbd = jnp.concatenate([jnp.concatenate([a["kb"], zb], axis=1),
                               jnp.concatenate([zb, b["kb"]], axis=1)], axis=0)
        kq = jnp.concatenate([jnp.concatenate([a["kb"], b["kb"]], axis=1),
                              jnp.concatenate([a["qb"], b["qb"]], axis=1)], axis=0)
        pairs.append(dict(a=a, b=b, decay=decay, kbd=kbd, kq=kq, beta=jnp.where(pm.lo, a["beta"], b["beta"])))
    kqs = [_dot_nt(p["kq"], p["kbd"]) for p in pairs]
    for p, kq in zip(pairs, kqs):
        qk_ref[0, p["a"]["k"], p["a"]["h"] // 2] = (kq[c:2 * c] * p["decay"]).astype(idt)
    ls = [p["beta"] * kq[0:c] * p["decay"] * pm.strict_f for p, kq in zip(pairs, kqs)]
    tinvs = _unit_lower_inverse_pairs(ls, pm, seq_len)
    for p, (th, tl) in zip(pairs, tinvs):
        rh, rl_ = _split(jnp.concatenate([p["a"]["rhs"], p["b"]["rhs"]], axis=0))
        ta_h, tb_h = th * pm.lo_b, th * pm.hi_b
        r1 = _dot(jnp.concatenate([ta_h, tl * pm.lo_b, tb_h, tl * pm.hi_b], axis=0), rh)
        r2 = _dot(jnp.concatenate([ta_h, tb_h], axis=0), rl_)
        p["sol"] = (r1[0:c] + (r2[0:c] + r1[c:2 * c]), r1[2 * c:3 * c] + (r2[c:2 * c] + r1[3 * c:4 * c]))
    for p in pairs:
        for it, sol in zip((p["a"], p["b"]), p["sol"]):
            wv_ref[0, it["k"], :, it["hs"]] = sol[:, 0:DH]
            wkqg_ref[0, it["k"], 0:c, it["hs"]] = sol[:, DH:2 * DH].astype(idt)


def _gdn_prep(proj3, grow, cprev, cw, pa, pl_, ra, rl, layer, *, seq_len, valid_lo, idt, tb):
    nb, t, _ = proj3.shape
    c = CHUNK
    nt = t // tb
    kb = tb // c
    has_cprev = valid_lo > 0
    in_specs = [
        pl.BlockSpec((1, tb, D_CONV), lambda b, i: (b, i, 0)),
        pl.BlockSpec((1, 8, D_CONV), lambda b, i: (b, jnp.maximum(i * (tb // 8) - 1, 0), 0)),
        pl.BlockSpec((1, tb, LANES), lambda b, i: (b, i, GATE_COL_BLOCK)),
        pl.BlockSpec((kb, N_GATE, c), lambda b, i: (b * nt + i, 0, 0)),
    ]
    args = [proj3, proj3, proj3, grow]
    if has_cprev:
        assert nb == 1
        in_specs.append(pl.BlockSpec((1, tb, D_CONV), lambda b, i: (layer, i, 0)))
        args.append(cprev)
    in_specs += [
        _layer_spec(layer, CONV_W, D_CONV),
        _layer_spec(layer, 8, LANES),
        _layer_spec(layer, 8, LANES),
        _layer_spec(layer, N_GATE, LANES),
        _layer_spec(layer, N_GATE, LANES),
    ]
    args += [cw, pa, pl_, ra, rl]
    n = t // c
    hd = N_HEADS * DH
    out_shape = [
        jax.ShapeDtypeStruct((nb, n, 2 * c, hd), idt),
        jax.ShapeDtypeStruct((nb, n, c, hd), F32),
        jax.ShapeDtypeStruct((nb, n, N_HEADS // 2, c, 2 * c), idt),
        jax.ShapeDtypeStruct((nb, n, N_HEADS // 2, DH, 2 * c), idt),
        jax.ShapeDtypeStruct((nb, n, 8, hd), F32),
    ]
    out_specs = [
        pl.BlockSpec((1, kb, 2 * c, hd), lambda b, i: (b, i, 0, 0)),
        pl.BlockSpec((1, kb, c, hd), lambda b, i: (b, i, 0, 0)),
        pl.BlockSpec((1, kb, N_HEADS // 2, c, 2 * c), lambda b, i: (b, i, 0, 0, 0)),
        pl.BlockSpec((1, kb, N_HEADS // 2, DH, 2 * c), lambda b, i: (b, i, 0, 0, 0)),
        pl.BlockSpec((1, kb, 8, hd), lambda b, i: (b, i, 0, 0)),
    ]
    return pl.pallas_call(
        functools.partial(_gdn_prep_kernel, tb=tb, seq_len=seq_len, valid_lo=valid_lo, idt=idt),
        grid=(nb, nt),
        in_specs=in_specs,
        out_specs=out_specs,
        out_shape=out_shape,
        compiler_params=_cparams(2),
        name="gdn_prep",
    )(*args)


def _lane_half_masks(dtype):
    lo = lax.broadcasted_iota(jnp.int32, (CHUNK, 2 * CHUNK), 1) < CHUNK
    return jnp.where(lo, 1.0, 0.0).astype(dtype), jnp.where(lo, 0.0, 1.0).astype(dtype)


def _gated_norm_store(o, z, gn, mix_ref, idx):
    mix_ref[idx] = (_rms(o, gn) * jax.nn.silu(z)).astype(BF16)


def _gdn_scan_prompt_kernel(wkqg_ref, wv_ref, qk_ref, kdt_ref, el_ref, z_ref, gn_ref,
                            mix_ref, sout_ref, s_scr, *, nc, bs):
    c = CHUNK

    @pl.when(pl.program_id(1) == 0)
    def _():
        s_scr[...] = jnp.zeros_like(s_scr)

    gn = gn_ref[...]
    units = [(b, h, slice(DH * h, DH * (h + 1))) for b in range(bs) for h in range(N_HEADS)]
    half_b = _lane_half_masks(BF16)
    half2_b = tuple(jnp.concatenate([m, m], axis=0) for m in half_b)

    def body(n, carry):
        rows = pl.ds(pl.multiple_of(n * c, c), c)
        ss = [s_scr[b * N_HEADS + h] for b, h, _ in units]
        aa = [_dot(wkqg_ref[b, n, :, hs], s.astype(BF16)) for (b, _, hs), s in zip(units, ss)]
        ubs = [(wv_ref[b, n, :, hs] - a[0:c]).astype(BF16) for (b, _, hs), a in zip(units, aa)]
        os_, ds = [], []
        for j in range(len(units) // 2):
            b, h, _ = units[2 * j]
            qkp, kdp = qk_ref[b, n, h // 2], kdt_ref[b, n, h // 2]
            r = _dot(jnp.concatenate([qkp * half_b[0], qkp * half_b[1], kdp * half2_b[0], kdp * half2_b[1]],
                                     axis=0),
                     jnp.concatenate(ubs[2 * j:2 * j + 2], axis=0))
            os_ += [r[0:c], r[c:2 * c]]
            ds += [r[2 * c:2 * c + DH], r[2 * c + DH:2 * c + 2 * DH]]
        for i, (b, h, hs) in enumerate(units):
            s_scr[b * N_HEADS + h] = el_ref[b, n, 0:1, hs] * ss[i] + ds[i]
            _gated_norm_store(aa[i][c:2 * c] + os_[i], z_ref[b, rows, hs], gn, mix_ref, (b, rows, hs))
        return carry

    lax.fori_loop(0, nc, body, 0)

    @pl.when(pl.program_id(1) == pl.num_programs(1) - 1)
    def _():
        for b in range(bs):
            sout_ref[b] = s_scr[b * N_HEADS:(b + 1) * N_HEADS]


def _gdn_scan_prompt(wkqg, wv, qk, kdt, el, proj3, gn, layer, *, ts, bs):
    nb, n, _, hd = wkqg.shape
    c = CHUNK
    t = n * c
    nc = ts // c
    zblk = D_CONV // hd
    return pl.pallas_call(
        functools.partial(_gdn_scan_prompt_kernel, nc=nc, bs=bs),
        grid=(nb // bs, t // ts),
        in_specs=[
            pl.BlockSpec((bs, nc, 2 * c, hd), lambda b, i: (b, i, 0, 0)),
            pl.BlockSpec((bs, nc, c, hd), lambda b, i: (b, i, 0, 0)),
            pl.BlockSpec((bs, nc, N_HEADS // 2, c, 2 * c), lambda b, i: (b, i, 0, 0, 0)),
            pl.BlockSpec((bs, nc, N_HEADS // 2, DH, 2 * c), lambda b, i: (b, i, 0, 0, 0)),
            pl.BlockSpec((bs, nc, 8, hd), lambda b, i: (b, i, 0, 0)),
            pl.BlockSpec((bs, ts, hd), lambda b, i: (b, i, zblk)),
            _layer_spec(layer, 1, DH),
        ],
        out_specs=[
            pl.BlockSpec((bs, ts, hd), lambda b, i: (b, i, 0)),
            pl.BlockSpec((bs, N_HEADS, DH, DH), lambda b, i: (b, 0, 0, 0)),
        ],
        out_shape=[jax.ShapeDtypeStruct((nb, t, hd), BF16),
                   jax.ShapeDtypeStruct((nb, N_HEADS, DH, DH), F32)],
        scratch_shapes=[pltpu.VMEM((bs * N_HEADS, DH, DH), F32)],
        compiler_params=_cparams(2),
        name="gdn_scan_prompt",
    )(wkqg, wv, qk, kdt, el, proj3, gn)


def _gdn_scan_decode_kernel(*refs, n_prev):
    (wkqg_ref, wv_ref, qk_ref, kdt_ref, el_ref, z_ref, gn_ref, s0_ref), rest = refs[:8], refs[8:]
    prev_refs, (mix_ref, sout_ref) = rest[:n_prev], rest[n_prev:]
    if n_prev:
        for l, p_ref in enumerate(prev_refs):
            sout_ref[l] = p_ref[...]
        sout_ref = sout_ref.at[n_prev]
    c = CHUNK
    nseq = c // SLOT
    rowseq = lax.broadcasted_iota(jnp.int32, (c, 1), 0) >> 3
    gn = gn_ref[...]
    heads = [slice(DH * h, DH * (h + 1)) for h in range(N_HEADS)]
    aa = []
    for h, hs in enumerate(heads):
        w = wkqg_ref[0, 0, :, hs]
        row = []
        for j in range(nseq):
            wj = jnp.concatenate([w[SLOT * j:SLOT * (j + 1)], w[c + SLOT * j:c + SLOT * (j + 1)]],
                                 axis=0).astype(BF16)
            row.append(_dot(wj, s0_ref[j, h].astype(BF16)))
        aa.append(row)
    us = [wv_ref[0, 0, :, hs] - jnp.concatenate([a[0:SLOT] for a in aa[h]], axis=0)
          for h, hs in enumerate(heads)]
    half_f = _lane_half_masks(F32)
    os_ = []
    for j in range(N_HEADS // 2):
        qkp = qk_ref[0, 0, j]
        o2 = _dot(jnp.concatenate([qkp * half_f[0], qkp * half_f[1]], axis=0).astype(BF16),
                  jnp.concatenate(us[2 * j:2 * j + 2], axis=0).astype(BF16))
        os_ += [o2[0:c], o2[c:2 * c]]
    half2_f = tuple(jnp.concatenate([m, m], axis=0) for m in half_f)
    for p in range(N_HEADS // 2):
        kdp = kdt_ref[0, 0, p]
        kd2 = jnp.concatenate([kdp * half2_f[0], kdp * half2_f[1]], axis=0).astype(BF16)
        for j in range(nseq):
            uj = jnp.concatenate([jnp.where(rowseq == j, us[2 * p + i], 0.0) for i in range(2)], axis=0)
            r = _dot(kd2, uj.astype(BF16))
            for i in range(2):
                h = 2 * p + i
                sout_ref[j, h] = (el_ref[0, 0, j:j + 1, heads[h]] * s0_ref[j, h] + r[DH * i:DH * (i + 1)])
    for h, hs in enumerate(heads):
        o = jnp.concatenate([a[SLOT:2 * SLOT] for a in aa[h]], axis=0) + os_[h]
        _gated_norm_store(o, z_ref[:, hs], gn, mix_ref, (slice(None), hs))


def _state_specs(layer, prev, state_shape, nseq):
    tail = state_shape[2:]
    zeros = (0,) * len(tail)
    in_specs = [pl.BlockSpec((None, nseq) + tail, lambda i: (layer, i) + zeros)]
    in_specs += [pl.BlockSpec((nseq,) + tail, lambda i: (i,) + zeros) for _ in prev]
    if prev:
        out_spec = pl.BlockSpec((len(prev) + 1, nseq) + tail, lambda i: (0, i) + zeros)
        out_shape = jax.ShapeDtypeStruct((len(prev) + 1,) + state_shape[1:], F32)
    else:
        out_spec = pl.BlockSpec((nseq,) + tail, lambda i: (i,) + zeros)
        out_shape = jax.ShapeDtypeStruct(state_shape[1:], F32)
    return in_specs, out_spec, out_shape


def _gdn_scan_decode(wkqg, wv, qk, kdt, el, proj, gn, s0_all, prev, layer):
    _, n, _, hd = wkqg.shape
    c = CHUNK
    nseq = c // SLOT
    zblk = D_CONV // hd
    st_in, st_out, st_shape = _state_specs(layer, prev, s0_all.shape, nseq)
    return pl.pallas_call(
        functools.partial(_gdn_scan_decode_kernel, n_prev=len(prev)),
        grid=(n,),
        in_specs=[
            pl.BlockSpec((1, 1, 2 * c, hd), lambda i: (0, i, 0, 0)),
            pl.BlockSpec((1, 1, c, hd), lambda i: (0, i, 0, 0)),
            pl.BlockSpec((1, 1, N_HEADS // 2, c, 2 * c), lambda i: (0, i, 0, 0, 0)),
            pl.BlockSpec((1, 1, N_HEADS // 2, DH, 2 * c), lambda i: (0, i, 0, 0, 0)),
            pl.BlockSpec((1, 1, 8, hd), lambda i: (0, i, 0, 0)),
            pl.BlockSpec((c, hd), lambda i: (i, zblk)),
            _layer_spec(layer, 1, DH),
        ] + st_in,
        out_specs=[pl.BlockSpec((c, hd), lambda i: (i, 0)), st_out],
        out_shape=[jax.ShapeDtypeStruct((n * c, hd), BF16), st_shape],
        compiler_params=_cparams(1),
        name="gdn_scan_decode",
    )(wkqg, wv, qk, kdt, el, proj, gn, s0_all, *prev)


def _mlstm_gates(gcol, growk, pa_ref, ra_ref, h, rows, valid_lo, rowpos, colpos):
    xg = gcol + pa_ref[0:1, :]
    ig_col = xg[:, 2 * N_HEADS + h:2 * N_HEADS + h + 1]
    fl_col = -_softplus(-xg[:, 3 * N_HEADS + h:3 * N_HEADS + h + 1])
    xr = growk + ra_ref[:, 0:1]
    ig_row = xr[2 * N_HEADS + h:2 * N_HEADS + h + 1, :]
    fl_row = -_softplus(-xr[3 * N_HEADS + h:3 * N_HEADS + h + 1, :])
    if valid_lo > 0:
        ig_col = jnp.where(rowpos >= valid_lo, ig_col, NEG)
        fl_col = jnp.where(rowpos >= valid_lo, fl_col, 0.0)
        ig_row = jnp.where(colpos >= valid_lo, ig_row, NEG)
        fl_row = jnp.where(colpos >= valid_lo, fl_row, 0.0)
    return ig_col, fl_col, ig_row, fl_row


def _mlstm_chunks(probs, seq_len):
    c = CHUNK
    nseq = c // seq_len
    ri, ci, same = _chunk_masks(seq_len)
    causal = (ri >= ci) & same
    upper = (ri <= ci) & same
    seq_end = ci == (ri | (seq_len - 1))
    rowseq = lax.broadcasted_iota(jnp.int32, (c, 1), 0) >> (seq_len.bit_length() - 1)
    zpad = jnp.zeros((16 - seq_len, DH), F32) if nseq > 1 else None
    for p in probs:
        p["f_col"] = jnp.sum(jnp.where(causal, p["fl_row"], 0.0), axis=1, keepdims=True)
        p["f_row"] = jnp.sum(jnp.where(upper, p["fl_col"], 0.0), axis=0, keepdims=True)
        p["km"] = p["k"] * (DH ** -0.5)
        p["qb"] = p["q"].astype(BF16)
        p["kb"] = p["km"].astype(BF16)
        p["vb"] = p["v"].astype(BF16)
        if nseq == 1:
            p["qn"] = _dot_nt(p["qb"], p["n_rows"].astype(BF16))[:, 0:1]
        else:
            p["qn"] = jnp.sum(p["q"] * p["n_rows"], axis=1, keepdims=True)
    for p in probs:
        p["b_col"] = p["ig_col"] - p["f_col"]
        p["b_row"] = p["ig_row"] - p["f_row"]
        p["bmax_col"] = jnp.max(jnp.where(causal, p["b_row"], -jnp.inf), axis=1, keepdims=True)
        if nseq > 1:
            p["fl_end"] = jnp.sum(jnp.where(seq_end, p["f_row"], 0.0), axis=1, keepdims=True)
            p["bmax_end"] = jnp.max(jnp.where(same, p["b_row"], -jnp.inf), axis=1, keepdims=True)
    for p in probs:
        f_col, mp_col = p["f_col"], p["mp_col"]
        m_col = f_col + jnp.maximum(mp_col, p["bmax_col"])
        p["m_col"] = m_col
        p["a_col"] = jnp.exp(f_col + mp_col - m_col)
        p["dexp"] = jnp.where(causal, jnp.exp(jnp.where(causal, (f_col - m_col) + p["b_row"], 0.0)), 0.0)
        if nseq == 1:
            fl_end, ml, mp = f_col[c - 1:c], m_col[c - 1:c], mp_col[c - 1:c]
            p["ml_col"] = jnp.broadcast_to(ml, (c, 1))
            p["al"] = jnp.broadcast_to(jnp.exp(fl_end + mp - ml), (1, DH))
        else:
            fl_end, mp = p["fl_end"], mp_col
            ml = fl_end + jnp.maximum(mp_col, p["bmax_end"])
            p["ml_col"] = ml
            p["al"] = jnp.broadcast_to(jnp.exp(fl_end + mp - ml), (c, DH))
        p["kw"] = p["km"] * jnp.exp(fl_end + p["b_col"] - ml)
    for p in probs:
        p["kwt"] = p["kw"].T.astype(BF16)
    for p in probs:
        p["qk"] = _dot_nt(p["qb"], p["kb"])
    for p in probs:
        if nseq == 1:
            p["qc"] = _dot(p["qb"], p["c_list"][0].astype(BF16))
            p["upd"] = [_dot(p["kwt"], p["vb"])]
        else:
            parts, upd = [], []
            for j in range(nseq):
                qj = jnp.concatenate([p["q"][seq_len * j:seq_len * (j + 1)], zpad], axis=0).astype(BF16)
                parts.append(_dot(qj, p["c_list"][j].astype(BF16))[0:seq_len])
                upd.append(_dot(p["kwt"], jnp.where(rowseq == j, p["v"], 0.0).astype(BF16)))
            p["qc"] = jnp.concatenate(parts, axis=0)
            p["upd"] = upd
    if nseq == 1:
        ones_b = jnp.ones((c, LANES), BF16)
        for p in probs:
            p["w_hl"] = _split(p["dexp"] * p["qk"])
        for p in probs:
            r = _dot(jnp.concatenate(p["w_hl"], axis=0), jnp.concatenate([p["vb"], ones_b], axis=1))
            p["wv"] = r[0:c, 0:DH]
            p["wsum"] = r[0:c, DH:DH + 1] + r[c:2 * c, DH:DH + 1]
    else:
        for p in probs:
            p["w"] = p["dexp"] * p["qk"]
            p["wsum"] = jnp.sum(p["w"], axis=1, keepdims=True)
        for p in probs:
            p["wv"] = _dot(p["w"].astype(BF16), p["vb"])
    for p in probs:
        num = p["a_col"] * p["qc"] + p["wv"]
        den = p["a_col"] * p["qn"] + p["wsum"]
        p["hout"] = num / jnp.maximum(jnp.abs(den), jnp.exp(-p["m_col"]))
        p["new_c"] = [p["al"][seq_len * j:seq_len * j + 1] * p["c_list"][j] + p["upd"][j]
                      for j in range(nseq)]
    return probs


def _mlstm_out_store(hout, og, gnorm, mix_ref, idx):
    x = hout * jax.nn.sigmoid(og)
    ssq = _dot(jnp.concatenate(_split(x * x), axis=0), jnp.ones((DH, DH), BF16))
    ms = (ssq[0:CHUNK] + ssq[CHUNK:2 * CHUNK]) * (1.0 / DH)
    mix_ref[idx] = (x * lax.rsqrt(ms + EPS) * gnorm).astype(BF16)


def _mlstm_prompt_kernel(q_ref, k_ref, v_ref, o_ref, gcol_ref, grow_ref, pa_ref, ra_ref, nrm_ref,
                         mix_ref, cout_ref, nout_ref, mout_ref, c_scr, n_scr, m_scr, *, nc, bs):
    c = CHUNK

    @pl.when(pl.program_id(1) == 0)
    def _():
        c_scr[...] = jnp.zeros_like(c_scr)
        n_scr[...] = jnp.zeros_like(n_scr)
        m_scr[...] = jnp.zeros_like(m_scr)

    def body(n, carry):
        r0 = pl.multiple_of(n * c, c)
        rows = pl.ds(r0, c)
        probs = []
        for b in range(bs):
            gcol = gcol_ref[b, rows, :]
            growk = grow_ref[b, n]
            for h in range(N_HEADS):
                hs = slice(DH * h, DH * (h + 1))
                ig_col, fl_col, ig_row, fl_row = _mlstm_gates(gcol, growk, pa_ref, ra_ref, h, c, 0, None, None)
                st = b * N_HEADS + h
                probs.append(dict(b=b, hs=hs, st=st, q=q_ref[b, rows, hs], k=k_ref[b, rows, hs],
                                  v=v_ref[b, rows, hs], ig_col=ig_col, fl_col=fl_col, ig_row=ig_row,
                                  fl_row=fl_row, mp_col=m_scr[st][:, 0:1], n_rows=n_scr[st],
                                  c_list=[c_scr[st]]))
        for p in _mlstm_chunks(probs, c):
            st = p["st"]
            c_scr[st] = p["new_c"][0]
            n_scr[st] = jnp.broadcast_to(
                p["al"][0:1] * p["n_rows"][0:1] + jnp.sum(p["kw"], axis=0, keepdims=True), (c, DH))
            m_scr[st] = jnp.broadcast_to(p["ml_col"], (c, DH))
            _mlstm_out_store(p["hout"], o_ref[p["b"], rows, p["hs"]], nrm_ref[:, p["hs"]], mix_ref,
                             (p["b"], rows, p["hs"]))
        return carry

    lax.fori_loop(0, nc, body, 0)

    @pl.when(pl.program_id(1) == pl.num_programs(1) - 1)
    def _():
        lane = lax.broadcasted_iota(jnp.int32, (8, DH), 1)
        for b in range(bs):
            mo = jnp.zeros((8, DH), F32)
            for h in range(N_HEADS):
                st = b * N_HEADS + h
                cout_ref[b, h] = c_scr[st]
                nout_ref[b, h:h + 1, :] = n_scr[st][0:1]
                mo = jnp.where(lane == h, m_scr[st][0:8], mo)
            mout_ref[b] = mo


def _mlstm_prompt(proj3, grow4, pa, ra, nrm, layer, *, ts, bs):
    nb, t, _ = proj3.shape
    c = CHUNK
    hd = N_HEADS * DH
    nc = ts // c
    qblk = (D_CONV + hd) // hd
    return pl.pallas_call(
        functools.partial(_mlstm_prompt_kernel, nc=nc, bs=bs),
        grid=(nb // bs, t // ts),
        in_specs=[
            pl.BlockSpec((bs, ts, hd), lambda b, i: (b, i, qblk)),
            pl.BlockSpec((bs, ts, hd), lambda b, i: (b, i, qblk + 1)),
            pl.BlockSpec((bs, ts, hd), lambda b, i: (b, i, qblk + 2)),
            pl.BlockSpec((bs, ts, hd), lambda b, i: (b, i, qblk + 3)),
            pl.BlockSpec((bs, ts, LANES), lambda b, i: (b, i, GATE_COL_BLOCK)),
            pl.BlockSpec((bs, nc, N_GATE, c), lambda b, i: (b, i, 0, 0)),
            _layer_spec(layer, 8, LANES),
            _layer_spec(layer, N_GATE, LANES),
            _layer_spec(layer, 1, hd),
        ],
        out_specs=[
            pl.BlockSpec((bs, ts, hd), lambda b, i: (b, i, 0)),
            pl.BlockSpec((bs, N_HEADS, DH, DH), lambda b, i: (b, 0, 0, 0)),
            pl.BlockSpec((bs, N_HEADS, DH), lambda b, i: (b, 0, 0)),
            pl.BlockSpec((bs, 8, DH), lambda b, i: (b, 0, 0)),
        ],
        out_shape=[jax.ShapeDtypeStruct((nb, t, hd), BF16),
                   jax.ShapeDtypeStruct((nb, N_HEADS, DH, DH), F32),
                   jax.ShapeDtypeStruct((nb, N_HEADS, DH), F32),
                   jax.ShapeDtypeStruct((nb, 8, DH), F32)],
        scratch_shapes=[pltpu.VMEM((bs * N_HEADS, DH, DH), F32),
                        pltpu.VMEM((bs * N_HEADS, c, DH), F32),
                        pltpu.VMEM((bs * N_HEADS, c, DH), F32)],
        compiler_params=_cparams(2),
        name="mlstm_prompt",
    )(proj3, proj3, proj3, proj3, proj3, grow4, pa, ra, nrm)


def _mlstm_decode_kernel(*refs, n_prev):
    (q_ref, k_ref, v_ref, o_ref, gcol_ref, grow_ref, pa_ref, ra_ref, nrm_ref, n0_ref, m0_ref,
     c0_ref), rest = refs[:12], refs[12:]
    prev_refs, (mix_ref, nout_ref, mout_ref, cout_ref) = rest[:n_prev], rest[n_prev:]
    if n_prev:
        for l, p_ref in enumerate(prev_refs):
            cout_ref[l] = p_ref[...]
        cout_ref = cout_ref.at[n_prev]
    c = CHUNK
    nseq = c // SLOT
    rowpos = lax.broadcasted_iota(jnp.int32, (c, 1), 0) & (SLOT - 1)
    colpos = lax.broadcasted_iota(jnp.int32, (1, c), 1) & (SLOT - 1)
    lane = lax.broadcasted_iota(jnp.int32, (c, DH), 1)
    gcol = gcol_ref[...]
    growk = grow_ref[0]
    probs = []
    for h in range(N_HEADS):
        hs = slice(DH * h, DH * (h + 1))
        ig_col, fl_col, ig_row, fl_row = _mlstm_gates(gcol, growk, pa_ref, ra_ref, h, c, SLOT_PAD, rowpos, colpos)
        n_rows = jnp.concatenate(
            [jnp.broadcast_to(n0_ref[j, h:h + 1, :], (SLOT, DH)) for j in range(nseq)], axis=0)
        probs.append(dict(h=h, hs=hs, q=q_ref[:, hs], k=k_ref[:, hs], v=v_ref[:, hs], ig_col=ig_col,
                          fl_col=fl_col, ig_row=ig_row, fl_row=fl_row, mp_col=m0_ref[:, h:h + 1],
                          n_rows=n_rows, c_list=[c0_ref[j, h] for j in range(nseq)]))
    mo = jnp.zeros((c, DH), F32)
    for p in _mlstm_chunks(probs, SLOT):
        h, hs = p["h"], p["hs"]
        for j in range(nseq):
            cout_ref[j, h] = p["new_c"][j]
            rs = slice(SLOT * j, SLOT * (j + 1))
            nout_ref[j, h:h + 1, :] = (p["al"][SLOT * j:SLOT * j + 1] * n0_ref[j, h:h + 1, :]
                                       + jnp.sum(p["kw"][rs], axis=0, keepdims=True))
        mo = jnp.where(lane == h, jnp.broadcast_to(p["ml_col"], (c, DH)), mo)
        _mlstm_out_store(p["hout"], o_ref[:, hs], nrm_ref[:, hs], mix_ref, (slice(None), hs))
    mout_ref[...] = mo


def _mlstm_decode(proj, grow, pa, ra, nrm, c0_all, n0_all, m0rows_all, prev, layer):
    rows = proj.shape[0]
    c = CHUNK
    hd = N_HEADS * DH
    nseq = c // SLOT
    qblk = (D_CONV + hd) // hd
    st_in, st_out, st_shape = _state_specs(layer, prev, c0_all.shape, nseq)
    return pl.pallas_call(
        functools.partial(_mlstm_decode_kernel, n_prev=len(prev)),
        grid=(rows // c,),
        in_specs=[
            pl.BlockSpec((c, hd), lambda i: (i, qblk)),
            pl.BlockSpec((c, hd), lambda i: (i, qblk + 1)),
            pl.BlockSpec((c, hd), lambda i: (i, qblk + 2)),
            pl.BlockSpec((c, hd), lambda i: (i, qblk + 3)),
            pl.BlockSpec((c, LANES), lambda i: (i, GATE_COL_BLOCK)),
            pl.BlockSpec((1, N_GATE, c), lambda i: (i, 0, 0)),
            _layer_spec(layer, 8, LANES),
            _layer_spec(layer, N_GATE, LANES),
            _layer_spec(layer, 1, hd),
            pl.BlockSpec((None, nseq, N_HEADS, DH), lambda i: (layer, i, 0, 0)),
            pl.BlockSpec((None, c, LANES), lambda i: (layer, i, 0)),
        ] + st_in,
        out_specs=[
            pl.BlockSpec((c, hd), lambda i: (i, 0)),
            pl.BlockSpec((nseq, N_HEADS, DH), lambda i: (i, 0, 0)),
            pl.BlockSpec((c, LANES), lambda i: (i, 0)),
            st_out,
        ],
        out_shape=[jax.ShapeDtypeStruct((rows, hd), BF16),
                   jax.ShapeDtypeStruct(n0_all.shape[1:], F32),
                   jax.ShapeDtypeStruct((rows, LANES), F32),
                   st_shape],
        compiler_params=_cparams(1),
        name="mlstm_decode",
    )(proj, proj, proj, proj, proj, grow, pa, ra, nrm, n0_all, m0rows_all, c0_all, *prev)


def _post_kernel(x_ref, ma_ref, mb_ref, p_ref, woa_ref, wob_ref, gf_ref, wg_ref, wu_ref, wd_ref,
                 gp_ref, wpg_ref, wpp_ref, gfin_ref, o_ref, acc_ref, *, final):
    x = x_ref[...] + (_dot(ma_ref[...], woa_ref[...]) + _dot(mb_ref[...], wob_ref[...]))
    ub = _rms(x, gf_ref[...]).astype(BF16)
    for j in range(D_FF // FF_CHUNK):
        sl = slice(j * FF_CHUNK, (j + 1) * FF_CHUNK)
        a = (jax.nn.silu(_dot(ub, wg_ref[:, sl])) * _dot(ub, wu_ref[:, sl])).astype(BF16)
        d = _dot(a, wd_ref[sl, :])
        if j == 0:
            acc_ref[...] = d
        else:
            acc_ref[...] += d
    x = x + acc_ref[...]
    gate = jax.nn.sigmoid(_dot(_rms(x, gp_ref[...]).astype(BF16), wpg_ref[...]))
    x = x + _dot(p_ref[...].astype(BF16), wpp_ref[...]) * gate
    if final:
        x = _rms(x, gfin_ref[...])
    o_ref[...] = x


def _post(x, ma, mb, p_all, wo, gf, wg, wu, wd, gp, wpg, wpp, gfin, layer, *, final):
    rows = x.shape[0]
    tm = ROW_TILE
    hd = N_HEADS * DH
    row = lambda w: pl.BlockSpec((tm, w), lambda i: (i, 0))

    def whole(a, b, blk=0):
        return pl.BlockSpec((None, a, b), lambda i: (layer, blk, 0), pipeline_mode=pl.Buffered(1))

    return pl.pallas_call(
        functools.partial(_post_kernel, final=final),
        grid=(rows // tm,),
        in_specs=[row(D_MODEL), row(hd), row(hd),
                  pl.BlockSpec((None, tm, D_PLE), lambda i: (layer, i, 0)),
                  whole(hd, D_MODEL, 0), whole(hd, D_MODEL, 1), whole(1, D_MODEL),
                  whole(D_MODEL, D_FF), whole(D_MODEL, D_FF), whole(D_FF, D_MODEL),
                  whole(1, D_MODEL), whole(D_MODEL, D_MODEL), whole(D_PLE, D_MODEL),
                  pl.BlockSpec((1, D_MODEL), lambda i: (0, 0))],
        out_specs=row(D_MODEL),
        out_shape=jax.ShapeDtypeStruct((rows, D_MODEL), F32),
        scratch_shapes=[pltpu.VMEM((tm, D_MODEL), F32)],
        compiler_params=_cparams(1),
        name="post",
    )(x, ma, mb, p_all, wo, wo, gf, wg, wu, wd, gp, wpg, wpp, gfin)


def _gate_vec(pairs, depth):
    v = jnp.zeros((depth, N_GATE), F32)
    for off, val in pairs:
        v = v.at[:, off:off + N_HEADS].set(val.astype(F32))
    return v


def _lane_form(v):
    d = v.shape[0]
    return jnp.broadcast_to(jnp.pad(v, ((0, 0), (0, LANES - N_GATE)))[:, None, :], (d, 8, LANES))


def _row_form(v):
    d = v.shape[0]
    return jnp.broadcast_to(v[:, :, None], (d, N_GATE, LANES))


def kernel(x_prompt, x_sample, p_prompt, p_sample, state_gdn, state_gdn_conv, state_mlstm_C, state_mlstm_n, state_mlstm_m, w_in, conv_w, gdn_a_log, gdn_dt_bias, gdn_norm, mlstm_i_bias, mlstm_f_bias, mlstm_norm, w_out, norm_mix, norm_ffn, w_gate, w_up, w_down, norm_ple, w_ple_gate, w_ple_proj, norm_final):
    depth = w_in.shape[0]
    nb, t, _ = x_prompt.shape
    ns, tdec, _ = x_sample.shape
    hd = N_HEADS * DH
    c = CHUNK
    pad = SLOT - tdec

    xp = x_prompt.reshape(nb * t, D_MODEL)
    xs = jnp.pad(x_sample, ((0, 0), (pad, 0), (0, 0))).reshape(ns * SLOT, D_MODEL)
    ps_all = jnp.pad(p_sample, ((0, 0), (0, 0), (pad, 0), (0, 0))).reshape(depth, ns * SLOT, D_PLE)
    pp_all = p_prompt.reshape(depth, nb * t, D_PLE)
    gfin = norm_final.reshape(1, D_MODEL)

    o = D_CONV + hd
    wt = jnp.swapaxes(w_in, 1, 2).astype(BF16)
    w_r = jnp.concatenate([wt[:, :o], wt[:, o + 2 * N_HEADS:o + 2 * N_HEADS + 4 * hd], wt[:, o:o + 2 * N_HEADS],
                           wt[:, o + 2 * N_HEADS + 4 * hd:],
                           jnp.zeros((depth, LANES - N_GATE, D_MODEL), BF16)], axis=1)
    adds = _gate_vec([(N_HEADS, gdn_dt_bias), (2 * N_HEADS, mlstm_i_bias), (3 * N_HEADS, mlstm_f_bias)], depth)
    alog = _gate_vec([(N_HEADS, gdn_a_log)], depth)
    pa, pl_, ra, rl = _lane_form(adds), _lane_form(alog), _row_form(adds), _row_form(alog)
    g_mix = norm_mix.reshape(depth, 1, D_MODEL)
    cw = conv_w.astype(F32)
    gn = gdn_norm.reshape(depth, 1, DH).astype(F32)
    nrm = mlstm_norm.reshape(depth, 1, hd).astype(F32)
    wts = (w_out.astype(BF16), norm_ffn.reshape(depth, 1, D_MODEL), w_gate.astype(BF16), w_up.astype(BF16),
           w_down.astype(BF16), norm_ple.reshape(depth, 1, D_MODEL), w_ple_gate.astype(BF16),
           w_ple_proj.astype(BF16), gfin)
    cprev = jnp.pad(state_gdn_conv.astype(F32), ((0, 0), (0, 0), (pad - (CONV_W - 1), SLOT - pad), (0, 0)))
    cprev = cprev.reshape(depth, ns * SLOT, D_CONV)
    m0rows = jnp.pad(jnp.repeat(state_mlstm_m.astype(F32), SLOT, axis=1), ((0, 0), (0, 0), (0, LANES - N_HEADS)))
    s0_all, c0_all, n0_all = state_gdn.astype(F32), state_mlstm_C.astype(F32), state_mlstm_n.astype(F32)

    outs_p = [[] for _ in range(5)]
    outs_s = [[] for _ in range(5)]
    for i in range(depth):
        final = i == depth - 1
        proj_p, grow_p = _in_proj(xp, g_mix, w_r, i)
        proj_s, grow_s = _in_proj(xs, g_mix, w_r, i)
        proj_p3 = proj_p.reshape(nb, t, N_PROJ)
        proj_s3 = proj_s.reshape(1, ns * SLOT, N_PROJ)

        prep_p = _gdn_prep(proj_p3, grow_p, None, cw, pa, pl_, ra, rl, i, seq_len=c, valid_lo=0, idt=BF16, tb=512)
        mixa_p, s_p = _gdn_scan_prompt(*prep_p, proj_p3, gn, i, ts=512, bs=4)
        prep_s = _gdn_prep(proj_s3, grow_s, cprev, cw, pa, pl_, ra, rl, i, seq_len=SLOT, valid_lo=pad, idt=F32, tb=c)
        mixa_s, s_s = _gdn_scan_decode(*prep_s, proj_s, gn, s0_all, outs_s[0] if final else [], i)

        mixb_p, c_p, n_p, m_p = _mlstm_prompt(proj_p3, grow_p.reshape(nb, t // c, N_GATE, c), pa, ra, nrm, i,
                                              ts=512, bs=4)
        mixb_s, n_s, m_s, c_s = _mlstm_decode(proj_s, grow_s, pa, ra, nrm, c0_all, n0_all, m0rows,
                                              outs_s[2] if final else [], i)

        xp = _post(xp, mixa_p.reshape(nb * t, hd), mixb_p.reshape(nb * t, hd), pp_all, *wts, i, final=final)
        xs = _post(xs, mixa_s, mixb_s, ps_all, *wts, i, final=final)

        outs_p[0].append(s_p)
        outs_p[1].append(proj_p3[:, t - (CONV_W - 1):, :D_CONV])
        outs_p[2].append(c_p)
        outs_p[3].append(n_p)
        outs_p[4].append(m_p[:, 0, :N_HEADS])
        outs_s[0].append(s_s)
        outs_s[1].append(proj_s.reshape(ns, SLOT, N_PROJ)[:, SLOT - (CONV_W - 1):, :D_CONV])
        outs_s[2].append(c_s)
        outs_s[3].append(n_s)
        outs_s[4].append(m_s.reshape(ns, SLOT, LANES)[:, SLOT - 1, :N_HEADS])

    y_prompt = xp.reshape(nb, t, D_MODEL)
    y_sample = xs.reshape(ns, SLOT, D_MODEL)[:, pad:, :]
    sp = [jnp.stack(a, axis=0) for a in outs_p]
    ss = [outs_s[j][-1] if j in (0, 2) else jnp.stack(outs_s[j], axis=0) for j in range(5)]
    return (y_prompt, y_sample, *sp, *ss)
```

```python
import functools

import jax
import jax.numpy as jnp
from jax import lax
from jax.experimental import pallas as pl
from jax.experimental.pallas import tpu as pltpu

F32 = jnp.float32
BF16 = jnp.bfloat16

D_MODEL = 1024
N_HEADS = 4
DH = 128
D_CONV = 3 * N_HEADS * DH
D_FF = 2816
D_PLE = 256
CONV_W = 4
CHUNK = 64
EPS = 1e-6
NEG = -1e30
N_GATE = 16
LANES = 128
N_PROJ = D_CONV + 5 * N_HEADS * DH + LANES
GATE_COL_BLOCK = (N_PROJ - LANES) // LANES
SLOT = 8
SLOT_PAD = 4
V7X_VMEM_LIMIT_BYTES = 56 * 1024 * 1024
ROW_TILE = 512
FF_CHUNK = 256
PREP_GROUPS = 2
PREP_SKEW_STAGES = 6


def _cparams(n_axes):
    return pltpu.CompilerParams(dimension_semantics=("arbitrary",) * n_axes,
                                vmem_limit_bytes=V7X_VMEM_LIMIT_BYTES)


def _rms(x, g):
    return x * lax.rsqrt(jnp.mean(x * x, axis=-1, keepdims=True) + EPS) * g


def _softplus(x):
    return jnp.maximum(x, 0.0) + jnp.log1p(jnp.exp(-jnp.abs(x)))


def _dot(a, b):
    return jnp.dot(a, b, preferred_element_type=F32)


def _dot_nt(a, b):
    return lax.dot_general(a, b, (((1,), (1,)), ((), ())), preferred_element_type=F32)


def _split(a):
    hi = a.astype(BF16)
    lo = (a - hi.astype(F32)).astype(BF16)
    return hi, lo


def _split_all(xs):
    return [_split(x) for x in xs]


def _mm3_all(a_list, b_list):
    out = []
    for (ah, al), (bh, bl) in zip(a_list, b_list):
        m = ah.shape[0]
        r = _dot(jnp.concatenate([ah, al], axis=0), bh)
        out.append(r[0:m] + (_dot(ah, bl) + r[m:2 * m]))
    return out


class _PairMasks:
    def __init__(self, seq_len):
        c = CHUNK
        self.r = lax.broadcasted_iota(jnp.int32, (c, 2 * c), 0)
        lane = lax.broadcasted_iota(jnp.int32, (c, 2 * c), 1)
        self.cc = lane & (c - 1)
        self.lo = lane < c
        if seq_len < c:
            sh = seq_len.bit_length() - 1
            same = (self.r >> sh) == (self.cc >> sh)
        else:
            same = self.r >= 0
        self.causal = (self.r >= self.cc) & same
        self.strict_f = jnp.where((self.r > self.cc) & same, 1.0, 0.0)
        self.eye = jnp.where(self.r == self.cc, 1.0, 0.0)
        self.lo_b = jnp.where(self.lo, 1.0, 0.0).astype(BF16)
        self.hi_b = jnp.where(self.lo, 0.0, 1.0).astype(BF16)

    def blockdiag(self, sp):
        return tuple(jnp.concatenate([x * self.lo_b, x * self.hi_b], axis=0) for x in sp)


def _unit_lower_inverse_pairs(l_list, pm, seq_len, out):
    def bd_all(sps):
        return [pm.blockdiag(sp) for sp in sps]

    blk8 = jnp.where((pm.r >> 3) == (pm.cc >> 3), 1.0, 0.0)
    n0 = [l * blk8 for l in l_list]
    n0s = _split_all(n0)
    n2 = _mm3_all(n0s, bd_all(n0s))
    yield
    n2s = _split_all(n2)
    n4 = _mm3_all(n2s, bd_all(n2s))
    p = _mm3_all(_split_all([pm.eye - a for a in n0]), bd_all(_split_all([pm.eye + a for a in n2])))
    yield
    d = _mm3_all(_split_all(p), bd_all(_split_all([pm.eye + a for a in n4])))
    yield
    ds = _split_all(d)
    s = 8
    while s < seq_len:
        sh = s.bit_length() - 1
        off = jnp.where(((pm.r >> (sh + 1)) == (pm.cc >> (sh + 1))) & ((pm.r >> sh) != (pm.cc >> sh)), 1.0, 0.0)
        de = _mm3_all(ds, bd_all(_split_all([l * off for l in l_list])))
        yield
        ded = _mm3_all(_split_all(de), bd_all(ds))
        yield
        d = [a - b for a, b in zip(d, ded)]
        ds = _split_all(d)
        s *= 2
    out.extend(ds)


def _in_proj_kernel(x_ref, g_ref, wt_ref, proj_ref, gt_ref):
    hb = _rms(x_ref[...], g_ref[...]).astype(BF16)
    proj_ref[...] = _dot_nt(hb, wt_ref[...])
    gt = proj_ref[:, N_PROJ - LANES:N_PROJ].T
    for j in range(gt_ref.shape[0]):
        gt_ref[j] = gt[0:N_GATE, CHUNK * j:CHUNK * (j + 1)]


def _layer_spec(layer, *shape):
    zeros = (0,) * len(shape)
    return pl.BlockSpec((None,) + shape, lambda *_: (layer,) + zeros)


def _in_proj(x, g, w, layer):
    rows = x.shape[0]
    tm = ROW_TILE
    return pl.pallas_call(
        _in_proj_kernel,
        grid=(rows // tm,),
        in_specs=[
            pl.BlockSpec((tm, D_MODEL), lambda i: (i, 0)),
            _layer_spec(layer, 1, D_MODEL),
            _layer_spec(layer, N_PROJ, D_MODEL),
        ],
        out_specs=[
            pl.BlockSpec((tm, N_PROJ), lambda i: (i, 0)),
            pl.BlockSpec((tm // CHUNK, N_GATE, CHUNK), lambda i: (i, 0, 0)),
        ],
        out_shape=[jax.ShapeDtypeStruct((rows, N_PROJ), F32),
                   jax.ShapeDtypeStruct((rows // CHUNK, N_GATE, CHUNK), F32)],
        compiler_params=_cparams(1),
        name="in_proj",
    )(x, g, w)


def _chunk_masks(seq_len):
    c = CHUNK
    ri = lax.broadcasted_iota(jnp.int32, (c, c), 0)
    ci = lax.broadcasted_iota(jnp.int32, (c, c), 1)
    if seq_len < c:
        sh = seq_len.bit_length() - 1
        same = (ri >> sh) == (ci >> sh)
    else:
        same = ri >= 0
    return ri, ci, same


def _gdn_prep_kernel(*refs, tb, seq_len, valid_lo, idt):
    nk = tb // CHUNK
    groups = PREP_GROUPS if nk % PREP_GROUPS == 0 else 1
    gens = [_gdn_prep_group(refs[:-5], refs[-5:], g * (nk // groups), nk // groups, seq_len, valid_lo, idt)
            for g in range(groups)]
    _run_skewed(gens, PREP_SKEW_STAGES)


def _run_skewed(gens, skew):
    live = []
    pending = list(gens)
    tick = 0
    while live or pending:
        if pending and tick % skew == 0:
            live.append(pending.pop(0))
        for g in list(live):
            try:
                next(g)
            except StopIteration:
                live.remove(g)
        tick += 1


def _gdn_prep_group(ins, outs, k0, nk, seq_len, valid_lo, idt):
    has_cprev = valid_lo > 0
    if has_cprev:
        u_ref, prev_ref, gcol_ref, grow_ref, cprev_ref, cw_ref, pa_ref, pl_ref, ra_ref, rl_ref = ins
        assert k0 == 0
    else:
        u_ref, prev_ref, gcol_ref, grow_ref, cw_ref, pa_ref, pl_ref, ra_ref, rl_ref = ins
    wkqg_ref, wv_ref, qk_ref, kdt_ref, el_ref = outs
    c = CHUNK
    sh = seq_len.bit_length() - 1
    r_lo, nrows = k0 * c, nk * c
    u = u_ref[0, r_lo:r_lo + nrows, :]
    rowpos = lax.broadcasted_iota(jnp.int32, (nrows, 1), 0) & (seq_len - 1)
    if has_cprev:
        u = jnp.where((rowpos >= valid_lo - (CONV_W - 1)) & (rowpos < valid_lo), cprev_ref[0], u)
    if k0 == 0:
        prev = jnp.where(pl.program_id(1) == 0, 0.0, prev_ref[0])
    else:
        prev = u_ref[0, r_lo - 8:r_lo, :]
    xp = jnp.concatenate([prev, u], axis=0)
    cw = cw_ref[...]
    ys = []
    for part in range(3):
        cs = slice(part * N_HEADS * DH, (part + 1) * N_HEADS * DH)
        y = None
        for i in range(CONV_W):
            s = CONV_W - 1 - i
            xs = u[:, cs] if s == 0 else pltpu.roll(xp[:, cs], s, 0)[8:8 + nrows]
            t = xs * cw[i:i + 1, cs]
            y = t if y is None else y + t
        ys.append(jax.nn.silu(y))
        yield
    yq, yk, yv = ys

    lane = lax.broadcasted_iota(jnp.int32, (1, LANES), 1)
    xg = gcol_ref[0, r_lo:r_lo + nrows, :] + pa_ref[0:1, :]
    gcolv = jnp.where(lane < N_HEADS, jax.nn.sigmoid(xg), -jnp.exp(pl_ref[0:1, :]) * _softplus(xg))
    if valid_lo > 0:
        gcolv = jnp.where(rowpos >= valid_lo, gcolv, 0.0)

    ri, ci, same = _chunk_masks(seq_len)
    causal = (ri >= ci) & same
    upper = (ri <= ci) & same
    colpos = lax.broadcasted_iota(jnp.int32, (1, c), 1) & (seq_len - 1)
    r8 = lax.broadcasted_iota(jnp.int32, (8, c), 0)
    c8 = lax.broadcasted_iota(jnp.int32, (8, c), 1)

    items = []
    for kk in range(nk):
        r0 = kk * c
        k = k0 + kk
        growv = -jnp.exp(rl_ref[:, 0:1]) * _softplus(grow_ref[k] + ra_ref[:, 0:1])
        if valid_lo > 0:
            growv = jnp.where(colpos >= valid_lo, growv, 0.0)
        for h in range(N_HEADS):
            hs = slice(DH * h, DH * (h + 1))
            qc = yq[r0:r0 + c, hs]
            kc = yk[r0:r0 + c, hs]
            g_col = gcolv[r0:r0 + c, N_HEADS + h:N_HEADS + h + 1]
            g_row = growv[N_HEADS + h:N_HEADS + h + 1, :]
            items.append(dict(
                k=k, h=h, hs=hs, qc=qc, kc=kc, vc=yv[r0:r0 + c, hs],
                beta=gcolv[r0:r0 + c, h:h + 1],
                qss=jnp.sum(qc * qc, axis=-1, keepdims=True),
                kss=jnp.sum(kc * kc, axis=-1, keepdims=True),
                gc_col=jnp.sum(jnp.where(causal, g_row, 0.0), axis=1, keepdims=True),
                gc_row=jnp.sum(jnp.where(upper, g_col, 0.0), axis=0, keepdims=True),
                gl_col=jnp.sum(jnp.where(same, g_row, 0.0), axis=1, keepdims=True),
                gl8=jnp.sum(jnp.where((c8 >> sh) == r8, g_row, 0.0), axis=1, keepdims=True)))
    yield
    for it in items:
        k, hs, gc_col, beta = it["k"], it["hs"], it["gc_col"], it["beta"]
        qn = it["qc"] * lax.rsqrt(it["qss"] + EPS) * (DH ** -0.5)
        kn = it["kc"] * lax.rsqrt(it["kss"] + EPS)
        wkqg_ref[0, k, c:2 * c, hs] = (qn * jnp.exp(gc_col)).astype(idt)
        el_ref[0, k, :, hs] = jnp.broadcast_to(jnp.exp(it["gl8"]), (8, DH))
        it["kd"] = kn * jnp.exp(it["gl_col"] - gc_col)
        it["qb"] = qn.astype(BF16)
        it["kb"] = kn.astype(BF16)
        it["rhs"] = jnp.concatenate([beta * it["vc"], (beta * jnp.exp(gc_col)) * kn], axis=1)
    yield
    for a, b in zip(items[0::2], items[1::2]):
        kdt_ref[0, a["k"], a["h"] // 2] = jnp.concatenate([a["kd"], b["kd"]], axis=0).T.astype(idt)
    pm = _PairMasks(seq_len)
    zb = jnp.zeros((c, DH), BF16)
    pairs = []
    for a, b in zip(items[0::2], items[1::2]):
        gcp = jnp.where(pm.lo, a["gc_col"], b["gc_col"])
        grp = jnp.concatenate([a["gc_row"], b["gc_row"]], axis=1)
        decay = jnp.where(pm.causal, jnp.exp(jnp.where(pm.causal, gcp - grp, 0.0)), 0.0)
        kbd = jnp.concatenate([jnp.concatenate([a["kb"], zb], axis=1),
                               jnp.concatenate([zb, b["kb"]], axis=1)], axis=0)
        kq = jnp.concatenate([jnp.concatenate([a["kb"], b["kb"]], axis=1),
                              jnp.concatenate([a["qb"], b["qb"]], axis=1)], axis=0)
        pairs.append(dict(a=a, b=b, decay=decay, kbd=kbd, kq=kq, beta=jnp.where(pm.lo, a["beta"], b["beta"])))
    kqs = [_dot_nt(p["kq"], p["kbd"]) for p in pairs]
    yield
    for p, kq in zip(pairs, kqs):
        qk_ref[0, p["a"]["k"], p["a"]["h"] // 2] = (kq[c:2 * c] * p["decay"]).astype(idt)
    ls = [p["beta"] * kq[0:c] * p["decay"] * pm.strict_f for p, kq in zip(pairs, kqs)]
    tinvs = []
    yield from _unit_lower_inverse_pairs(ls, pm, seq_len, tinvs)
    for p, (th, tl) in zip(pairs, tinvs):
        rh, rl_ = _split(jnp.concatenate([p["a"]["rhs"], p["b"]["rhs"]], axis=0))
        ta_h, tb_h = th * pm.lo_b, th * pm.hi_b
        r1 = _dot(jnp.concatenate([ta_h, tl * pm.lo_b, tb_h, tl * pm.hi_b], axis=0), rh)
        r2 = _dot(jnp.concatenate([ta_h, tb_h], axis=0), rl_)
        p["sol"] = (r1[0:c] + (r2[0:c] + r1[c:2 * c]), r1[2 * c:3 * c] + (r2[c:2 * c] + r1[3 * c:4 * c]))
    yield
    for p in pairs:
        for it, sol in zip((p["a"], p["b"]), p["sol"]):
            wv_ref[0, it["k"], :, it["hs"]] = sol[:, 0:DH]
            wkqg_ref[0, it["k"], 0:c, it["hs"]] = sol[:, DH:2 * DH].astype(idt)


def _gdn_prep(proj3, grow, cprev, cw, pa, pl_, ra, rl, layer, *, seq_len, valid_lo, idt, tb):
    nb, t, _ = proj3.shape
    c = CHUNK
    nt = t // tb
    kb = tb // c
    has_cprev = valid_lo > 0
    in_specs = [
        pl.BlockSpec((1, tb, D_CONV), lambda b, i: (b, i, 0)),
        pl.BlockSpec((1, 8, D_CONV), lambda b, i: (b, jnp.maximum(i * (tb // 8) - 1, 0), 0)),
        pl.BlockSpec((1, tb, LANES), lambda b, i: (b, i, GATE_COL_BLOCK)),
        pl.BlockSpec((kb, N_GATE, c), lambda b, i: (b * nt + i, 0, 0)),
    ]
    args = [proj3, proj3, proj3, grow]
    if has_cprev:
        assert nb == 1
        in_specs.append(pl.BlockSpec((1, tb, D_CONV), lambda b, i: (layer, i, 0)))
        args.append(cprev)
    in_specs += [
        _layer_spec(layer, CONV_W, D_CONV),
        _layer_spec(layer, 8, LANES),
        _layer_spec(layer, 8, LANES),
        _layer_spec(layer, N_GATE, LANES),
        _layer_spec(layer, N_GATE, LANES),
    ]
    args += [cw, pa, pl_, ra, rl]
    n = t // c
    hd = N_HEADS * DH
    out_shape = [
        jax.ShapeDtypeStruct((nb, n, 2 * c, hd), idt),
        jax.ShapeDtypeStruct((nb, n, c, hd), F32),
        jax.ShapeDtypeStruct((nb, n, N_HEADS // 2, c, 2 * c), idt),
        jax.ShapeDtypeStruct((nb, n, N_HEADS // 2, DH, 2 * c), idt),
        jax.ShapeDtypeStruct((nb, n, 8, hd), F32),
    ]
    out_specs = [
        pl.BlockSpec((1, kb, 2 * c, hd), lambda b, i: (b, i, 0, 0)),
        pl.BlockSpec((1, kb, c, hd), lambda b, i: (b, i, 0, 0)),
        pl.BlockSpec((1, kb, N_HEADS // 2, c, 2 * c), lambda b, i: (b, i, 0, 0, 0)),
        pl.BlockSpec((1, kb, N_HEADS // 2, DH, 2 * c), lambda b, i: (b, i, 0, 0, 0)),
        pl.BlockSpec((1, kb, 8, hd), lambda b, i: (b, i, 0, 0)),
    ]
    return pl.pallas_call(
        functools.partial(_gdn_prep_kernel, tb=tb, seq_len=seq_len, valid_lo=valid_lo, idt=idt),
        grid=(nb, nt),
        in_specs=in_specs,
        out_specs=out_specs,
        out_shape=out_shape,
        compiler_params=_cparams(2),
        name="gdn_prep",
    )(*args)


def _lane_half_masks(dtype):
    lo = lax.broadcasted_iota(jnp.int32, (CHUNK, 2 * CHUNK), 1) < CHUNK
    return jnp.where(lo, 1.0, 0.0).astype(dtype), jnp.where(lo, 0.0, 1.0).astype(dtype)


def _gated_norm_store(o, z, gn, mix_ref, idx):
    mix_ref[idx] = (_rms(o, gn) * jax.nn.silu(z)).astype(BF16)


def _gdn_scan_prompt_kernel(wkqg_ref, wv_ref, qk_ref, kdt_ref, el_ref, z_ref, gn_ref,
                            mix_ref, sout_ref, s_scr, *, nc, bs):
    c = CHUNK

    @pl.when(pl.program_id(1) == 0)
    def _():
        s_scr[...] = jnp.zeros_like(s_scr)

    gn = gn_ref[...]
    units = [(b, h, slice(DH * h, DH * (h + 1))) for b in range(bs) for h in range(N_HEADS)]
    half_b = _lane_half_masks(BF16)
    half2_b = tuple(jnp.concatenate([m, m], axis=0) for m in half_b)

    def body(n, carry):
        rows = pl.ds(pl.multiple_of(n * c, c), c)
        ss = [s_scr[b * N_HEADS + h] for b, h, _ in units]
        aa = [_dot(wkqg_ref[b, n, :, hs], s.astype(BF16)) for (b, _, hs), s in zip(units, ss)]
        ubs = [(wv_ref[b, n, :, hs] - a[0:c]).astype(BF16) for (b, _, hs), a in zip(units, aa)]
        os_, ds = [], []
        for j in range(len(units) // 2):
            b, h, _ = units[2 * j]
            qkp, kdp = qk_ref[b, n, h // 2], kdt_ref[b, n, h // 2]
            r = _dot(jnp.concatenate([qkp * half_b[0], qkp * half_b[1], kdp * half2_b[0], kdp * half2_b[1]],
                                     axis=0),
                     jnp.concatenate(ubs[2 * j:2 * j + 2], axis=0))
            os_ += [r[0:c], r[c:2 * c]]
            ds += [r[2 * c:2 * c + DH], r[2 * c + DH:2 * c + 2 * DH]]
        for i, (b, h, hs) in enumerate(units):
            s_scr[b * N_HEADS + h] = el_ref[b, n, 0:1, hs] * ss[i] + ds[i]
            _gated_norm_store(aa[i][c:2 * c] + os_[i], z_ref[b, rows, hs], gn, mix_ref, (b, rows, hs))
        return carry

    lax.fori_loop(0, nc, body, 0)

    @pl.when(pl.program_id(1) == pl.num_programs(1) - 1)
    def _():
        for b in range(bs):
            sout_ref[b] = s_scr[b * N_HEADS:(b + 1) * N_HEADS]


def _gdn_scan_prompt(wkqg, wv, qk, kdt, el, proj3, gn, layer, *, ts, bs):
    nb, n, _, hd = wkqg.shape
    c = CHUNK
    t = n * c
    nc = ts // c
    zblk = D_CONV // hd
    return pl.pallas_call(
        functools.partial(_gdn_scan_prompt_kernel, nc=nc, bs=bs),
        grid=(nb // bs, t // ts),
        in_specs=[
            pl.BlockSpec((bs, nc, 2 * c, hd), lambda b, i: (b, i, 0, 0)),
            pl.BlockSpec((bs, nc, c, hd), lambda b, i: (b, i, 0, 0)),
            pl.BlockSpec((bs, nc, N_HEADS // 2, c, 2 * c), lambda b, i: (b, i, 0, 0, 0)),
            pl.BlockSpec((bs, nc, N_HEADS // 2, DH, 2 * c), lambda b, i: (b, i, 0, 0, 0)),
            pl.BlockSpec((bs, nc, 8, hd), lambda b, i: (b, i, 0, 0)),
            pl.BlockSpec((bs, ts, hd), lambda b, i: (b, i, zblk)),
            _layer_spec(layer, 1, DH),
        ],
        out_specs=[
            pl.BlockSpec((bs, ts, hd), lambda b, i: (b, i, 0)),
            pl.BlockSpec((bs, N_HEADS, DH, DH), lambda b, i: (b, 0, 0, 0)),
        ],
        out_shape=[jax.ShapeDtypeStruct((nb, t, hd), BF16),
                   jax.ShapeDtypeStruct((nb, N_HEADS, DH, DH), F32)],
        scratch_shapes=[pltpu.VMEM((bs * N_HEADS, DH, DH), F32)],
        compiler_params=_cparams(2),
        name="gdn_scan_prompt",
    )(wkqg, wv, qk, kdt, el, proj3, gn)


def _gdn_scan_decode_kernel(*refs, n_prev):
    (wkqg_ref, wv_ref, qk_ref, kdt_ref, el_ref, z_ref, gn_ref, s0_ref), rest = refs[:8], refs[8:]
    prev_refs, (mix_ref, sout_ref) = rest[:n_prev], rest[n_prev:]
    if n_prev:
        for l, p_ref in enumerate(prev_refs):
            sout_ref[l] = p_ref[...]
        sout_ref = sout_ref.at[n_prev]
    c = CHUNK
    nseq = c // SLOT
    rowseq = lax.broadcasted_iota(jnp.int32, (c, 1), 0) >> 3
    gn = gn_ref[...]
    heads = [slice(DH * h, DH * (h + 1)) for h in range(N_HEADS)]
    aa = []
    for h, hs in enumerate(heads):
        w = wkqg_ref[0, 0, :, hs]
        row = []
        for j in range(nseq):
            wj = jnp.concatenate([w[SLOT * j:SLOT * (j + 1)], w[c + SLOT * j:c + SLOT * (j + 1)]],
                                 axis=0).astype(BF16)
            row.append(_dot(wj, s0_ref[j, h].astype(BF16)))
        aa.append(row)
    us = [wv_ref[0, 0, :, hs] - jnp.concatenate([a[0:SLOT] for a in aa[h]], axis=0)
          for h, hs in enumerate(heads)]
    half_f = _lane_half_masks(F32)
    os_ = []
    for j in range(N_HEADS // 2):
        qkp = qk_ref[0, 0, j]
        o2 = _dot(jnp.concatenate([qkp * half_f[0], qkp * half_f[1]], axis=0).astype(BF16),
                  jnp.concatenate(us[2 * j:2 * j + 2], axis=0).astype(BF16))
        os_ += [o2[0:c], o2[c:2 * c]]
    half2_f = tuple(jnp.concatenate([m, m], axis=0) for m in half_f)
    for p in range(N_HEADS // 2):
        kdp = kdt_ref[0, 0, p]
        kd2 = jnp.concatenate([kdp * half2_f[0], kdp * half2_f[1]], axis=0).astype(BF16)
        for j in range(nseq):
            uj = jnp.concatenate([jnp.where(rowseq == j, us[2 * p + i], 0.0) for i in range(2)], axis=0)
            r = _dot(kd2, uj.astype(BF16))
            for i in range(2):
                h = 2 * p + i
                sout_ref[j, h] = (el_ref[0, 0, j:j + 1, heads[h]] * s0_ref[j, h] + r[DH * i:DH * (i + 1)])
    for h, hs in enumerate(heads):
        o = jnp.concatenate([a[SLOT:2 * SLOT] for a in aa[h]], axis=0) + os_[h]
        _gated_norm_store(o, z_ref[:, hs], gn, mix_ref, (slice(None), hs))


def _state_specs(layer, prev, state_shape, nseq):
    tail = state_shape[2:]
    zeros = (0,) * len(tail)
    in_specs = [pl.BlockSpec((None, nseq) + tail, lambda i: (layer, i) + zeros)]
    in_specs += [pl.BlockSpec((nseq,) + tail, lambda i: (i,) + zeros) for _ in prev]
    if prev:
        out_spec = pl.BlockSpec((len(prev) + 1, nseq) + tail, lambda i: (0, i) + zeros)
        out_shape = jax.ShapeDtypeStruct((len(prev) + 1,) + state_shape[1:], F32)
    else:
        out_spec = pl.BlockSpec((nseq,) + tail, lambda i: (i,) + zeros)
        out_shape = jax.ShapeDtypeStruct(state_shape[1:], F32)
    return in_specs, out_spec, out_shape


def _gdn_scan_decode(wkqg, wv, qk, kdt, el, proj, gn, s0_all, prev, layer):
    _, n, _, hd = wkqg.shape
    c = CHUNK
    nseq = c // SLOT
    zblk = D_CONV // hd
    st_in, st_out, st_shape = _state_specs(layer, prev, s0_all.shape, nseq)
    return pl.pallas_call(
        functools.partial(_gdn_scan_decode_kernel, n_prev=len(prev)),
        grid=(n,),
        in_specs=[
            pl.BlockSpec((1, 1, 2 * c, hd), lambda i: (0, i, 0, 0)),
            pl.BlockSpec((1, 1, c, hd), lambda i: (0, i, 0, 0)),
            pl.BlockSpec((1, 1, N_HEADS // 2, c, 2 * c), lambda i: (0, i, 0, 0, 0)),
            pl.BlockSpec((1, 1, N_HEADS // 2, DH, 2 * c), lambda i: (0, i, 0, 0, 0)),
            pl.BlockSpec((1, 1, 8, hd), lambda i: (0, i, 0, 0)),
            pl.BlockSpec((c, hd), lambda i: (i, zblk)),
            _layer_spec(layer, 1, DH),
        ] + st_in,
        out_specs=[pl.BlockSpec((c, hd), lambda i: (i, 0)), st_out],
        out_shape=[jax.ShapeDtypeStruct((n * c, hd), BF16), st_shape],
        compiler_params=_cparams(1),
        name="gdn_scan_decode",
    )(wkqg, wv, qk, kdt, el, proj, gn, s0_all, *prev)


def _mlstm_gates(gcol, growk, pa_ref, ra_ref, h, rows, valid_lo, rowpos, colpos):
    xg = gcol + pa_ref[0:1, :]
    ig_col = xg[:, 2 * N_HEADS + h:2 * N_HEADS + h + 1]
    fl_col = -_softplus(-xg[:, 3 * N_HEADS + h:3 * N_HEADS + h + 1])
    xr = growk + ra_ref[:, 0:1]
    ig_row = xr[2 * N_HEADS + h:2 * N_HEADS + h + 1, :]
    fl_row = -_softplus(-xr[3 * N_HEADS + h:3 * N_HEADS + h + 1, :])
    if valid_lo > 0:
        ig_col = jnp.where(rowpos >= valid_lo, ig_col, NEG)
        fl_col = jnp.where(rowpos >= valid_lo, fl_col, 0.0)
        ig_row = jnp.where(colpos >= valid_lo, ig_row, NEG)
        fl_row = jnp.where(colpos >= valid_lo, fl_row, 0.0)
    return ig_col, fl_col, ig_row, fl_row


def _mlstm_chunks(probs, seq_len, finish):
    c = CHUNK
    nseq = c // seq_len
    ri, ci, same = _chunk_masks(seq_len)
    causal = (ri >= ci) & same
    upper = (ri <= ci) & same
    seq_end = ci == (ri | (seq_len - 1))
    rowseq = lax.broadcasted_iota(jnp.int32, (c, 1), 0) >> (seq_len.bit_length() - 1)
    zpad = jnp.zeros((16 - seq_len, DH), F32) if nseq > 1 else None
    ones_b = jnp.ones((c, LANES), BF16)
    for p in probs:
        p["f_row"] = jnp.sum(jnp.where(upper, p["fl_col"], 0.0), axis=0, keepdims=True)
        p["km"] = p["k"] * (DH ** -0.5)
        p["qb"] = p["q"].astype(BF16)
        p["kb"] = p["km"].astype(BF16)
        p["vb"] = p["v"].astype(BF16)
        if nseq == 1:
            r = _dot(jnp.concatenate(_split(jnp.where(causal, p["fl_row"], 0.0)), axis=0), ones_b)
            p["f_col"] = r[0:c] + r[c:2 * c]
            nb2 = p["n_rows"].astype(BF16)
            p["qn"] = _dot_nt(p["qb"], jnp.concatenate([nb2, nb2], axis=0))
            p["ig_col"] = jnp.broadcast_to(p["ig_col"], (c, LANES))
        else:
            p["f_col"] = jnp.sum(jnp.where(causal, p["fl_row"], 0.0), axis=1, keepdims=True)
            p["qn"] = jnp.sum(p["q"] * p["n_rows"], axis=1, keepdims=True)
    yield
    for p in probs:
        p["b_col"] = p["ig_col"] - p["f_col"]
        p["b_row"] = p["ig_row"] - p["f_row"]
        p["bmax_col"] = jnp.max(jnp.where(causal, p["b_row"], -jnp.inf), axis=1, keepdims=True)
        if nseq == 1:
            p["bmax_col"] = jnp.broadcast_to(p["bmax_col"], (c, LANES))
        if nseq > 1:
            p["fl_end"] = jnp.sum(jnp.where(seq_end, p["f_row"], 0.0), axis=1, keepdims=True)
            p["bmax_end"] = jnp.max(jnp.where(same, p["b_row"], -jnp.inf), axis=1, keepdims=True)
    yield
    for p in probs:
        f_col, mp_col = p["f_col"], p["mp_col"]
        m_col = f_col + jnp.maximum(mp_col, p["bmax_col"])
        p["m_col"] = m_col
        p["a_col"] = jnp.exp(f_col + mp_col - m_col)
        fm = (f_col - m_col)[:, 0:c] if nseq == 1 else f_col - m_col
        p["dexp"] = jnp.where(causal, jnp.exp(jnp.where(causal, fm + p["b_row"], 0.0)), 0.0)
        if nseq == 1:
            fl_end, ml, mp = f_col[c - 1:c], m_col[c - 1:c], mp_col[c - 1:c]
            p["ml_col"] = ml
            p["al"] = jnp.exp(fl_end + mp - ml)
        else:
            fl_end, mp = p["fl_end"], mp_col
            ml = fl_end + jnp.maximum(mp_col, p["bmax_end"])
            p["ml_col"] = ml
            p["al"] = jnp.broadcast_to(jnp.exp(fl_end + mp - ml), (c, DH))
        p["kw"] = p["km"] * jnp.exp(fl_end + p["b_col"] - ml)
    yield
    for p in probs:
        p["kwt"] = p["kw"].T.astype(BF16)
    for p in probs:
        p["qk"] = _dot_nt(p["qb"], p["kb"])
    yield
    for p in probs:
        if nseq == 1:
            p["qc"] = _dot(p["qb"], p["c_list"][0].astype(BF16))
            p["upd"] = [_dot(p["kwt"], p["vb"])]
        else:
            parts, upd = [], []
            for j in range(nseq):
                qj = jnp.concatenate([p["q"][seq_len * j:seq_len * (j + 1)], zpad], axis=0).astype(BF16)
                parts.append(_dot(qj, p["c_list"][j].astype(BF16))[0:seq_len])
                upd.append(_dot(p["kwt"], jnp.where(rowseq == j, p["v"], 0.0).astype(BF16)))
            p["qc"] = jnp.concatenate(parts, axis=0)
            p["upd"] = upd
    yield
    if nseq == 1:
        for p in probs:
            p["w_hl"] = _split(p["dexp"] * p["qk"])
        yield
        for p in probs:
            r = _dot(jnp.concatenate(p["w_hl"], axis=0), jnp.concatenate([p["vb"], ones_b], axis=1))
            p["wv"] = r[0:c, 0:DH]
            p["wsum"] = r[0:c, DH:DH + LANES] + r[c:2 * c, DH:DH + LANES]
    else:
        for p in probs:
            p["w"] = p["dexp"] * p["qk"]
            p["wsum"] = jnp.sum(p["w"], axis=1, keepdims=True)
        for p in probs:
            p["wv"] = _dot(p["w"].astype(BF16), p["vb"])
    yield
    for p in probs:
        num = p["a_col"] * p["qc"] + p["wv"]
        den = p["a_col"] * p["qn"] + p["wsum"]
        p["hout"] = num / jnp.maximum(jnp.abs(den), jnp.exp(-p["m_col"]))
        p["new_c"] = [p["al"][seq_len * j:seq_len * j + 1] * p["c_list"][j] + p["upd"][j]
                      for j in range(nseq)]
    yield
    finish(probs)


def _mlstm_out_store(hout, og, gnorm, mix_ref, idx):
    x = hout * jax.nn.sigmoid(og)
    ssq = _dot(jnp.concatenate(_split(x * x), axis=0), jnp.ones((DH, DH), BF16))
    ms = (ssq[0:CHUNK] + ssq[CHUNK:2 * CHUNK]) * (1.0 / DH)
    mix_ref[idx] = (x * lax.rsqrt(ms + EPS) * gnorm).astype(BF16)


def _mlstm_prompt_kernel(q_ref, k_ref, v_ref, o_ref, gcol_ref, grow_ref, pa_ref, ra_ref, nrm_ref,
                         mix_ref, cout_ref, nout_ref, mout_ref, c_scr, n_scr, m_scr, *, nc, bs):
    c = CHUNK

    @pl.when(pl.program_id(1) == 0)
    def _():
        c_scr[...] = jnp.zeros_like(c_scr)
        n_scr[...] = jnp.zeros_like(n_scr)
        m_scr[...] = jnp.zeros_like(m_scr)

    def body(n, carry):
        r0 = pl.multiple_of(n * c, c)
        rows = pl.ds(r0, c)
        def finish(probs):
            for p in probs:
                st = p["st"]
                c_scr[st] = p["new_c"][0]
                n_scr[st] = jnp.broadcast_to(
                    p["al"][0:1] * p["n_rows"][0:1] + jnp.sum(p["kw"], axis=0, keepdims=True), (c, DH))
                m_scr[st] = jnp.broadcast_to(p["ml_col"], (c, DH))
                _mlstm_out_store(p["hout"], o_ref[p["b"], rows, p["hs"]], nrm_ref[:, p["hs"]], mix_ref,
                                 (p["b"], rows, p["hs"]))

        probs = []
        for b in range(bs):
            gcol = gcol_ref[b, rows, :]
            growk = grow_ref[b, n]
            for h in range(N_HEADS):
                hs = slice(DH * h, DH * (h + 1))
                ig_col, fl_col, ig_row, fl_row = _mlstm_gates(gcol, growk, pa_ref, ra_ref, h, c, 0, None, None)
                st = b * N_HEADS + h
                probs.append(dict(b=b, hs=hs, st=st, q=q_ref[b, rows, hs], k=k_ref[b, rows, hs],
                                  v=v_ref[b, rows, hs], ig_col=ig_col, fl_col=fl_col, ig_row=ig_row,
                                  fl_row=fl_row, mp_col=m_scr[st], n_rows=n_scr[st],
                                  c_list=[c_scr[st]]))
        _run_skewed([_mlstm_chunks(probs, c, finish)], 1)
        return carry

    lax.fori_loop(0, nc, body, 0)

    @pl.when(pl.program_id(1) == pl.num_programs(1) - 1)
    def _():
        lane = lax.broadcasted_iota(jnp.int32, (8, DH), 1)
        for b in range(bs):
            mo = jnp.zeros((8, DH), F32)
            for h in range(N_HEADS):
                st = b * N_HEADS + h
                cout_ref[b, h] = c_scr[st]
                nout_ref[b, h:h + 1, :] = n_scr[st][0:1]
                mo = jnp.where(lane == h, m_scr[st][0:8], mo)
            mout_ref[b] = mo


def _mlstm_prompt(proj3, grow4, pa, ra, nrm, layer, *, ts, bs):
    nb, t, _ = proj3.shape
    c = CHUNK
    hd = N_HEADS * DH
    nc = ts // c
    qblk = (D_CONV + hd) // hd
    return pl.pallas_call(
        functools.partial(_mlstm_prompt_kernel, nc=nc, bs=bs),
        grid=(nb // bs, t // ts),
        in_specs=[
            pl.BlockSpec((bs, ts, hd), lambda b, i: (b, i, qblk)),
            pl.BlockSpec((bs, ts, hd), lambda b, i: (b, i, qblk + 1)),
            pl.BlockSpec((bs, ts, hd), lambda b, i: (b, i, qblk + 2)),
            pl.BlockSpec((bs, ts, hd), lambda b, i: (b, i, qblk + 3)),
            pl.BlockSpec((bs, ts, LANES), lambda b, i: (b, i, GATE_COL_BLOCK)),
            pl.BlockSpec((bs, nc, N_GATE, c), lambda b, i: (b, i, 0, 0)),
            _layer_spec(layer, 8, LANES),
            _layer_spec(layer, N_GATE, LANES),
            _layer_spec(layer, 1, hd),
        ],
        out_specs=[
            pl.BlockSpec((bs, ts, hd), lambda b, i: (b, i, 0)),
            pl.BlockSpec((bs, N_HEADS, DH, DH), lambda b, i: (b, 0, 0, 0)),
            pl.BlockSpec((bs, N_HEADS, DH), lambda b, i: (b, 0, 0)),
            pl.BlockSpec((bs, 8, DH), lambda b, i: (b, 0, 0)),
        ],
        out_shape=[jax.ShapeDtypeStruct((nb, t, hd), BF16),
                   jax.ShapeDtypeStruct((nb, N_HEADS, DH, DH), F32),
                   jax.ShapeDtypeStruct((nb, N_HEADS, DH), F32),
                   jax.ShapeDtypeStruct((nb, 8, DH), F32)],
        scratch_shapes=[pltpu.VMEM((bs * N_HEADS, DH, DH), F32),
                        pltpu.VMEM((bs * N_HEADS, c, DH), F32),
                        pltpu.VMEM((bs * N_HEADS, c, DH), F32)],
        compiler_params=_cparams(2),
        name="mlstm_prompt",
    )(proj3, proj3, proj3, proj3, proj3, grow4, pa, ra, nrm)


def _mlstm_decode_kernel(*refs, n_prev):
    (q_ref, k_ref, v_ref, o_ref, gcol_ref, grow_ref, pa_ref, ra_ref, nrm_ref, n0_ref, m0_ref,
     c0_ref), rest = refs[:12], refs[12:]
    prev_refs, (mix_ref, nout_ref, mout_ref, cout_ref) = rest[:n_prev], rest[n_prev:]
    if n_prev:
        for l, p_ref in enumerate(prev_refs):
            cout_ref[l] = p_ref[...]
        cout_ref = cout_ref.at[n_prev]
    c = CHUNK
    nseq = c // SLOT
    rowpos = lax.broadcasted_iota(jnp.int32, (c, 1), 0) & (SLOT - 1)
    colpos = lax.broadcasted_iota(jnp.int32, (1, c), 1) & (SLOT - 1)
    lane = lax.broadcasted_iota(jnp.int32, (c, DH), 1)
    gcol = gcol_ref[...]
    growk = grow_ref[0]
    probs = []
    for h in range(N_HEADS):
        hs = slice(DH * h, DH * (h + 1))
        ig_col, fl_col, ig_row, fl_row = _mlstm_gates(gcol, growk, pa_ref, ra_ref, h, c, SLOT_PAD, rowpos, colpos)
        n_rows = jnp.concatenate(
            [jnp.broadcast_to(n0_ref[j, h:h + 1, :], (SLOT, DH)) for j in range(nseq)], axis=0)
        probs.append(dict(h=h, hs=hs, q=q_ref[:, hs], k=k_ref[:, hs], v=v_ref[:, hs], ig_col=ig_col,
                          fl_col=fl_col, ig_row=ig_row, fl_row=fl_row, mp_col=m0_ref[:, h:h + 1],
                          n_rows=n_rows, c_list=[c0_ref[j, h] for j in range(nseq)]))
    def finish(probs):
        mo = jnp.zeros((c, DH), F32)
        for p in probs:
            h, hs = p["h"], p["hs"]
            for j in range(nseq):
                cout_ref[j, h] = p["new_c"][j]
                rs = slice(SLOT * j, SLOT * (j + 1))
                nout_ref[j, h:h + 1, :] = (p["al"][SLOT * j:SLOT * j + 1] * n0_ref[j, h:h + 1, :]
                                           + jnp.sum(p["kw"][rs], axis=0, keepdims=True))
            mo = jnp.where(lane == h, jnp.broadcast_to(p["ml_col"], (c, DH)), mo)
            _mlstm_out_store(p["hout"], o_ref[:, hs], nrm_ref[:, hs], mix_ref, (slice(None), hs))
        mout_ref[...] = mo

    _run_skewed([_mlstm_chunks(probs, SLOT, finish)], 1)


def _mlstm_decode(proj, grow, pa, ra, nrm, c0_all, n0_all, m0rows_all, prev, layer):
    rows = proj.shape[0]
    c = CHUNK
    hd = N_HEADS * DH
    nseq = c // SLOT
    qblk = (D_CONV + hd) // hd
    st_in, st_out, st_shape = _state_specs(layer, prev, c0_all.shape, nseq)
    return pl.pallas_call(
        functools.partial(_mlstm_decode_kernel, n_prev=len(prev)),
        grid=(rows // c,),
        in_specs=[
            pl.BlockSpec((c, hd), lambda i: (i, qblk)),
            pl.BlockSpec((c, hd), lambda i: (i, qblk + 1)),
            pl.BlockSpec((c, hd), lambda i: (i, qblk + 2)),
            pl.BlockSpec((c, hd), lambda i: (i, qblk + 3)),
            pl.BlockSpec((c, LANES), lambda i: (i, GATE_COL_BLOCK)),
            pl.BlockSpec((1, N_GATE, c), lambda i: (i, 0, 0)),
            _layer_spec(layer, 8, LANES),
            _layer_spec(layer, N_GATE, LANES),
            _layer_spec(layer, 1, hd),
            pl.BlockSpec((None, nseq, N_HEADS, DH), lambda i: (layer, i, 0, 0)),
            pl.BlockSpec((None, c, LANES), lambda i: (layer, i, 0)),
        ] + st_in,
        out_specs=[
            pl.BlockSpec((c, hd), lambda i: (i, 0)),
            pl.BlockSpec((nseq, N_HEADS, DH), lambda i: (i, 0, 0)),
            pl.BlockSpec((c, LANES), lambda i: (i, 0)),
            st_out,
        ],
        out_shape=[jax.ShapeDtypeStruct((rows, hd), BF16),
                   jax.ShapeDtypeStruct(n0_all.shape[1:], F32),
                   jax.ShapeDtypeStruct((rows, LANES), F32),
                   st_shape],
        compiler_params=_cparams(1),
        name="mlstm_decode",
    )(proj, proj, proj, proj, proj, grow, pa, ra, nrm, n0_all, m0rows_all, c0_all, *prev)


def _post_kernel(x_ref, ma_ref, mb_ref, p_ref, woa_ref, wob_ref, gf_ref, wg_ref, wu_ref, wd_ref,
                 gp_ref, wpg_ref, wpp_ref, gfin_ref, o_ref, acc_ref, *, final):
    x = x_ref[...] + (_dot(ma_ref[...], woa_ref[...]) + _dot(mb_ref[...], wob_ref[...]))
    ub = _rms(x, gf_ref[...]).astype(BF16)
    for j in range(D_FF // FF_CHUNK):
        sl = slice(j * FF_CHUNK, (j + 1) * FF_CHUNK)
        a = (jax.nn.silu(_dot(ub, wg_ref[:, sl])) * _dot(ub, wu_ref[:, sl])).astype(BF16)
        d = _dot(a, wd_ref[sl, :])
        if j == 0:
            acc_ref[...] = d
        else:
            acc_ref[...] += d
    x = x + acc_ref[...]
    gate = jax.nn.sigmoid(_dot(_rms(x, gp_ref[...]).astype(BF16), wpg_ref[...]))
    x = x + _dot(p_ref[...].astype(BF16), wpp_ref[...]) * gate
    if final:
        x = _rms(x, gfin_ref[...])
    o_ref[...] = x


def _post(x, ma, mb, p_all, wo, gf, wg, wu, wd, gp, wpg, wpp, gfin, layer, *, final):
    rows = x.shape[0]
    tm = ROW_TILE
    hd = N_HEADS * DH
    row = lambda w: pl.BlockSpec((tm, w), lambda i: (i, 0))

    def whole(a, b, blk=0):
        return pl.BlockSpec((None, a, b), lambda i: (layer, blk, 0), pipeline_mode=pl.Buffered(1))

    return pl.pallas_call(
        functools.partial(_post_kernel, final=final),
        grid=(rows // tm,),
        in_specs=[row(D_MODEL), row(hd), row(hd),
                  pl.BlockSpec((None, tm, D_PLE), lambda i: (layer, i, 0)),
                  whole(hd, D_MODEL, 0), whole(hd, D_MODEL, 1), whole(1, D_MODEL),
                  whole(D_MODEL, D_FF), whole(D_MODEL, D_FF), whole(D_FF, D_MODEL),
                  whole(1, D_MODEL), whole(D_MODEL, D_MODEL), whole(D_PLE, D_MODEL),
                  pl.BlockSpec((1, D_MODEL), lambda i: (0, 0))],
        out_specs=row(D_MODEL),
        out_shape=jax.ShapeDtypeStruct((rows, D_MODEL), F32),
        scratch_shapes=[pltpu.VMEM((tm, D_MODEL), F32)],
        compiler_params=_cparams(1),
        name="post",
    )(x, ma, mb, p_all, wo, wo, gf, wg, wu, wd, gp, wpg, wpp, gfin)


def _gate_vec(pairs, depth):
    v = jnp.zeros((depth, N_GATE), F32)
    for off, val in pairs:
        v = v.at[:, off:off + N_HEADS].set(val.astype(F32))
    return v


def _lane_form(v):
    d = v.shape[0]
    return jnp.broadcast_to(jnp.pad(v, ((0, 0), (0, LANES - N_GATE)))[:, None, :], (d, 8, LANES))


def _row_form(v):
    d = v.shape[0]
    return jnp.broadcast_to(v[:, :, None], (d, N_GATE, LANES))


def kernel(x_prompt, x_sample, p_prompt, p_sample, state_gdn, state_gdn_conv, state_mlstm_C, state_mlstm_n, state_mlstm_m, w_in, conv_w, gdn_a_log, gdn_dt_bias, gdn_norm, mlstm_i_bias, mlstm_f_bias, mlstm_norm, w_out, norm_mix, norm_ffn, w_gate, w_up, w_down, norm_ple, w_ple_gate, w_ple_proj, norm_final):
    depth = w_in.shape[0]
    nb, t, _ = x_prompt.shape
    ns, tdec, _ = x_sample.shape
    hd = N_HEADS * DH
    c = CHUNK
    pad = SLOT - tdec

    xp = x_prompt.reshape(nb * t, D_MODEL)
    xs = jnp.pad(x_sample, ((0, 0), (pad, 0), (0, 0))).reshape(ns * SLOT, D_MODEL)
    ps_all = jnp.pad(p_sample, ((0, 0), (0, 0), (pad, 0), (0, 0))).reshape(depth, ns * SLOT, D_PLE)
    pp_all = p_prompt.reshape(depth, nb * t, D_PLE)
    gfin = norm_final.reshape(1, D_MODEL)

    o = D_CONV + hd
    wt = jnp.swapaxes(w_in, 1, 2).astype(BF16)
    w_r = jnp.concatenate([wt[:, :o], wt[:, o + 2 * N_HEADS:o + 2 * N_HEADS + 4 * hd], wt[:, o:o + 2 * N_HEADS],
                           wt[:, o + 2 * N_HEADS + 4 * hd:],
                           jnp.zeros((depth, LANES - N_GATE, D_MODEL), BF16)], axis=1)
    adds = _gate_vec([(N_HEADS, gdn_dt_bias), (2 * N_HEADS, mlstm_i_bias), (3 * N_HEADS, mlstm_f_bias)], depth)
    alog = _gate_vec([(N_HEADS, gdn_a_log)], depth)
    pa, pl_, ra, rl = _lane_form(adds), _lane_form(alog), _row_form(adds), _row_form(alog)
    g_mix = norm_mix.reshape(depth, 1, D_MODEL)
    cw = conv_w.astype(F32)
    gn = gdn_norm.reshape(depth, 1, DH).astype(F32)
    nrm = mlstm_norm.reshape(depth, 1, hd).astype(F32)
    wts = (w_out.astype(BF16), norm_ffn.reshape(depth, 1, D_MODEL), w_gate.astype(BF16), w_up.astype(BF16),
           w_down.astype(BF16), norm_ple.reshape(depth, 1, D_MODEL), w_ple_gate.astype(BF16),
           w_ple_proj.astype(BF16), gfin)
    cprev = jnp.pad(state_gdn_conv.astype(F32), ((0, 0), (0, 0), (pad - (CONV_W - 1), SLOT - pad), (0, 0)))
    cprev = cprev.reshape(depth, ns * SLOT, D_CONV)
    m0rows = jnp.pad(jnp.repeat(state_mlstm_m.astype(F32), SLOT, axis=1), ((0, 0), (0, 0), (0, LANES - N_HEADS)))
    s0_all, c0_all, n0_all = state_gdn.astype(F32), state_mlstm_C.astype(F32), state_mlstm_n.astype(F32)

    outs_p = [[] for _ in range(5)]
    outs_s = [[] for _ in range(5)]
    for i in range(depth):
        final = i == depth - 1
        proj_p, grow_p = _in_proj(xp, g_mix, w_r, i)
        proj_s, grow_s = _in_proj(xs, g_mix, w_r, i)
        proj_p3 = proj_p.reshape(nb, t, N_PROJ)
        proj_s3 = proj_s.reshape(1, ns * SLOT, N_PROJ)

        prep_p = _gdn_prep(proj_p3, grow_p, None, cw, pa, pl_, ra, rl, i, seq_len=c, valid_lo=0, idt=BF16, tb=512)
        mixa_p, s_p = _gdn_scan_prompt(*prep_p, proj_p3, gn, i, ts=512, bs=4)
        prep_s = _gdn_prep(proj_s3, grow_s, cprev, cw, pa, pl_, ra, rl, i, seq_len=SLOT, valid_lo=pad, idt=F32, tb=c)
        mixa_s, s_s = _gdn_scan_decode(*prep_s, proj_s, gn, s0_all, outs_s[0] if final else [], i)

        mixb_p, c_p, n_p, m_p = _mlstm_prompt(proj_p3, grow_p.reshape(nb, t // c, N_GATE, c), pa, ra, nrm, i,
                                              ts=512, bs=4)
        mixb_s, n_s, m_s, c_s = _mlstm_decode(proj_s, grow_s, pa, ra, nrm, c0_all, n0_all, m0rows,
                                              outs_s[2] if final else [], i)

        xp = _post(xp, mixa_p.reshape(nb * t, hd), mixb_p.reshape(nb * t, hd), pp_all, *wts, i, final=final)
        xs = _post(xs, mixa_s, mixb_s, ps_all, *wts, i, final=final)

        outs_p[0].append(s_p)
        outs_p[1].append(proj_p3[:, t - (CONV_W - 1):, :D_CONV])
        outs_p[2].append(c_p)
        outs_p[3].append(n_p)
        outs_p[4].append(m_p[:, 0, :N_HEADS])
        outs_s[0].append(s_s)
        outs_s[1].append(proj_s.reshape(ns, SLOT, N_PROJ)[:, SLOT - (CONV_W - 1):, :D_CONV])
        outs_s[2].append(c_s)
        outs_s[3].append(n_s)
        outs_s[4].append(m_s.reshape(ns, SLOT, LANES)[:, SLOT - 1, :N_HEADS])

    y_prompt = xp.reshape(nb, t, D_MODEL)
    y_sample = xs.reshape(ns, SLOT, D_MODEL)[:, pad:, :]
    sp = [jnp.stack(a, axis=0) for a in outs_p]
    ss = [outs_s[j][-1] if j in (0, 2) else jnp.stack(outs_s[j], axis=0) for j in range(5)]
    return (y_prompt, y_sample, *sp, *ss)
```

```python
import functools

import jax
import jax.numpy as jnp
from jax import lax
from jax.experimental import pallas as pl
from jax.experimental.pallas import tpu as pltpu

F32 = jnp.float32
BF16 = jnp.bfloat16

D_MODEL = 1024
N_HEADS = 4
DH = 128
D_CONV = 3 * N_HEADS * DH
D_FF = 2816
D_PLE = 256
CONV_W = 4
CHUNK = 64
EPS = 1e-6
NEG = -1e30
N_GATE = 16
LANES = 128
N_PROJ = D_CONV + 5 * N_HEADS * DH + LANES
GATE_COL_BLOCK = (N_PROJ - LANES) // LANES
SLOT = 8
SLOT_PAD = 4
V7X_VMEM_LIMIT_BYTES = 56 * 1024 * 1024
ROW_TILE = 512
FF_CHUNK = 256
PREP_GROUPS = 2
PREP_SKEW_STAGES = 6


def _cparams(n_axes):
    return pltpu.CompilerParams(dimension_semantics=("arbitrary",) * n_axes,
                                vmem_limit_bytes=V7X_VMEM_LIMIT_BYTES)


def _rms(x, g):
    return x * lax.rsqrt(jnp.mean(x * x, axis=-1, keepdims=True) + EPS) * g


def _softplus(x):
    return jnp.maximum(x, 0.0) + jnp.log1p(jnp.exp(-jnp.abs(x)))


def _dot(a, b):
    return jnp.dot(a, b, preferred_element_type=F32)


def _dot_nt(a, b):
    return lax.dot_general(a, b, (((1,), (1,)), ((), ())), preferred_element_type=F32)


def _split(a):
    hi = a.astype(BF16)
    lo = (a - hi.astype(F32)).astype(BF16)
    return hi, lo


def _split_all(xs):
    return [_split(x) for x in xs]


def _mm3_all(a_list, b_list):
    out = []
    for (ah, al), (bh, bl) in zip(a_list, b_list):
        m = ah.shape[0]
        r = _dot(jnp.concatenate([ah, al], axis=0), bh)
        out.append(r[0:m] + (_dot(ah, bl) + r[m:2 * m]))
    return out


class _PairMasks:
    def __init__(self, seq_len):
        c = CHUNK
        self.r = lax.broadcasted_iota(jnp.int32, (c, 2 * c), 0)
        lane = lax.broadcasted_iota(jnp.int32, (c, 2 * c), 1)
        self.cc = lane & (c - 1)
        self.lo = lane < c
        if seq_len < c:
            sh = seq_len.bit_length() - 1
            same = (self.r >> sh) == (self.cc >> sh)
        else:
            same = self.r >= 0
        self.causal = (self.r >= self.cc) & same
        self.strict_f = jnp.where((self.r > self.cc) & same, 1.0, 0.0)
        self.eye = jnp.where(self.r == self.cc, 1.0, 0.0)
        self.lo_b = jnp.where(self.lo, 1.0, 0.0).astype(BF16)
        self.hi_b = jnp.where(self.lo, 0.0, 1.0).astype(BF16)

    def blockdiag(self, sp):
        return tuple(jnp.concatenate([x * self.lo_b, x * self.hi_b], axis=0) for x in sp)


def _unit_lower_inverse_pairs(l_list, pm, seq_len, out):
    def bd_all(sps):
        return [pm.blockdiag(sp) for sp in sps]

    blk8 = jnp.where((pm.r >> 3) == (pm.cc >> 3), 1.0, 0.0)
    n0 = [l * blk8 for l in l_list]
    n0s = _split_all(n0)
    n2 = _mm3_all(n0s, bd_all(n0s))
    yield
    n2s = _split_all(n2)
    n4 = _mm3_all(n2s, bd_all(n2s))
    p = _mm3_all(_split_all([pm.eye - a for a in n0]), bd_all(_split_all([pm.eye + a for a in n2])))
    yield
    d = _mm3_all(_split_all(p), bd_all(_split_all([pm.eye + a for a in n4])))
    yield
    ds = _split_all(d)
    s = 8
    while s < seq_len:
        sh = s.bit_length() - 1
        off = jnp.where(((pm.r >> (sh + 1)) == (pm.cc >> (sh + 1))) & ((pm.r >> sh) != (pm.cc >> sh)), 1.0, 0.0)
        de = _mm3_all(ds, bd_all(_split_all([l * off for l in l_list])))
        yield
        ded = _mm3_all(_split_all(de), bd_all(ds))
        yield
        d = [a - b for a, b in zip(d, ded)]
        ds = _split_all(d)
        s *= 2
    out.extend(ds)


def _in_proj_kernel(x_ref, g_ref, wt_ref, proj_ref, gt_ref):
    hb = _rms(x_ref[...], g_ref[...]).astype(BF16)
    proj_ref[...] = _dot_nt(hb, wt_ref[...])
    gt = proj_ref[:, N_PROJ - LANES:N_PROJ].T
    for j in range(gt_ref.shape[0]):
        gt_ref[j] = gt[0:N_GATE, CHUNK * j:CHUNK * (j + 1)]


def _layer_spec(layer, *shape):
    zeros = (0,) * len(shape)
    return pl.BlockSpec((None,) + shape, lambda *_: (layer,) + zeros)


def _in_proj(x, g, w, layer):
    rows = x.shape[0]
    tm = ROW_TILE
    return pl.pallas_call(
        _in_proj_kernel,
        grid=(rows // tm,),
        in_specs=[
            pl.BlockSpec((tm, D_MODEL), lambda i: (i, 0)),
            _layer_spec(layer, 1, D_MODEL),
            _layer_spec(layer, N_PROJ, D_MODEL),
        ],
        out_specs=[
            pl.BlockSpec((tm, N_PROJ), lambda i: (i, 0)),
            pl.BlockSpec((tm // CHUNK, N_GATE, CHUNK), lambda i: (i, 0, 0)),
        ],
        out_shape=[jax.ShapeDtypeStruct((rows, N_PROJ), F32),
                   jax.ShapeDtypeStruct((rows // CHUNK, N_GATE, CHUNK), F32)],
        compiler_params=_cparams(1),
        name="in_proj",
    )(x, g, w)


def _chunk_masks(seq_len):
    c = CHUNK
    ri = lax.broadcasted_iota(jnp.int32, (c, c), 0)
    ci = lax.broadcasted_iota(jnp.int32, (c, c), 1)
    if seq_len < c:
        sh = seq_len.bit_length() - 1
        same = (ri >> sh) == (ci >> sh)
    else:
        same = ri >= 0
    return ri, ci, same


def _gdn_prep_kernel(*refs, tb, seq_len, valid_lo, idt):
    nk = tb // CHUNK
    groups = PREP_GROUPS if nk % PREP_GROUPS == 0 else 1
    gens = [_gdn_prep_group(refs[:-5], refs[-5:], g * (nk // groups), nk // groups, seq_len, valid_lo, idt)
            for g in range(groups)]
    _run_skewed(gens, PREP_SKEW_STAGES)


def _run_skewed(gens, skew):
    live = []
    pending = list(gens)
    tick = 0
    while live or pending:
        if pending and tick % skew == 0:
            live.append(pending.pop(0))
        for g in list(live):
            try:
                next(g)
            except StopIteration:
                live.remove(g)
        tick += 1


def _gdn_prep_group(ins, outs, k0, nk, seq_len, valid_lo, idt):
    has_cprev = valid_lo > 0
    if has_cprev:
        u_ref, prev_ref, gcol_ref, grow_ref, cprev_ref, cw_ref, pa_ref, pl_ref, ra_ref, rl_ref = ins
    else:
        u_ref, prev_ref, gcol_ref, grow_ref, cw_ref, pa_ref, pl_ref, ra_ref, rl_ref = ins
    wkqg_ref, wv_ref, qk_ref, kdt_ref, el_ref = outs
    c = CHUNK
    sh = seq_len.bit_length() - 1
    r_lo, nrows = k0 * c, nk * c
    u = u_ref[0, r_lo:r_lo + nrows, :]
    rowpos = lax.broadcasted_iota(jnp.int32, (nrows, 1), 0) & (seq_len - 1)
    if has_cprev:
        u = jnp.where((rowpos >= valid_lo - (CONV_W - 1)) & (rowpos < valid_lo),
                      cprev_ref[0, r_lo:r_lo + nrows, :], u)
    if k0 == 0:
        prev = jnp.where(pl.program_id(1) == 0, 0.0, prev_ref[0])
    else:
        prev = u_ref[0, r_lo - 8:r_lo, :]
    xp = jnp.concatenate([prev, u], axis=0)
    cw = cw_ref[...]
    ys = []
    for part in range(3):
        cs = slice(part * N_HEADS * DH, (part + 1) * N_HEADS * DH)
        y = None
        for i in range(CONV_W):
            s = CONV_W - 1 - i
            xs = u[:, cs] if s == 0 else pltpu.roll(xp[:, cs], s, 0)[8:8 + nrows]
            t = xs * cw[i:i + 1, cs]
            y = t if y is None else y + t
        ys.append(jax.nn.silu(y))
        yield
    yq, yk, yv = ys

    lane = lax.broadcasted_iota(jnp.int32, (1, LANES), 1)
    xg = gcol_ref[0, r_lo:r_lo + nrows, :] + pa_ref[0:1, :]
    gcolv = jnp.where(lane < N_HEADS, jax.nn.sigmoid(xg), -jnp.exp(pl_ref[0:1, :]) * _softplus(xg))
    if valid_lo > 0:
        gcolv = jnp.where(rowpos >= valid_lo, gcolv, 0.0)

    ri, ci, same = _chunk_masks(seq_len)
    causal = (ri >= ci) & same
    upper = (ri <= ci) & same
    colpos = lax.broadcasted_iota(jnp.int32, (1, c), 1) & (seq_len - 1)
    r8 = lax.broadcasted_iota(jnp.int32, (8, c), 0)
    c8 = lax.broadcasted_iota(jnp.int32, (8, c), 1)

    items = []
    for kk in range(nk):
        r0 = kk * c
        k = k0 + kk
        growv = -jnp.exp(rl_ref[:, 0:1]) * _softplus(grow_ref[k] + ra_ref[:, 0:1])
        if valid_lo > 0:
            growv = jnp.where(colpos >= valid_lo, growv, 0.0)
        for h in range(N_HEADS):
            hs = slice(DH * h, DH * (h + 1))
            qc = yq[r0:r0 + c, hs]
            kc = yk[r0:r0 + c, hs]
            g_col = gcolv[r0:r0 + c, N_HEADS + h:N_HEADS + h + 1]
            g_row = growv[N_HEADS + h:N_HEADS + h + 1, :]
            items.append(dict(
                k=k, h=h, hs=hs, qc=qc, kc=kc, vc=yv[r0:r0 + c, hs],
                beta=gcolv[r0:r0 + c, h:h + 1],
                qss=jnp.sum(qc * qc, axis=-1, keepdims=True),
                kss=jnp.sum(kc * kc, axis=-1, keepdims=True),
                gc_col=jnp.sum(jnp.where(causal, g_row, 0.0), axis=1, keepdims=True),
                gc_row=jnp.sum(jnp.where(upper, g_col, 0.0), axis=0, keepdims=True),
                gl_col=jnp.sum(jnp.where(same, g_row, 0.0), axis=1, keepdims=True),
                gl8=jnp.sum(jnp.where((c8 >> sh) == r8, g_row, 0.0), axis=1, keepdims=True)))
    yield
    for it in items:
        k, hs, gc_col, beta = it["k"], it["hs"], it["gc_col"], it["beta"]
        qn = it["qc"] * lax.rsqrt(it["qss"] + EPS) * (DH ** -0.5)
        kn = it["kc"] * lax.rsqrt(it["kss"] + EPS)
        wkqg_ref[0, k, c:2 * c, hs] = (qn * jnp.exp(gc_col)).astype(idt)
        el_ref[0, k, :, hs] = jnp.broadcast_to(jnp.exp(it["gl8"]), (8, DH))
        it["kd"] = kn * jnp.exp(it["gl_col"] - gc_col)
        it["qb"] = qn.astype(BF16)
        it["kb"] = kn.astype(BF16)
        it["rhs"] = jnp.concatenate([beta * it["vc"], (beta * jnp.exp(gc_col)) * kn], axis=1)
    yield
    for a, b in zip(items[0::2], items[1::2]):
        kdt_ref[0, a["k"], a["h"] // 2] = jnp.concatenate([a["kd"], b["kd"]], axis=0).T.astype(idt)
    pm = _PairMasks(seq_len)
    zb = jnp.zeros((c, DH), BF16)
    pairs = []
    for a, b in zip(items[0::2], items[1::2]):
        gcp = jnp.where(pm.lo, a["gc_col"], b["gc_col"])
        grp = jnp.concatenate([a["gc_row"], b["gc_row"]], axis=1)
        decay = jnp.where(pm.causal, jnp.exp(jnp.where(pm.causal, gcp - grp, 0.0)), 0.0)
        kbd = jnp.concatenate([jnp.concatenate([a["kb"], zb], axis=1),
                               jnp.concatenate([zb, b["kb"]], axis=1)], axis=0)
        kq = jnp.concatenate([jnp.concatenate([a["kb"], b["kb"]], axis=1),
                              jnp.concatenate([a["qb"], b["qb"]], axis=1)], axis=0)
        pairs.append(dict(a=a, b=b, decay=decay, kbd=kbd, kq=kq, beta=jnp.where(pm.lo, a["beta"], b["beta"])))
    kqs = [_dot_nt(p["kq"], p["kbd"]) for p in pairs]
    yield
    for p, kq in zip(pairs, kqs):
        qk_ref[0, p["a"]["k"], p["a"]["h"] // 2] = (kq[c:2 * c] * p["decay"]).astype(idt)
    ls = [p["beta"] * kq[0:c] * p["decay"] * pm.strict_f for p, kq in zip(pairs, kqs)]
    tinvs = []
    yield from _unit_lower_inverse_pairs(ls, pm, seq_len, tinvs)
    for p, (th, tl) in zip(pairs, tinvs):
        rh, rl_ = _split(jnp.concatenate([p["a"]["rhs"], p["b"]["rhs"]], axis=0))
        ta_h, tb_h = th * pm.lo_b, th * pm.hi_b
        r1 = _dot(jnp.concatenate([ta_h, tl * pm.lo_b, tb_h, tl * pm.hi_b], axis=0), rh)
        r2 = _dot(jnp.concatenate([ta_h, tb_h], axis=0), rl_)
        p["sol"] = (r1[0:c] + (r2[0:c] + r1[c:2 * c]), r1[2 * c:3 * c] + (r2[c:2 * c] + r1[3 * c:4 * c]))
    yield
    for p in pairs:
        for it, sol in zip((p["a"], p["b"]), p["sol"]):
            wv_ref[0, it["k"], :, it["hs"]] = sol[:, 0:DH]
            wkqg_ref[0, it["k"], 0:c, it["hs"]] = sol[:, DH:2 * DH].astype(idt)


def _gdn_prep(proj3, grow, cprev, cw, pa, pl_, ra, rl, layer, *, seq_len, valid_lo, idt, tb):
    nb, t, _ = proj3.shape
    c = CHUNK
    nt = t // tb
    kb = tb // c
    has_cprev = valid_lo > 0
    in_specs = [
        pl.BlockSpec((1, tb, D_CONV), lambda b, i: (b, i, 0)),
        pl.BlockSpec((1, 8, D_CONV), lambda b, i: (b, jnp.maximum(i * (tb // 8) - 1, 0), 0)),
        pl.BlockSpec((1, tb, LANES), lambda b, i: (b, i, GATE_COL_BLOCK)),
        pl.BlockSpec((kb, N_GATE, c), lambda b, i: (b * nt + i, 0, 0)),
    ]
    args = [proj3, proj3, proj3, grow]
    if has_cprev:
        assert nb == 1
        in_specs.append(pl.BlockSpec((1, tb, D_CONV), lambda b, i: (layer, i, 0)))
        args.append(cprev)
    in_specs += [
        _layer_spec(layer, CONV_W, D_CONV),
        _layer_spec(layer, 8, LANES),
        _layer_spec(layer, 8, LANES),
        _layer_spec(layer, N_GATE, LANES),
        _layer_spec(layer, N_GATE, LANES),
    ]
    args += [cw, pa, pl_, ra, rl]
    n = t // c
    hd = N_HEADS * DH
    out_shape = [
        jax.ShapeDtypeStruct((nb, n, 2 * c, hd), idt),
        jax.ShapeDtypeStruct((nb, n, c, hd), F32),
        jax.ShapeDtypeStruct((nb, n, N_HEADS // 2, c, 2 * c), idt),
        jax.ShapeDtypeStruct((nb, n, N_HEADS // 2, DH, 2 * c), idt),
        jax.ShapeDtypeStruct((nb, n, 8, hd), F32),
    ]
    out_specs = [
        pl.BlockSpec((1, kb, 2 * c, hd), lambda b, i: (b, i, 0, 0)),
        pl.BlockSpec((1, kb, c, hd), lambda b, i: (b, i, 0, 0)),
        pl.BlockSpec((1, kb, N_HEADS // 2, c, 2 * c), lambda b, i: (b, i, 0, 0, 0)),
        pl.BlockSpec((1, kb, N_HEADS // 2, DH, 2 * c), lambda b, i: (b, i, 0, 0, 0)),
        pl.BlockSpec((1, kb, 8, hd), lambda b, i: (b, i, 0, 0)),
    ]
    return pl.pallas_call(
        functools.partial(_gdn_prep_kernel, tb=tb, seq_len=seq_len, valid_lo=valid_lo, idt=idt),
        grid=(nb, nt),
        in_specs=in_specs,
        out_specs=out_specs,
        out_shape=out_shape,
        compiler_params=_cparams(2),
        name="gdn_prep",
    )(*args)


def _lane_half_masks(dtype):
    lo = lax.broadcasted_iota(jnp.int32, (CHUNK, 2 * CHUNK), 1) < CHUNK
    return jnp.where(lo, 1.0, 0.0).astype(dtype), jnp.where(lo, 0.0, 1.0).astype(dtype)


def _gated_norm_store(o, z, gn, mix_ref, idx):
    mix_ref[idx] = (_rms(o, gn) * jax.nn.silu(z)).astype(BF16)


def _gdn_scan_prompt_kernel(wkqg_ref, wv_ref, qk_ref, kdt_ref, el_ref, z_ref, gn_ref,
                            mix_ref, sout_ref, s_scr, *, nc, bs):
    c = CHUNK

    @pl.when(pl.program_id(1) == 0)
    def _():
        s_scr[...] = jnp.zeros_like(s_scr)

    gn = gn_ref[...]
    units = [(b, h, slice(DH * h, DH * (h + 1))) for b in range(bs) for h in range(N_HEADS)]
    half_b = _lane_half_masks(BF16)
    half2_b = tuple(jnp.concatenate([m, m], axis=0) for m in half_b)

    def body(n, carry):
        rows = pl.ds(pl.multiple_of(n * c, c), c)
        ss = [s_scr[b * N_HEADS + h] for b, h, _ in units]
        aa = [_dot(wkqg_ref[b, n, :, hs], s.astype(BF16)) for (b, _, hs), s in zip(units, ss)]
        ubs = [(wv_ref[b, n, :, hs] - a[0:c]).astype(BF16) for (b, _, hs), a in zip(units, aa)]
        os_, ds = [], []
        for j in range(len(units) // 2):
            b, h, _ = units[2 * j]
            qkp, kdp = qk_ref[b, n, h // 2], kdt_ref[b, n, h // 2]
            r = _dot(jnp.concatenate([qkp * half_b[0], qkp * half_b[1], kdp * half2_b[0], kdp * half2_b[1]],
                                     axis=0),
                     jnp.concatenate(ubs[2 * j:2 * j + 2], axis=0))
            os_ += [r[0:c], r[c:2 * c]]
            ds += [r[2 * c:2 * c + DH], r[2 * c + DH:2 * c + 2 * DH]]
        for i, (b, h, hs) in enumerate(units):
            s_scr[b * N_HEADS + h] = el_ref[b, n, 0:1, hs] * ss[i] + ds[i]
            _gated_norm_store(aa[i][c:2 * c] + os_[i], z_ref[b, rows, hs], gn, mix_ref, (b, rows, hs))
        return carry

    lax.fori_loop(0, nc, body, 0)

    @pl.when(pl.program_id(1) == pl.num_programs(1) - 1)
    def _():
        for b in range(bs):
            sout_ref[b] = s_scr[b * N_HEADS:(b + 1) * N_HEADS]


def _gdn_scan_prompt(wkqg, wv, qk, kdt, el, proj3, gn, layer, *, ts, bs):
    nb, n, _, hd = wkqg.shape
    c = CHUNK
    t = n * c
    nc = ts // c
    zblk = D_CONV // hd
    return pl.pallas_call(
        functools.partial(_gdn_scan_prompt_kernel, nc=nc, bs=bs),
        grid=(nb // bs, t // ts),
        in_specs=[
            pl.BlockSpec((bs, nc, 2 * c, hd), lambda b, i: (b, i, 0, 0)),
            pl.BlockSpec((bs, nc, c, hd), lambda b, i: (b, i, 0, 0)),
            pl.BlockSpec((bs, nc, N_HEADS // 2, c, 2 * c), lambda b, i: (b, i, 0, 0, 0)),
            pl.BlockSpec((bs, nc, N_HEADS // 2, DH, 2 * c), lambda b, i: (b, i, 0, 0, 0)),
            pl.BlockSpec((bs, nc, 8, hd), lambda b, i: (b, i, 0, 0)),
            pl.BlockSpec((bs, ts, hd), lambda b, i: (b, i, zblk)),
            _layer_spec(layer, 1, DH),
        ],
        out_specs=[
            pl.BlockSpec((bs, ts, hd), lambda b, i: (b, i, 0)),
            pl.BlockSpec((bs, N_HEADS, DH, DH), lambda b, i: (b, 0, 0, 0)),
        ],
        out_shape=[jax.ShapeDtypeStruct((nb, t, hd), BF16),
                   jax.ShapeDtypeStruct((nb, N_HEADS, DH, DH), F32)],
        scratch_shapes=[pltpu.VMEM((bs * N_HEADS, DH, DH), F32)],
        compiler_params=_cparams(2),
        name="gdn_scan_prompt",
    )(wkqg, wv, qk, kdt, el, proj3, gn)


def _gdn_scan_decode_kernel(*refs, n_prev):
    (wkqg_ref, wv_ref, qk_ref, kdt_ref, el_ref, z_ref, gn_ref, s0_ref), rest = refs[:8], refs[8:]
    prev_refs, (mix_ref, sout_ref) = rest[:n_prev], rest[n_prev:]
    if n_prev:
        for l, p_ref in enumerate(prev_refs):
            sout_ref[l] = p_ref[...]
        sout_ref = sout_ref.at[n_prev]
    c = CHUNK
    nseq = c // SLOT
    rowseq = lax.broadcasted_iota(jnp.int32, (c, 1), 0) >> 3
    gn = gn_ref[...]
    heads = [slice(DH * h, DH * (h + 1)) for h in range(N_HEADS)]
    aa = []
    for h, hs in enumerate(heads):
        w = wkqg_ref[0, 0, :, hs]
        row = []
        for j in range(nseq):
            wj = jnp.concatenate([w[SLOT * j:SLOT * (j + 1)], w[c + SLOT * j:c + SLOT * (j + 1)]],
                                 axis=0).astype(BF16)
            row.append(_dot(wj, s0_ref[j, h].astype(BF16)))
        aa.append(row)
    us = [wv_ref[0, 0, :, hs] - jnp.concatenate([a[0:SLOT] for a in aa[h]], axis=0)
          for h, hs in enumerate(heads)]
    half_f = _lane_half_masks(F32)
    os_ = []
    for j in range(N_HEADS // 2):
        qkp = qk_ref[0, 0, j]
        o2 = _dot(jnp.concatenate([qkp * half_f[0], qkp * half_f[1]], axis=0).astype(BF16),
                  jnp.concatenate(us[2 * j:2 * j + 2], axis=0).astype(BF16))
        os_ += [o2[0:c], o2[c:2 * c]]
    half2_f = tuple(jnp.concatenate([m, m], axis=0) for m in half_f)
    for p in range(N_HEADS // 2):
        kdp = kdt_ref[0, 0, p]
        kd2 = jnp.concatenate([kdp * half2_f[0], kdp * half2_f[1]], axis=0).astype(BF16)
        for j in range(nseq):
            uj = jnp.concatenate([jnp.where(rowseq == j, us[2 * p + i], 0.0) for i in range(2)], axis=0)
            r = _dot(kd2, uj.astype(BF16))
            for i in range(2):
                h = 2 * p + i
                sout_ref[j, h] = (el_ref[0, 0, j:j + 1, heads[h]] * s0_ref[j, h] + r[DH * i:DH * (i + 1)])
    for h, hs in enumerate(heads):
        o = jnp.concatenate([a[SLOT:2 * SLOT] for a in aa[h]], axis=0) + os_[h]
        _gated_norm_store(o, z_ref[:, hs], gn, mix_ref, (slice(None), hs))


def _state_specs(layer, prev, state_shape, nseq):
    tail = state_shape[2:]
    zeros = (0,) * len(tail)
    in_specs = [pl.BlockSpec((None, nseq) + tail, lambda i: (layer, i) + zeros)]
    in_specs += [pl.BlockSpec((nseq,) + tail, lambda i: (i,) + zeros) for _ in prev]
    if prev:
        out_spec = pl.BlockSpec((len(prev) + 1, nseq) + tail, lambda i: (0, i) + zeros)
        out_shape = jax.ShapeDtypeStruct((len(prev) + 1,) + state_shape[1:], F32)
    else:
        out_spec = pl.BlockSpec((nseq,) + tail, lambda i: (i,) + zeros)
        out_shape = jax.ShapeDtypeStruct(state_shape[1:], F32)
    return in_specs, out_spec, out_shape


def _gdn_scan_decode(wkqg, wv, qk, kdt, el, proj, gn, s0_all, prev, layer):
    _, n, _, hd = wkqg.shape
    c = CHUNK
    nseq = c // SLOT
    zblk = D_CONV // hd
    st_in, st_out, st_shape = _state_specs(layer, prev, s0_all.shape, nseq)
    return pl.pallas_call(
        functools.partial(_gdn_scan_decode_kernel, n_prev=len(prev)),
        grid=(n,),
        in_specs=[
            pl.BlockSpec((1, 1, 2 * c, hd), lambda i: (0, i, 0, 0)),
            pl.BlockSpec((1, 1, c, hd), lambda i: (0, i, 0, 0)),
            pl.BlockSpec((1, 1, N_HEADS // 2, c, 2 * c), lambda i: (0, i, 0, 0, 0)),
            pl.BlockSpec((1, 1, N_HEADS // 2, DH, 2 * c), lambda i: (0, i, 0, 0, 0)),
            pl.BlockSpec((1, 1, 8, hd), lambda i: (0, i, 0, 0)),
            pl.BlockSpec((c, hd), lambda i: (i, zblk)),
            _layer_spec(layer, 1, DH),
        ] + st_in,
        out_specs=[pl.BlockSpec((c, hd), lambda i: (i, 0)), st_out],
        out_shape=[jax.ShapeDtypeStruct((n * c, hd), BF16), st_shape],
        compiler_params=_cparams(1),
        name="gdn_scan_decode",
    )(wkqg, wv, qk, kdt, el, proj, gn, s0_all, *prev)


def _mlstm_gates(gcol, growk, pa_ref, ra_ref, h, rows, valid_lo, rowpos, colpos):
    xg = gcol + pa_ref[0:1, :]
    ig_col = xg[:, 2 * N_HEADS + h:2 * N_HEADS + h + 1]
    fl_col = -_softplus(-xg[:, 3 * N_HEADS + h:3 * N_HEADS + h + 1])
    xr = growk + ra_ref[:, 0:1]
    ig_row = xr[2 * N_HEADS + h:2 * N_HEADS + h + 1, :]
    fl_row = -_softplus(-xr[3 * N_HEADS + h:3 * N_HEADS + h + 1, :])
    if valid_lo > 0:
        ig_col = jnp.where(rowpos >= valid_lo, ig_col, NEG)
        fl_col = jnp.where(rowpos >= valid_lo, fl_col, 0.0)
        ig_row = jnp.where(colpos >= valid_lo, ig_row, NEG)
        fl_row = jnp.where(colpos >= valid_lo, fl_row, 0.0)
    return ig_col, fl_col, ig_row, fl_row


def _mlstm_chunks(probs, seq_len, finish):
    c = CHUNK
    nseq = c // seq_len
    ri, ci, same = _chunk_masks(seq_len)
    causal = (ri >= ci) & same
    upper = (ri <= ci) & same
    seq_end = ci == (ri | (seq_len - 1))
    rowseq = lax.broadcasted_iota(jnp.int32, (c, 1), 0) >> (seq_len.bit_length() - 1)
    zpad = jnp.zeros((16 - seq_len, DH), F32) if nseq > 1 else None
    ones_b = jnp.ones((c, LANES), BF16)
    for p in probs:
        p["f_row"] = jnp.sum(jnp.where(upper, p["fl_col"], 0.0), axis=0, keepdims=True)
        p["km"] = p["k"] * (DH ** -0.5)
        p["qb"] = p["q"].astype(BF16)
        p["kb"] = p["km"].astype(BF16)
        p["vb"] = p["v"].astype(BF16)
        if nseq == 1:
            r = _dot(jnp.concatenate(_split(jnp.where(causal, p["fl_row"], 0.0)), axis=0), ones_b)
            p["f_col"] = r[0:c] + r[c:2 * c]
            nb2 = p["n_rows"].astype(BF16)
            p["qn"] = _dot_nt(p["qb"], jnp.concatenate([nb2, nb2], axis=0))
            p["ig_col"] = jnp.broadcast_to(p["ig_col"], (c, LANES))
        else:
            p["f_col"] = jnp.sum(jnp.where(causal, p["fl_row"], 0.0), axis=1, keepdims=True)
            p["qn"] = jnp.sum(p["q"] * p["n_rows"], axis=1, keepdims=True)
    yield
    for p in probs:
        p["b_col"] = p["ig_col"] - p["f_col"]
        p["b_row"] = p["ig_row"] - p["f_row"]
        p["bmax_col"] = jnp.max(jnp.where(causal, p["b_row"], -jnp.inf), axis=1, keepdims=True)
        if nseq == 1:
            p["bmax_col"] = jnp.broadcast_to(p["bmax_col"], (c, LANES))
        if nseq > 1:
            p["fl_end"] = jnp.sum(jnp.where(seq_end, p["f_row"], 0.0), axis=1, keepdims=True)
            p["bmax_end"] = jnp.max(jnp.where(same, p["b_row"], -jnp.inf), axis=1, keepdims=True)
    yield
    for p in probs:
        f_col, mp_col = p["f_col"], p["mp_col"]
        m_col = f_col + jnp.maximum(mp_col, p["bmax_col"])
        p["m_col"] = m_col
        p["a_col"] = jnp.exp(f_col + mp_col - m_col)
        fm = (f_col - m_col)[:, 0:c] if nseq == 1 else f_col - m_col
        p["dexp"] = jnp.where(causal, jnp.exp(jnp.where(causal, fm + p["b_row"], 0.0)), 0.0)
        if nseq == 1:
            fl_end, ml, mp = f_col[c - 1:c], m_col[c - 1:c], mp_col[c - 1:c]
            p["ml_col"] = ml
            p["al"] = jnp.exp(fl_end + mp - ml)
        else:
            fl_end, mp = p["fl_end"], mp_col
            ml = fl_end + jnp.maximum(mp_col, p["bmax_end"])
            p["ml_col"] = ml
            p["al"] = jnp.broadcast_to(jnp.exp(fl_end + mp - ml), (c, DH))
        p["kw"] = p["km"] * jnp.exp(fl_end + p["b_col"] - ml)
    yield
    for p in probs:
        p["kwt"] = p["kw"].T.astype(BF16)
    for p in probs:
        p["qk"] = _dot_nt(p["qb"], p["kb"])
    yield
    for p in probs:
        if nseq == 1:
            p["qc"] = _dot(p["qb"], p["c_list"][0].astype(BF16))
            p["upd"] = [_dot(p["kwt"], p["vb"])]
        else:
            parts, upd = [], []
            for j in range(nseq):
                qj = jnp.concatenate([p["q"][seq_len * j:seq_len * (j + 1)], zpad], axis=0).astype(BF16)
                parts.append(_dot(qj, p["c_list"][j].astype(BF16))[0:seq_len])
                upd.append(_dot(p["kwt"], jnp.where(rowseq == j, p["v"], 0.0).astype(BF16)))
            p["qc"] = jnp.concatenate(parts, axis=0)
            p["upd"] = upd
    yield
    if nseq == 1:
        for p in probs:
            p["w_hl"] = _split(p["dexp"] * p["qk"])
        yield
        for p in probs:
            r = _dot(jnp.concatenate(p["w_hl"], axis=0), jnp.concatenate([p["vb"], ones_b], axis=1))
            p["wv"] = r[0:c, 0:DH]
            p["wsum"] = r[0:c, DH:DH + LANES] + r[c:2 * c, DH:DH + LANES]
    else:
        for p in probs:
            p["w"] = p["dexp"] * p["qk"]
            p["wsum"] = jnp.sum(p["w"], axis=1, keepdims=True)
        for p in probs:
            p["wv"] = _dot(p["w"].astype(BF16), p["vb"])
    yield
    for p in probs:
        num = p["a_col"] * p["qc"] + p["wv"]
        den = p["a_col"] * p["qn"] + p["wsum"]
        p["hout"] = num / jnp.maximum(jnp.abs(den), jnp.exp(-p["m_col"]))
        p["new_c"] = [p["al"][seq_len * j:seq_len * j + 1] * p["c_list"][j] + p["upd"][j]
                      for j in range(nseq)]
    yield
    finish(probs)


def _mlstm_out_store(hout, og, gnorm, mix_ref, idx):
    x = hout * jax.nn.sigmoid(og)
    ssq = _dot(jnp.concatenate(_split(x * x), axis=0), jnp.ones((DH, DH), BF16))
    ms = (ssq[0:CHUNK] + ssq[CHUNK:2 * CHUNK]) * (1.0 / DH)
    mix_ref[idx] = (x * lax.rsqrt(ms + EPS) * gnorm).astype(BF16)


def _mlstm_prompt_kernel(q_ref, k_ref, v_ref, o_ref, gcol_ref, grow_ref, pa_ref, ra_ref, nrm_ref,
                         mix_ref, cout_ref, nout_ref, mout_ref, c_scr, n_scr, m_scr, *, nc, bs):
    c = CHUNK

    @pl.when(pl.program_id(1) == 0)
    def _():
        c_scr[...] = jnp.zeros_like(c_scr)
        n_scr[...] = jnp.zeros_like(n_scr)
        m_scr[...] = jnp.zeros_like(m_scr)

    def body(n, carry):
        r0 = pl.multiple_of(n * c, c)
        rows = pl.ds(r0, c)
        def finish(probs):
            for p in probs:
                st = p["st"]
                c_scr[st] = p["new_c"][0]
                n_scr[st] = jnp.broadcast_to(
                    p["al"][0:1] * p["n_rows"][0:1] + jnp.sum(p["kw"], axis=0, keepdims=True), (c, DH))
                m_scr[st] = jnp.broadcast_to(p["ml_col"], (c, DH))
                _mlstm_out_store(p["hout"], o_ref[p["b"], rows, p["hs"]], nrm_ref[:, p["hs"]], mix_ref,
                                 (p["b"], rows, p["hs"]))

        probs = []
        for b in range(bs):
            gcol = gcol_ref[b, rows, :]
            growk = grow_ref[b, n]
            for h in range(N_HEADS):
                hs = slice(DH * h, DH * (h + 1))
                ig_col, fl_col, ig_row, fl_row = _mlstm_gates(gcol, growk, pa_ref, ra_ref, h, c, 0, None, None)
                st = b * N_HEADS + h
                probs.append(dict(b=b, hs=hs, st=st, q=q_ref[b, rows, hs], k=k_ref[b, rows, hs],
                                  v=v_ref[b, rows, hs], ig_col=ig_col, fl_col=fl_col, ig_row=ig_row,
                                  fl_row=fl_row, mp_col=m_scr[st], n_rows=n_scr[st],
                                  c_list=[c_scr[st]]))
        _run_skewed([_mlstm_chunks(probs, c, finish)], 1)
        return carry

    lax.fori_loop(0, nc, body, 0)

    @pl.when(pl.program_id(1) == pl.num_programs(1) - 1)
    def _():
        lane = lax.broadcasted_iota(jnp.int32, (8, DH), 1)
        for b in range(bs):
            mo = jnp.zeros((8, DH), F32)
            for h in range(N_HEADS):
                st = b * N_HEADS + h
                cout_ref[b, h] = c_scr[st]
                nout_ref[b, h:h + 1, :] = n_scr[st][0:1]
                mo = jnp.where(lane == h, m_scr[st][0:8], mo)
            mout_ref[b] = mo


def _mlstm_prompt(proj3, grow4, pa, ra, nrm, layer, *, ts, bs):
    nb, t, _ = proj3.shape
    c = CHUNK
    hd = N_HEADS * DH
    nc = ts // c
    qblk = (D_CONV + hd) // hd
    return pl.pallas_call(
        functools.partial(_mlstm_prompt_kernel, nc=nc, bs=bs),
        grid=(nb // bs, t // ts),
        in_specs=[
            pl.BlockSpec((bs, ts, hd), lambda b, i: (b, i, qblk)),
            pl.BlockSpec((bs, ts, hd), lambda b, i: (b, i, qblk + 1)),
            pl.BlockSpec((bs, ts, hd), lambda b, i: (b, i, qblk + 2)),
            pl.BlockSpec((bs, ts, hd), lambda b, i: (b, i, qblk + 3)),
            pl.BlockSpec((bs, ts, LANES), lambda b, i: (b, i, GATE_COL_BLOCK)),
            pl.BlockSpec((bs, nc, N_GATE, c), lambda b, i: (b, i, 0, 0)),
            _layer_spec(layer, 8, LANES),
            _layer_spec(layer, N_GATE, LANES),
            _layer_spec(layer, 1, hd),
        ],
        out_specs=[
            pl.BlockSpec((bs, ts, hd), lambda b, i: (b, i, 0)),
            pl.BlockSpec((bs, N_HEADS, DH, DH), lambda b, i: (b, 0, 0, 0)),
            pl.BlockSpec((bs, N_HEADS, DH), lambda b, i: (b, 0, 0)),
            pl.BlockSpec((bs, 8, DH), lambda b, i: (b, 0, 0)),
        ],
        out_shape=[jax.ShapeDtypeStruct((nb, t, hd), BF16),
                   jax.ShapeDtypeStruct((nb, N_HEADS, DH, DH), F32),
                   jax.ShapeDtypeStruct((nb, N_HEADS, DH), F32),
                   jax.ShapeDtypeStruct((nb, 8, DH), F32)],
        scratch_shapes=[pltpu.VMEM((bs * N_HEADS, DH, DH), F32),
                        pltpu.VMEM((bs * N_HEADS, c, DH), F32),
                        pltpu.VMEM((bs * N_HEADS, c, DH), F32)],
        compiler_params=_cparams(2),
        name="mlstm_prompt",
    )(proj3, proj3, proj3, proj3, proj3, grow4, pa, ra, nrm)


def _mlstm_decode_kernel(*refs, n_prev):
    (q_ref, k_ref, v_ref, o_ref, gcol_ref, grow_ref, pa_ref, ra_ref, nrm_ref, n0_ref, m0_ref,
     c0_ref), rest = refs[:12], refs[12:]
    prev_refs, (mix_ref, nout_ref, mout_ref, cout_ref) = rest[:n_prev], rest[n_prev:]
    if n_prev:
        for l, p_ref in enumerate(prev_refs):
            cout_ref[l] = p_ref[...]
        cout_ref = cout_ref.at[n_prev]
    c = CHUNK
    nseq = c // SLOT
    nck = q_ref.shape[0] // c
    rowpos = lax.broadcasted_iota(jnp.int32, (c, 1), 0) & (SLOT - 1)
    colpos = lax.broadcasted_iota(jnp.int32, (1, c), 1) & (SLOT - 1)
    lane = lax.broadcasted_iota(jnp.int32, (c, DH), 1)
    probs = []
    for ck in range(nck):
        rows = slice(c * ck, c * (ck + 1))
        gcol = gcol_ref[rows, :]
        growk = grow_ref[ck]
        for h in range(N_HEADS):
            hs = slice(DH * h, DH * (h + 1))
            ig_col, fl_col, ig_row, fl_row = _mlstm_gates(gcol, growk, pa_ref, ra_ref, h, c, SLOT_PAD, rowpos, colpos)
            seqs = [nseq * ck + j for j in range(nseq)]
            n_rows = jnp.concatenate(
                [jnp.broadcast_to(n0_ref[s, h:h + 1, :], (SLOT, DH)) for s in seqs], axis=0)
            probs.append(dict(ck=ck, rows=rows, seqs=seqs, h=h, hs=hs, q=q_ref[rows, hs], k=k_ref[rows, hs],
                              v=v_ref[rows, hs], ig_col=ig_col, fl_col=fl_col, ig_row=ig_row, fl_row=fl_row,
                              mp_col=m0_ref[rows, h:h + 1], n_rows=n_rows,
                              c_list=[c0_ref[s, h] for s in seqs]))

    def finish(probs):
        mos = [jnp.zeros((c, DH), F32) for _ in range(nck)]
        for p in probs:
            h, hs, rows = p["h"], p["hs"], p["rows"]
            for j, s in enumerate(p["seqs"]):
                cout_ref[s, h] = p["new_c"][j]
                rs = slice(SLOT * j, SLOT * (j + 1))
                nout_ref[s, h:h + 1, :] = (p["al"][SLOT * j:SLOT * j + 1] * n0_ref[s, h:h + 1, :]
                                           + jnp.sum(p["kw"][rs], axis=0, keepdims=True))
            mos[p["ck"]] = jnp.where(lane == h, jnp.broadcast_to(p["ml_col"], (c, DH)), mos[p["ck"]])
            _mlstm_out_store(p["hout"], o_ref[rows, hs], nrm_ref[:, hs], mix_ref, (rows, hs))
        for ck in range(nck):
            mout_ref[c * ck:c * (ck + 1), :] = mos[ck]

    _run_skewed([_mlstm_chunks(probs, SLOT, finish)], 1)


def _mlstm_decode(proj, grow, pa, ra, nrm, c0_all, n0_all, m0rows_all, prev, layer, *, nck):
    rows = proj.shape[0]
    c = CHUNK * nck
    hd = N_HEADS * DH
    nseq = c // SLOT
    qblk = (D_CONV + hd) // hd
    st_in, st_out, st_shape = _state_specs(layer, prev, c0_all.shape, nseq)
    return pl.pallas_call(
        functools.partial(_mlstm_decode_kernel, n_prev=len(prev)),
        grid=(rows // c,),
        in_specs=[
            pl.BlockSpec((c, hd), lambda i: (i, qblk)),
            pl.BlockSpec((c, hd), lambda i: (i, qblk + 1)),
            pl.BlockSpec((c, hd), lambda i: (i, qblk + 2)),
            pl.BlockSpec((c, hd), lambda i: (i, qblk + 3)),
            pl.BlockSpec((c, LANES), lambda i: (i, GATE_COL_BLOCK)),
            pl.BlockSpec((nck, N_GATE, CHUNK), lambda i: (i, 0, 0)),
            _layer_spec(layer, 8, LANES),
            _layer_spec(layer, N_GATE, LANES),
            _layer_spec(layer, 1, hd),
            pl.BlockSpec((None, nseq, N_HEADS, DH), lambda i: (layer, i, 0, 0)),
            pl.BlockSpec((None, c, LANES), lambda i: (layer, i, 0)),
        ] + st_in,
        out_specs=[
            pl.BlockSpec((c, hd), lambda i: (i, 0)),
            pl.BlockSpec((nseq, N_HEADS, DH), lambda i: (i, 0, 0)),
            pl.BlockSpec((c, LANES), lambda i: (i, 0)),
            st_out,
        ],
        out_shape=[jax.ShapeDtypeStruct((rows, hd), BF16),
                   jax.ShapeDtypeStruct(n0_all.shape[1:], F32),
                   jax.ShapeDtypeStruct((rows, LANES), F32),
                   st_shape],
        compiler_params=_cparams(1),
        name="mlstm_decode",
    )(proj, proj, proj, proj, proj, grow, pa, ra, nrm, n0_all, m0rows_all, c0_all, *prev)


def _post_kernel(x_ref, ma_ref, mb_ref, p_ref, woa_ref, wob_ref, gf_ref, wg_ref, wu_ref, wd_ref,
                 gp_ref, wpg_ref, wpp_ref, gfin_ref, o_ref, acc_ref, *, final):
    x = x_ref[...] + (_dot(ma_ref[...], woa_ref[...]) + _dot(mb_ref[...], wob_ref[...]))
    ub = _rms(x, gf_ref[...]).astype(BF16)
    for j in range(D_FF // FF_CHUNK):
        sl = slice(j * FF_CHUNK, (j + 1) * FF_CHUNK)
        a = (jax.nn.silu(_dot(ub, wg_ref[:, sl])) * _dot(ub, wu_ref[:, sl])).astype(BF16)
        d = _dot(a, wd_ref[sl, :])
        if j == 0:
            acc_ref[...] = d
        else:
            acc_ref[...] += d
    x = x + acc_ref[...]
    gate = jax.nn.sigmoid(_dot(_rms(x, gp_ref[...]).astype(BF16), wpg_ref[...]))
    x = x + _dot(p_ref[...].astype(BF16), wpp_ref[...]) * gate
    if final:
        x = _rms(x, gfin_ref[...])
    o_ref[...] = x


def _post(x, ma, mb, p_all, wo, gf, wg, wu, wd, gp, wpg, wpp, gfin, layer, *, final):
    rows = x.shape[0]
    tm = ROW_TILE
    hd = N_HEADS * DH
    row = lambda w: pl.BlockSpec((tm, w), lambda i: (i, 0))

    def whole(a, b, blk=0):
        return pl.BlockSpec((None, a, b), lambda i: (layer, blk, 0), pipeline_mode=pl.Buffered(1))

    return pl.pallas_call(
        functools.partial(_post_kernel, final=final),
        grid=(rows // tm,),
        in_specs=[row(D_MODEL), row(hd), row(hd),
                  pl.BlockSpec((None, tm, D_PLE), lambda i: (layer, i, 0)),
                  whole(hd, D_MODEL, 0), whole(hd, D_MODEL, 1), whole(1, D_MODEL),
                  whole(D_MODEL, D_FF), whole(D_MODEL, D_FF), whole(D_FF, D_MODEL),
                  whole(1, D_MODEL), whole(D_MODEL, D_MODEL), whole(D_PLE, D_MODEL),
                  pl.BlockSpec((1, D_MODEL), lambda i: (0, 0))],
        out_specs=row(D_MODEL),
        out_shape=jax.ShapeDtypeStruct((rows, D_MODEL), F32),
        scratch_shapes=[pltpu.VMEM((tm, D_MODEL), F32)],
        compiler_params=_cparams(1),
        name="post",
    )(x, ma, mb, p_all, wo, wo, gf, wg, wu, wd, gp, wpg, wpp, gfin)


def _gate_vec(pairs, depth):
    v = jnp.zeros((depth, N_GATE), F32)
    for off, val in pairs:
        v = v.at[:, off:off + N_HEADS].set(val.astype(F32))
    return v


def _lane_form(v):
    d = v.shape[0]
    return jnp.broadcast_to(jnp.pad(v, ((0, 0), (0, LANES - N_GATE)))[:, None, :], (d, 8, LANES))


def _row_form(v):
    d = v.shape[0]
    return jnp.broadcast_to(v[:, :, None], (d, N_GATE, LANES))


def kernel(x_prompt, x_sample, p_prompt, p_sample, state_gdn, state_gdn_conv, state_mlstm_C, state_mlstm_n, state_mlstm_m, w_in, conv_w, gdn_a_log, gdn_dt_bias, gdn_norm, mlstm_i_bias, mlstm_f_bias, mlstm_norm, w_out, norm_mix, norm_ffn, w_gate, w_up, w_down, norm_ple, w_ple_gate, w_ple_proj, norm_final):
    depth = w_in.shape[0]
    nb, t, _ = x_prompt.shape
    ns, tdec, _ = x_sample.shape
    hd = N_HEADS * DH
    c = CHUNK
    pad = SLOT - tdec

    xp = x_prompt.reshape(nb * t, D_MODEL)
    xs = jnp.pad(x_sample, ((0, 0), (pad, 0), (0, 0))).reshape(ns * SLOT, D_MODEL)
    ps_all = jnp.pad(p_sample, ((0, 0), (0, 0), (pad, 0), (0, 0))).reshape(depth, ns * SLOT, D_PLE)
    pp_all = p_prompt.reshape(depth, nb * t, D_PLE)
    gfin = norm_final.reshape(1, D_MODEL)

    o = D_CONV + hd
    wt = jnp.swapaxes(w_in, 1, 2).astype(BF16)
    w_r = jnp.concatenate([wt[:, :o], wt[:, o + 2 * N_HEADS:o + 2 * N_HEADS + 4 * hd], wt[:, o:o + 2 * N_HEADS],
                           wt[:, o + 2 * N_HEADS + 4 * hd:],
                           jnp.zeros((depth, LANES - N_GATE, D_MODEL), BF16)], axis=1)
    adds = _gate_vec([(N_HEADS, gdn_dt_bias), (2 * N_HEADS, mlstm_i_bias), (3 * N_HEADS, mlstm_f_bias)], depth)
    alog = _gate_vec([(N_HEADS, gdn_a_log)], depth)
    pa, pl_, ra, rl = _lane_form(adds), _lane_form(alog), _row_form(adds), _row_form(alog)
    g_mix = norm_mix.reshape(depth, 1, D_MODEL)
    cw = conv_w.astype(F32)
    gn = gdn_norm.reshape(depth, 1, DH).astype(F32)
    nrm = mlstm_norm.reshape(depth, 1, hd).astype(F32)
    wts = (w_out.astype(BF16), norm_ffn.reshape(depth, 1, D_MODEL), w_gate.astype(BF16), w_up.astype(BF16),
           w_down.astype(BF16), norm_ple.reshape(depth, 1, D_MODEL), w_ple_gate.astype(BF16),
           w_ple_proj.astype(BF16), gfin)
    cprev = jnp.pad(state_gdn_conv.astype(F32), ((0, 0), (0, 0), (pad - (CONV_W - 1), SLOT - pad), (0, 0)))
    cprev = cprev.reshape(depth, ns * SLOT, D_CONV)
    m0rows = jnp.pad(jnp.repeat(state_mlstm_m.astype(F32), SLOT, axis=1), ((0, 0), (0, 0), (0, LANES - N_HEADS)))
    s0_all, c0_all, n0_all = state_gdn.astype(F32), state_mlstm_C.astype(F32), state_mlstm_n.astype(F32)

    outs_p = [[] for _ in range(5)]
    outs_s = [[] for _ in range(5)]
    for i in range(depth):
        final = i == depth - 1
        proj_p, grow_p = _in_proj(xp, g_mix, w_r, i)
        proj_s, grow_s = _in_proj(xs, g_mix, w_r, i)
        proj_p3 = proj_p.reshape(nb, t, N_PROJ)
        proj_s3 = proj_s.reshape(1, ns * SLOT, N_PROJ)

        prep_p = _gdn_prep(proj_p3, grow_p, None, cw, pa, pl_, ra, rl, i, seq_len=c, valid_lo=0, idt=BF16, tb=512)
        mixa_p, s_p = _gdn_scan_prompt(*prep_p, proj_p3, gn, i, ts=512, bs=4)
        prep_s = _gdn_prep(proj_s3, grow_s, cprev, cw, pa, pl_, ra, rl, i, seq_len=SLOT, valid_lo=pad, idt=F32, tb=4 * c)
        mixa_s, s_s = _gdn_scan_decode(*prep_s, proj_s, gn, s0_all, outs_s[0] if final else [], i)

        mixb_p, c_p, n_p, m_p = _mlstm_prompt(proj_p3, grow_p.reshape(nb, t // c, N_GATE, c), pa, ra, nrm, i,
                                              ts=512, bs=4)
        mixb_s, n_s, m_s, c_s = _mlstm_decode(proj_s, grow_s, pa, ra, nrm, c0_all, n0_all, m0rows,
                                              outs_s[2] if final else [], i, nck=2)

        xp = _post(xp, mixa_p.reshape(nb * t, hd), mixb_p.reshape(nb * t, hd), pp_all, *wts, i, final=final)
        xs = _post(xs, mixa_s, mixb_s, ps_all, *wts, i, final=final)

        outs_p[0].append(s_p)
        outs_p[1].append(proj_p3[:, t - (CONV_W - 1):, :D_CONV])
        outs_p[2].append(c_p)
        outs_p[3].append(n_p)
        outs_p[4].append(m_p[:, 0, :N_HEADS])
        outs_s[0].append(s_s)
        outs_s[1].append(proj_s.reshape(ns, SLOT, N_PROJ)[:, SLOT - (CONV_W - 1):, :D_CONV])
        outs_s[2].append(c_s)
        outs_s[3].append(n_s)
        outs_s[4].append(m_s.reshape(ns, SLOT, LANES)[:, SLOT - 1, :N_HEADS])

    y_prompt = xp.reshape(nb, t, D_MODEL)
    y_sample = xs.reshape(ns, SLOT, D_MODEL)[:, pad:, :]
    sp = [jnp.stack(a, axis=0) for a in outs_p]
    ss = [outs_s[j][-1] if j in (0, 2) else jnp.stack(outs_s[j], axis=0) for j in range(5)]
    return (y_prompt, y_sample, *sp, *ss)
```

```python
import functools

import jax
import jax.numpy as jnp
from jax import lax
from jax.experimental import pallas as pl
from jax.experimental.pallas import tpu as pltpu

F32 = jnp.float32
BF16 = jnp.bfloat16

D_MODEL = 1024
N_HEADS = 4
DH = 128
D_CONV = 3 * N_HEADS * DH
D_FF = 2816
D_PLE = 256
CONV_W = 4
CHUNK = 64
EPS = 1e-6
NEG = -1e30
N_GATE = 16
LANES = 128
N_PROJ = D_CONV + 5 * N_HEADS * DH + LANES
GATE_COL_BLOCK = (N_PROJ - LANES) // LANES
SLOT = 8
SLOT_PAD = 4
V7X_VMEM_LIMIT_BYTES = 56 * 1024 * 1024
ROW_TILE = 512
FF_CHUNK = 256
PREP_ROWS = 512
SCAN_ROWS = 512
SCAN_SEQS = 4
DECODE_PREP_ROWS = 4 * CHUNK
DECODE_MLSTM_CHUNKS = 2
PREP_GROUPS = 2
PREP_SKEW_STAGES = 6


def _cparams(n_axes):
    return pltpu.CompilerParams(dimension_semantics=("arbitrary",) * n_axes,
                                vmem_limit_bytes=V7X_VMEM_LIMIT_BYTES)


def _rms(x, g):
    return x * lax.rsqrt(jnp.mean(x * x, axis=-1, keepdims=True) + EPS) * g


def _softplus(x):
    return jnp.maximum(x, 0.0) + jnp.log1p(jnp.exp(-jnp.abs(x)))


def _dot(a, b):
    return jnp.dot(a, b, preferred_element_type=F32)


def _dot_nt(a, b):
    return lax.dot_general(a, b, (((1,), (1,)), ((), ())), preferred_element_type=F32)


def _split(a):
    hi = a.astype(BF16)
    lo = (a - hi.astype(F32)).astype(BF16)
    return hi, lo


def _split_all(xs):
    return [_split(x) for x in xs]


def _mm3_all(a_list, b_list):
    out = []
    for (ah, al), (bh, bl) in zip(a_list, b_list):
        m = ah.shape[0]
        r = _dot(jnp.concatenate([ah, al], axis=0), bh)
        out.append(r[0:m] + (_dot(ah, bl) + r[m:2 * m]))
    return out


class _PairMasks:
    def __init__(self, seq_len):
        c = CHUNK
        self.r = lax.broadcasted_iota(jnp.int32, (c, 2 * c), 0)
        lane = lax.broadcasted_iota(jnp.int32, (c, 2 * c), 1)
        self.cc = lane & (c - 1)
        self.lo = lane < c
        if seq_len < c:
            sh = seq_len.bit_length() - 1
            same = (self.r >> sh) == (self.cc >> sh)
        else:
            same = self.r >= 0
        self.causal = (self.r >= self.cc) & same
        self.causal_f = jnp.where(self.causal, 1.0, 0.0)
        self.strict_f = jnp.where((self.r > self.cc) & same, 1.0, 0.0)
        self.eye = jnp.where(self.r == self.cc, 1.0, 0.0)
        self.lo_b = jnp.where(self.lo, 1.0, 0.0).astype(BF16)
        self.hi_b = jnp.where(self.lo, 0.0, 1.0).astype(BF16)

    def blockdiag(self, sp):
        return tuple(jnp.concatenate([x * self.lo_b, x * self.hi_b], axis=0) for x in sp)


def _unit_lower_inverse_pairs(l_list, pm, seq_len, out):
    def bd_all(sps):
        return [pm.blockdiag(sp) for sp in sps]

    blk8 = jnp.where((pm.r >> 3) == (pm.cc >> 3), 1.0, 0.0)
    n0 = [l * blk8 for l in l_list]
    n0s = _split_all(n0)
    n2 = _mm3_all(n0s, bd_all(n0s))
    yield
    n2s = _split_all(n2)
    n4 = _mm3_all(n2s, bd_all(n2s))
    p = _mm3_all(_split_all([pm.eye - a for a in n0]), bd_all(_split_all([pm.eye + a for a in n2])))
    yield
    d = _mm3_all(_split_all(p), bd_all(_split_all([pm.eye + a for a in n4])))
    yield
    ds = _split_all(d)
    s = 8
    while s < seq_len:
        sh = s.bit_length() - 1
        off = jnp.where(((pm.r >> (sh + 1)) == (pm.cc >> (sh + 1))) & ((pm.r >> sh) != (pm.cc >> sh)), 1.0, 0.0)
        de = _mm3_all(ds, bd_all(_split_all([l * off for l in l_list])))
        yield
        ded = _mm3_all(_split_all(de), bd_all(ds))
        yield
        d = [a - b for a, b in zip(d, ded)]
        ds = _split_all(d)
        s *= 2
    out.extend(ds)


def _in_proj_kernel(x_ref, g_ref, wt_ref, proj_ref, gt_ref):
    hb = _rms(x_ref[...], g_ref[...]).astype(BF16)
    proj_ref[...] = _dot_nt(hb, wt_ref[...])
    gt = proj_ref[:, N_PROJ - LANES:N_PROJ].T
    for j in range(gt_ref.shape[0]):
        gt_ref[j] = gt[0:N_GATE, CHUNK * j:CHUNK * (j + 1)]


def _layer_spec(layer, *shape):
    zeros = (0,) * len(shape)
    return pl.BlockSpec((None,) + shape, lambda *_: (layer,) + zeros)


def _in_proj(x, g, w, layer):
    rows = x.shape[0]
    tm = ROW_TILE
    return pl.pallas_call(
        _in_proj_kernel,
        grid=(rows // tm,),
        in_specs=[
            pl.BlockSpec((tm, D_MODEL), lambda i: (i, 0)),
            _layer_spec(layer, 1, D_MODEL),
            _layer_spec(layer, N_PROJ, D_MODEL),
        ],
        out_specs=[
            pl.BlockSpec((tm, N_PROJ), lambda i: (i, 0)),
            pl.BlockSpec((tm // CHUNK, N_GATE, CHUNK), lambda i: (i, 0, 0)),
        ],
        out_shape=[jax.ShapeDtypeStruct((rows, N_PROJ), F32),
                   jax.ShapeDtypeStruct((rows // CHUNK, N_GATE, CHUNK), F32)],
        compiler_params=_cparams(1),
        name="in_proj",
    )(x, g, w)


def _chunk_masks(seq_len):
    c = CHUNK
    ri = lax.broadcasted_iota(jnp.int32, (c, c), 0)
    ci = lax.broadcasted_iota(jnp.int32, (c, c), 1)
    if seq_len < c:
        sh = seq_len.bit_length() - 1
        same = (ri >> sh) == (ci >> sh)
    else:
        same = ri >= 0
    return ri, ci, same


def _gdn_prep_kernel(*refs, tb, seq_len, valid_lo, idt):
    nk = tb // CHUNK
    groups = PREP_GROUPS if nk % PREP_GROUPS == 0 else 1
    gens = [_gdn_prep_group(refs[:-5], refs[-5:], g * (nk // groups), nk // groups, seq_len, valid_lo, idt)
            for g in range(groups)]
    _run_skewed(gens, PREP_SKEW_STAGES)


def _run_skewed(gens, skew):
    live = []
    pending = list(gens)
    tick = 0
    while live or pending:
        if pending and tick % skew == 0:
            live.append(pending.pop(0))
        for g in list(live):
            try:
                next(g)
            except StopIteration:
                live.remove(g)
        tick += 1


def _gdn_prep_group(ins, outs, k0, nk, seq_len, valid_lo, idt):
    has_cprev = valid_lo > 0
    if has_cprev:
        u_ref, prev_ref, gcol_ref, grow_ref, cprev_ref, cw_ref, pa_ref, pl_ref, ra_ref, rl_ref = ins
    else:
        u_ref, prev_ref, gcol_ref, grow_ref, cw_ref, pa_ref, pl_ref, ra_ref, rl_ref = ins
    wkqg_ref, wv_ref, qk_ref, kdt_ref, el_ref = outs
    c = CHUNK
    sh = seq_len.bit_length() - 1
    r_lo, nrows = k0 * c, nk * c
    u = u_ref[0, r_lo:r_lo + nrows, :]
    rowpos = lax.broadcasted_iota(jnp.int32, (nrows, 1), 0) & (seq_len - 1)
    if has_cprev:
        u = jnp.where((rowpos >= valid_lo - (CONV_W - 1)) & (rowpos < valid_lo),
                      cprev_ref[0, r_lo:r_lo + nrows, :], u)
    if k0 == 0:
        prev = jnp.where(pl.program_id(1) == 0, 0.0, prev_ref[0])
    else:
        prev = u_ref[0, r_lo - 8:r_lo, :]
    xp = jnp.concatenate([prev, u], axis=0)
    cw = cw_ref[...]
    ys = []
    for part in range(3):
        cs = slice(part * N_HEADS * DH, (part + 1) * N_HEADS * DH)
        y = None
        for i in range(CONV_W):
            s = CONV_W - 1 - i
            xs = u[:, cs] if s == 0 else pltpu.roll(xp[:, cs], s, 0)[8:8 + nrows]
            t = xs * cw[i:i + 1, cs]
            y = t if y is None else y + t
        ys.append(jax.nn.silu(y))
        yield
    yq, yk, yv = ys

    lane = lax.broadcasted_iota(jnp.int32, (1, LANES), 1)
    xg = gcol_ref[0, r_lo:r_lo + nrows, :] + pa_ref[0:1, :]
    gcolv = jnp.where(lane < N_HEADS, jax.nn.sigmoid(xg), -jnp.exp(pl_ref[0:1, :]) * _softplus(xg))
    if valid_lo > 0:
        gcolv = jnp.where(rowpos >= valid_lo, gcolv, 0.0)

    ri, ci, same = _chunk_masks(seq_len)
    causal = (ri >= ci) & same
    upper = (ri <= ci) & same
    colpos = lax.broadcasted_iota(jnp.int32, (1, c), 1) & (seq_len - 1)
    r8 = lax.broadcasted_iota(jnp.int32, (8, c), 0)
    c8 = lax.broadcasted_iota(jnp.int32, (8, c), 1)

    items = []
    for kk in range(nk):
        r0 = kk * c
        k = k0 + kk
        growv = -jnp.exp(rl_ref[:, 0:1]) * _softplus(grow_ref[k] + ra_ref[:, 0:1])
        if valid_lo > 0:
            growv = jnp.where(colpos >= valid_lo, growv, 0.0)
        for h in range(N_HEADS):
            hs = slice(DH * h, DH * (h + 1))
            qc = yq[r0:r0 + c, hs]
            kc = yk[r0:r0 + c, hs]
            g_col = gcolv[r0:r0 + c, N_HEADS + h:N_HEADS + h + 1]
            g_row = growv[N_HEADS + h:N_HEADS + h + 1, :]
            items.append(dict(
                k=k, h=h, hs=hs, qc=qc, kc=kc, vc=yv[r0:r0 + c, hs],
                beta=gcolv[r0:r0 + c, h:h + 1],
                qss=jnp.sum(qc * qc, axis=-1, keepdims=True),
                kss=jnp.sum(kc * kc, axis=-1, keepdims=True),
                gc_col=jnp.sum(jnp.where(causal, g_row, 0.0), axis=1, keepdims=True),
                gc_row=jnp.sum(jnp.where(upper, g_col, 0.0), axis=0, keepdims=True),
                gl_col=jnp.sum(jnp.where(same, g_row, 0.0), axis=1, keepdims=True),
                gl8=jnp.sum(jnp.where((c8 >> sh) == r8, g_row, 0.0), axis=1, keepdims=True)))
    yield
    for it in items:
        k, hs, gc_col, beta = it["k"], it["hs"], it["gc_col"], it["beta"]
        qn = it["qc"] * lax.rsqrt(it["qss"] + EPS) * (DH ** -0.5)
        kn = it["kc"] * lax.rsqrt(it["kss"] + EPS)
        wkqg_ref[0, k, c:2 * c, hs] = (qn * jnp.exp(gc_col)).astype(idt)
        el_ref[0, k, :, hs] = jnp.broadcast_to(jnp.exp(it["gl8"]), (8, DH))
        it["kd"] = kn * jnp.exp(it["gl_col"] - gc_col)
        it["qb"] = qn.astype(BF16)
        it["kb"] = kn.astype(BF16)
        it["rhs"] = jnp.concatenate([beta * it["vc"], (beta * jnp.exp(gc_col)) * kn], axis=1)
    yield
    for a, b in zip(items[0::2], items[1::2]):
        kdt_ref[0, a["k"], a["h"] // 2] = jnp.concatenate([a["kd"], b["kd"]], axis=0).T.astype(idt)
    pm = _PairMasks(seq_len)
    zb = jnp.zeros((c, DH), BF16)
    pairs = []
    for a, b in zip(items[0::2], items[1::2]):
        gcp = jnp.where(pm.lo, a["gc_col"], b["gc_col"])
        grp = jnp.concatenate([a["gc_row"], b["gc_row"]], axis=1)
        decay = jnp.exp(jnp.where(pm.causal, gcp - grp, 0.0)) * pm.causal_f
        kbd = jnp.concatenate([jnp.concatenate([a["kb"], zb], axis=1),
                               jnp.concatenate([zb, b["kb"]], axis=1)], axis=0)
        kq = jnp.concatenate([jnp.concatenate([a["kb"], b["kb"]], axis=1),
                              jnp.concatenate([a["qb"], b["qb"]], axis=1)], axis=0)
        pairs.append(dict(a=a, b=b, decay=decay, kbd=kbd, kq=kq, beta=jnp.where(pm.lo, a["beta"], b["beta"])))
    kqs = [_dot_nt(p["kq"], p["kbd"]) for p in pairs]
    yield
    for p, kq in zip(pairs, kqs):
        qk_ref[0, p["a"]["k"], p["a"]["h"] // 2] = (kq[c:2 * c] * p["decay"]).astype(idt)
    ls = [p["beta"] * kq[0:c] * p["decay"] * pm.strict_f for p, kq in zip(pairs, kqs)]
    tinvs = []
    yield from _unit_lower_inverse_pairs(ls, pm, seq_len, tinvs)
    for p, (th, tl) in zip(pairs, tinvs):
        rh, rl_ = _split(jnp.concatenate([p["a"]["rhs"], p["b"]["rhs"]], axis=0))
        ta_h, tb_h = th * pm.lo_b, th * pm.hi_b
        r1 = _dot(jnp.concatenate([ta_h, tl * pm.lo_b, tb_h, tl * pm.hi_b], axis=0), rh)
        r2 = _dot(jnp.concatenate([ta_h, tb_h], axis=0), rl_)
        p["sol"] = (r1[0:c] + (r2[0:c] + r1[c:2 * c]), r1[2 * c:3 * c] + (r2[c:2 * c] + r1[3 * c:4 * c]))
    yield
    for p in pairs:
        for it, sol in zip((p["a"], p["b"]), p["sol"]):
            wv_ref[0, it["k"], :, it["hs"]] = sol[:, 0:DH]
            wkqg_ref[0, it["k"], 0:c, it["hs"]] = sol[:, DH:2 * DH].astype(idt)


def _gdn_prep(proj3, grow, cprev, cw, pa, pl_, ra, rl, layer, *, seq_len, valid_lo, idt, tb):
    nb, t, _ = proj3.shape
    c = CHUNK
    nt = t // tb
    kb = tb // c
    has_cprev = valid_lo > 0
    in_specs = [
        pl.BlockSpec((1, tb, D_CONV), lambda b, i: (b, i, 0)),
        pl.BlockSpec((1, 8, D_CONV), lambda b, i: (b, jnp.maximum(i * (tb // 8) - 1, 0), 0)),
        pl.BlockSpec((1, tb, LANES), lambda b, i: (b, i, GATE_COL_BLOCK)),
        pl.BlockSpec((kb, N_GATE, c), lambda b, i: (b * nt + i, 0, 0)),
    ]
    args = [proj3, proj3, proj3, grow]
    if has_cprev:
        assert nb == 1
        in_specs.append(pl.BlockSpec((1, tb, D_CONV), lambda b, i: (layer, i, 0)))
        args.append(cprev)
    in_specs += [
        _layer_spec(layer, CONV_W, D_CONV),
        _layer_spec(layer, 8, LANES),
        _layer_spec(layer, 8, LANES),
        _layer_spec(layer, N_GATE, LANES),
        _layer_spec(layer, N_GATE, LANES),
    ]
    args += [cw, pa, pl_, ra, rl]
    n = t // c
    hd = N_HEADS * DH
    out_shape = [
        jax.ShapeDtypeStruct((nb, n, 2 * c, hd), idt),
        jax.ShapeDtypeStruct((nb, n, c, hd), F32),
        jax.ShapeDtypeStruct((nb, n, N_HEADS // 2, c, 2 * c), idt),
        jax.ShapeDtypeStruct((nb, n, N_HEADS // 2, DH, 2 * c), idt),
        jax.ShapeDtypeStruct((nb, n, 8, hd), F32),
    ]
    out_specs = [
        pl.BlockSpec((1, kb, 2 * c, hd), lambda b, i: (b, i, 0, 0)),
        pl.BlockSpec((1, kb, c, hd), lambda b, i: (b, i, 0, 0)),
        pl.BlockSpec((1, kb, N_HEADS // 2, c, 2 * c), lambda b, i: (b, i, 0, 0, 0)),
        pl.BlockSpec((1, kb, N_HEADS // 2, DH, 2 * c), lambda b, i: (b, i, 0, 0, 0)),
        pl.BlockSpec((1, kb, 8, hd), lambda b, i: (b, i, 0, 0)),
    ]
    return pl.pallas_call(
        functools.partial(_gdn_prep_kernel, tb=tb, seq_len=seq_len, valid_lo=valid_lo, idt=idt),
        grid=(nb, nt),
        in_specs=in_specs,
        out_specs=out_specs,
        out_shape=out_shape,
        compiler_params=_cparams(2),
        name="gdn_prep",
    )(*args)


def _lane_half_masks(dtype):
    lo = lax.broadcasted_iota(jnp.int32, (CHUNK, 2 * CHUNK), 1) < CHUNK
    return jnp.where(lo, 1.0, 0.0).astype(dtype), jnp.where(lo, 0.0, 1.0).astype(dtype)


def _mean_square_lanes(x):
    rows = x.shape[0]
    ssq = _dot(jnp.concatenate(_split(x * x), axis=0), jnp.ones((DH, DH), BF16))
    return (ssq[0:rows] + ssq[rows:2 * rows]) * (1.0 / DH)


def _gated_norm_store(o, z, gn, mix_ref, idx):
    mix_ref[idx] = (_rms(o, gn) * jax.nn.silu(z)).astype(BF16)


def _gdn_scan_prompt_kernel(wkqg_ref, wv_ref, qk_ref, kdt_ref, el_ref, z_ref, gn_ref,
                            mix_ref, sout_ref, s_scr, *, nc, bs):
    c = CHUNK

    @pl.when(pl.program_id(1) == 0)
    def _():
        s_scr[...] = jnp.zeros_like(s_scr)

    gn = gn_ref[...]
    units = [(b, h, slice(DH * h, DH * (h + 1))) for b in range(bs) for h in range(N_HEADS)]
    half_b = _lane_half_masks(BF16)
    half2_b = tuple(jnp.concatenate([m, m], axis=0) for m in half_b)

    def body(n, carry):
        rows = pl.ds(pl.multiple_of(n * c, c), c)
        ss = [s_scr[b * N_HEADS + h] for b, h, _ in units]
        aa = [_dot(wkqg_ref[b, n, :, hs], s.astype(BF16)) for (b, _, hs), s in zip(units, ss)]
        ubs = [(wv_ref[b, n, :, hs] - a[0:c]).astype(BF16) for (b, _, hs), a in zip(units, aa)]
        os_, ds = [], []
        for j in range(len(units) // 2):
            b, h, _ = units[2 * j]
            qkp, kdp = qk_ref[b, n, h // 2], kdt_ref[b, n, h // 2]
            r = _dot(jnp.concatenate([qkp * half_b[0], qkp * half_b[1], kdp * half2_b[0], kdp * half2_b[1]],
                                     axis=0),
                     jnp.concatenate(ubs[2 * j:2 * j + 2], axis=0))
            os_ += [r[0:c], r[c:2 * c]]
            ds += [r[2 * c:2 * c + DH], r[2 * c + DH:2 * c + 2 * DH]]
        for i, (b, h, hs) in enumerate(units):
            s_scr[b * N_HEADS + h] = el_ref[b, n, 0:1, hs] * ss[i] + ds[i]
            _gated_norm_store(aa[i][c:2 * c] + os_[i], z_ref[b, rows, hs], gn, mix_ref, (b, rows, hs))
        return carry

    lax.fori_loop(0, nc, body, 0)

    @pl.when(pl.program_id(1) == pl.num_programs(1) - 1)
    def _():
        for b in range(bs):
            sout_ref[b] = s_scr[b * N_HEADS:(b + 1) * N_HEADS]


def _gdn_scan_prompt(wkqg, wv, qk, kdt, el, proj3, gn, layer, *, ts, bs):
    nb, n, _, hd = wkqg.shape
    c = CHUNK
    t = n * c
    nc = ts // c
    zblk = D_CONV // hd
    return pl.pallas_call(
        functools.partial(_gdn_scan_prompt_kernel, nc=nc, bs=bs),
        grid=(nb // bs, t // ts),
        in_specs=[
            pl.BlockSpec((bs, nc, 2 * c, hd), lambda b, i: (b, i, 0, 0)),
            pl.BlockSpec((bs, nc, c, hd), lambda b, i: (b, i, 0, 0)),
            pl.BlockSpec((bs, nc, N_HEADS // 2, c, 2 * c), lambda b, i: (b, i, 0, 0, 0)),
            pl.BlockSpec((bs, nc, N_HEADS // 2, DH, 2 * c), lambda b, i: (b, i, 0, 0, 0)),
            pl.BlockSpec((bs, nc, 8, hd), lambda b, i: (b, i, 0, 0)),
            pl.BlockSpec((bs, ts, hd), lambda b, i: (b, i, zblk)),
            _layer_spec(layer, 1, DH),
        ],
        out_specs=[
            pl.BlockSpec((bs, ts, hd), lambda b, i: (b, i, 0)),
            pl.BlockSpec((bs, N_HEADS, DH, DH), lambda b, i: (b, 0, 0, 0)),
        ],
        out_shape=[jax.ShapeDtypeStruct((nb, t, hd), BF16),
                   jax.ShapeDtypeStruct((nb, N_HEADS, DH, DH), F32)],
        scratch_shapes=[pltpu.VMEM((bs * N_HEADS, DH, DH), F32)],
        compiler_params=_cparams(2),
        name="gdn_scan_prompt",
    )(wkqg, wv, qk, kdt, el, proj3, gn)


def _gdn_scan_decode_kernel(*refs, n_prev):
    (wkqg_ref, wv_ref, qk_ref, kdt_ref, el_ref, z_ref, gn_ref, s0_ref), rest = refs[:8], refs[8:]
    prev_refs, (mix_ref, sout_ref) = rest[:n_prev], rest[n_prev:]
    if n_prev:
        for l, p_ref in enumerate(prev_refs):
            sout_ref[l] = p_ref[...]
        sout_ref = sout_ref.at[n_prev]
    c = CHUNK
    nseq = c // SLOT
    rowseq = lax.broadcasted_iota(jnp.int32, (c, 1), 0) >> 3
    gn = gn_ref[...]
    heads = [slice(DH * h, DH * (h + 1)) for h in range(N_HEADS)]
    aa = []
    for h, hs in enumerate(heads):
        w = wkqg_ref[0, 0, :, hs]
        row = []
        for j in range(nseq):
            wj = jnp.concatenate([w[SLOT * j:SLOT * (j + 1)], w[c + SLOT * j:c + SLOT * (j + 1)]],
                                 axis=0).astype(BF16)
            row.append(_dot(wj, s0_ref[j, h].astype(BF16)))
        aa.append(row)
    us = [wv_ref[0, 0, :, hs] - jnp.concatenate([a[0:SLOT] for a in aa[h]], axis=0)
          for h, hs in enumerate(heads)]
    half_f = _lane_half_masks(F32)
    os_ = []
    for j in range(N_HEADS // 2):
        qkp = qk_ref[0, 0, j]
        o2 = _dot(jnp.concatenate([qkp * half_f[0], qkp * half_f[1]], axis=0).astype(BF16),
                  jnp.concatenate(us[2 * j:2 * j + 2], axis=0).astype(BF16))
        os_ += [o2[0:c], o2[c:2 * c]]
    half2_f = tuple(jnp.concatenate([m, m], axis=0) for m in half_f)
    for p in range(N_HEADS // 2):
        kdp = kdt_ref[0, 0, p]
        kd2 = jnp.concatenate([kdp * half2_f[0], kdp * half2_f[1]], axis=0).astype(BF16)
        for j in range(nseq):
            uj = jnp.concatenate([jnp.where(rowseq == j, us[2 * p + i], 0.0) for i in range(2)], axis=0)
            r = _dot(kd2, uj.astype(BF16))
            for i in range(2):
                h = 2 * p + i
                sout_ref[j, h] = (el_ref[0, 0, j:j + 1, heads[h]] * s0_ref[j, h] + r[DH * i:DH * (i + 1)])
    for h, hs in enumerate(heads):
        o = jnp.concatenate([a[SLOT:2 * SLOT] for a in aa[h]], axis=0) + os_[h]
        _gated_norm_store(o, z_ref[:, hs], gn, mix_ref, (slice(None), hs))


def _state_specs(layer, prev, state_shape, nseq):
    tail = state_shape[2:]
    zeros = (0,) * len(tail)
    in_specs = [pl.BlockSpec((None, nseq) + tail, lambda i: (layer, i) + zeros)]
    in_specs += [pl.BlockSpec((nseq,) + tail, lambda i: (i,) + zeros) for _ in prev]
    if prev:
        out_spec = pl.BlockSpec((len(prev) + 1, nseq) + tail, lambda i: (0, i) + zeros)
        out_shape = jax.ShapeDtypeStruct((len(prev) + 1,) + state_shape[1:], F32)
    else:
        out_spec = pl.BlockSpec((nseq,) + tail, lambda i: (i,) + zeros)
        out_shape = jax.ShapeDtypeStruct(state_shape[1:], F32)
    return in_specs, out_spec, out_shape


def _gdn_scan_decode(wkqg, wv, qk, kdt, el, proj, gn, s0_all, prev, layer):
    _, n, _, hd = wkqg.shape
    c = CHUNK
    nseq = c // SLOT
    zblk = D_CONV // hd
    st_in, st_out, st_shape = _state_specs(layer, prev, s0_all.shape, nseq)
    return pl.pallas_call(
        functools.partial(_gdn_scan_decode_kernel, n_prev=len(prev)),
        grid=(n,),
        in_specs=[
            pl.BlockSpec((1, 1, 2 * c, hd), lambda i: (0, i, 0, 0)),
            pl.BlockSpec((1, 1, c, hd), lambda i: (0, i, 0, 0)),
            pl.BlockSpec((1, 1, N_HEADS // 2, c, 2 * c), lambda i: (0, i, 0, 0, 0)),
            pl.BlockSpec((1, 1, N_HEADS // 2, DH, 2 * c), lambda i: (0, i, 0, 0, 0)),
            pl.BlockSpec((1, 1, 8, hd), lambda i: (0, i, 0, 0)),
            pl.BlockSpec((c, hd), lambda i: (i, zblk)),
            _layer_spec(layer, 1, DH),
        ] + st_in,
        out_specs=[pl.BlockSpec((c, hd), lambda i: (i, 0)), st_out],
        out_shape=[jax.ShapeDtypeStruct((n * c, hd), BF16), st_shape],
        compiler_params=_cparams(1),
        name="gdn_scan_decode",
    )(wkqg, wv, qk, kdt, el, proj, gn, s0_all, *prev)


def _mlstm_gate_blocks(gcol, growk, pa_ref, ra_ref):
    xg = gcol + pa_ref[0:1, :]
    xr = growk + ra_ref[:, 0:1]
    return xg, -_softplus(-xg), xr, -_softplus(-xr)


def _mlstm_gates(blocks, h, valid_lo, rowpos, colpos):
    xg, lg, xr, lr = blocks
    ig_col = xg[:, 2 * N_HEADS + h:2 * N_HEADS + h + 1]
    fl_col = lg[:, 3 * N_HEADS + h:3 * N_HEADS + h + 1]
    ig_row = xr[2 * N_HEADS + h:2 * N_HEADS + h + 1, :]
    fl_row = lr[3 * N_HEADS + h:3 * N_HEADS + h + 1, :]
    if valid_lo > 0:
        ig_col = jnp.where(rowpos >= valid_lo, ig_col, NEG)
        fl_col = jnp.where(rowpos >= valid_lo, fl_col, 0.0)
        ig_row = jnp.where(colpos >= valid_lo, ig_row, NEG)
        fl_row = jnp.where(colpos >= valid_lo, fl_row, 0.0)
    return ig_col, fl_col, ig_row, fl_row


def _mlstm_chunks(probs, seq_len, finish):
    c = CHUNK
    nseq = c // seq_len
    ri, ci, same = _chunk_masks(seq_len)
    causal = (ri >= ci) & same
    causal_f = jnp.where(causal, 1.0, 0.0)
    upper = (ri <= ci) & same
    seq_end = ci == (ri | (seq_len - 1))
    rowseq = lax.broadcasted_iota(jnp.int32, (c, 1), 0) >> (seq_len.bit_length() - 1)
    zpad = jnp.zeros((16 - seq_len, DH), F32) if nseq > 1 else None
    ones_b = jnp.ones((c, LANES), BF16)
    for p in probs:
        p["f_row"] = jnp.sum(jnp.where(upper, p["fl_col"], 0.0), axis=0, keepdims=True)
        p["km"] = p["k"] * (DH ** -0.5)
        p["qb"] = p["q"].astype(BF16)
        p["kb"] = p["km"].astype(BF16)
        p["vb"] = p["v"].astype(BF16)
        if nseq == 1:
            r = _dot(jnp.concatenate(_split(jnp.where(causal, p["fl_row"], 0.0)), axis=0), ones_b)
            p["f_col"] = r[0:c] + r[c:2 * c]
            nb2 = p["n_rows"].astype(BF16)
            p["qn"] = _dot_nt(p["qb"], jnp.concatenate([nb2, nb2], axis=0))
            p["ig_col"] = jnp.broadcast_to(p["ig_col"], (c, LANES))
        else:
            p["f_col"] = jnp.sum(jnp.where(causal, p["fl_row"], 0.0), axis=1, keepdims=True)
            p["qn"] = jnp.sum(p["q"] * p["n_rows"], axis=1, keepdims=True)
    yield
    for p in probs:
        p["b_col"] = p["ig_col"] - p["f_col"]
        p["b_row"] = p["ig_row"] - p["f_row"]
        p["bmax_col"] = jnp.max(jnp.where(causal, p["b_row"], -jnp.inf), axis=1, keepdims=True)
        if nseq == 1:
            p["bmax_col"] = jnp.broadcast_to(p["bmax_col"], (c, LANES))
        if nseq > 1:
            p["fl_end"] = jnp.sum(jnp.where(seq_end, p["f_row"], 0.0), axis=1, keepdims=True)
            p["bmax_end"] = jnp.max(jnp.where(same, p["b_row"], -jnp.inf), axis=1, keepdims=True)
    yield
    for p in probs:
        f_col, mp_col = p["f_col"], p["mp_col"]
        m_col = f_col + jnp.maximum(mp_col, p["bmax_col"])
        p["m_col"] = m_col
        p["a_col"] = jnp.exp(f_col + mp_col - m_col)
        fm = (f_col - m_col)[:, 0:c] if nseq == 1 else f_col - m_col
        p["dexp"] = jnp.exp(jnp.where(causal, fm + p["b_row"], 0.0)) * causal_f
        if nseq == 1:
            fl_end, ml, mp = f_col[c - 1:c], m_col[c - 1:c], mp_col[c - 1:c]
            p["ml_col"] = ml
            p["al"] = jnp.exp(fl_end + mp - ml)
        else:
            fl_end, mp = p["fl_end"], mp_col
            ml = fl_end + jnp.maximum(mp_col, p["bmax_end"])
            p["ml_col"] = ml
            p["al"] = jnp.broadcast_to(jnp.exp(fl_end + mp - ml), (c, DH))
        p["kw"] = p["km"] * jnp.exp(fl_end + p["b_col"] - ml)
    yield
    for p in probs:
        p["kwt"] = p["kw"].T.astype(BF16)
    for p in probs:
        p["qk"] = _dot_nt(p["qb"], p["kb"])
    yield
    for p in probs:
        if nseq == 1:
            p["qc"] = _dot(p["qb"], p["c_list"][0].astype(BF16))
            p["upd"] = [_dot(p["kwt"], p["vb"])]
        else:
            parts, upd = [], []
            for j in range(nseq):
                qj = jnp.concatenate([p["q"][seq_len * j:seq_len * (j + 1)], zpad], axis=0).astype(BF16)
                parts.append(_dot(qj, p["c_list"][j].astype(BF16))[0:seq_len])
                upd.append(_dot(p["kwt"], jnp.where(rowseq == j, p["v"], 0.0).astype(BF16)))
            p["qc"] = jnp.concatenate(parts, axis=0)
            p["upd"] = upd
    yield
    if nseq == 1:
        for p in probs:
            p["w_hl"] = _split(p["dexp"] * p["qk"])
        yield
        for p in probs:
            r = _dot(jnp.concatenate(p["w_hl"], axis=0), jnp.concatenate([p["vb"], ones_b], axis=1))
            p["wv"] = r[0:c, 0:DH]
            p["wsum"] = r[0:c, DH:DH + LANES] + r[c:2 * c, DH:DH + LANES]
    else:
        for p in probs:
            p["w"] = p["dexp"] * p["qk"]
            p["wsum"] = jnp.sum(p["w"], axis=1, keepdims=True)
        for p in probs:
            p["wv"] = _dot(p["w"].astype(BF16), p["vb"])
    yield
    for p in probs:
        num = p["a_col"] * p["qc"] + p["wv"]
        den = p["a_col"] * p["qn"] + p["wsum"]
        p["hout"] = num / jnp.maximum(jnp.abs(den), jnp.exp(-p["m_col"]))
        p["new_c"] = [p["al"][seq_len * j:seq_len * j + 1] * p["c_list"][j] + p["upd"][j]
                      for j in range(nseq)]
    yield
    finish(probs)


def _mlstm_out_store(hout, og, gnorm, mix_ref, idx):
    x = hout * jax.nn.sigmoid(og)
    mix_ref[idx] = (x * lax.rsqrt(_mean_square_lanes(x) + EPS) * gnorm).astype(BF16)


def _mlstm_prompt_kernel(q_ref, k_ref, v_ref, o_ref, gcol_ref, grow_ref, pa_ref, ra_ref, nrm_ref,
                         mix_ref, cout_ref, nout_ref, mout_ref, c_scr, n_scr, m_scr, *, nc, bs):
    c = CHUNK

    @pl.when(pl.program_id(1) == 0)
    def _():
        c_scr[...] = jnp.zeros_like(c_scr)
        n_scr[...] = jnp.zeros_like(n_scr)
        m_scr[...] = jnp.zeros_like(m_scr)

    def body(n, carry):
        r0 = pl.multiple_of(n * c, c)
        rows = pl.ds(r0, c)
        def finish(probs):
            for p in probs:
                st = p["st"]
                c_scr[st] = p["new_c"][0]
                n_scr[st] = jnp.broadcast_to(
                    p["al"][0:1] * p["n_rows"][0:1] + jnp.sum(p["kw"], axis=0, keepdims=True), (c, DH))
                m_scr[st] = jnp.broadcast_to(p["ml_col"], (c, DH))
                _mlstm_out_store(p["hout"], o_ref[p["b"], rows, p["hs"]], nrm_ref[:, p["hs"]], mix_ref,
                                 (p["b"], rows, p["hs"]))

        probs = []
        for b in range(bs):
            blocks = _mlstm_gate_blocks(gcol_ref[b, rows, :], grow_ref[b, n], pa_ref, ra_ref)
            for h in range(N_HEADS):
                hs = slice(DH * h, DH * (h + 1))
                ig_col, fl_col, ig_row, fl_row = _mlstm_gates(blocks, h, 0, None, None)
                st = b * N_HEADS + h
                probs.append(dict(b=b, hs=hs, st=st, q=q_ref[b, rows, hs], k=k_ref[b, rows, hs],
                                  v=v_ref[b, rows, hs], ig_col=ig_col, fl_col=fl_col, ig_row=ig_row,
                                  fl_row=fl_row, mp_col=m_scr[st], n_rows=n_scr[st],
                                  c_list=[c_scr[st]]))
        _run_skewed([_mlstm_chunks(probs, c, finish)], 1)
        return carry

    lax.fori_loop(0, nc, body, 0)

    @pl.when(pl.program_id(1) == pl.num_programs(1) - 1)
    def _():
        lane = lax.broadcasted_iota(jnp.int32, (8, DH), 1)
        for b in range(bs):
            mo = jnp.zeros((8, DH), F32)
            for h in range(N_HEADS):
                st = b * N_HEADS + h
                cout_ref[b, h] = c_scr[st]
                nout_ref[b, h:h + 1, :] = n_scr[st][0:1]
                mo = jnp.where(lane == h, m_scr[st][0:8], mo)
            mout_ref[b] = mo


def _mlstm_prompt(proj3, grow4, pa, ra, nrm, layer, *, ts, bs):
    nb, t, _ = proj3.shape
    c = CHUNK
    hd = N_HEADS * DH
    nc = ts // c
    qblk = (D_CONV + hd) // hd
    return pl.pallas_call(
        functools.partial(_mlstm_prompt_kernel, nc=nc, bs=bs),
        grid=(nb // bs, t // ts),
        in_specs=[
            pl.BlockSpec((bs, ts, hd), lambda b, i: (b, i, qblk)),
            pl.BlockSpec((bs, ts, hd), lambda b, i: (b, i, qblk + 1)),
            pl.BlockSpec((bs, ts, hd), lambda b, i: (b, i, qblk + 2)),
            pl.BlockSpec((bs, ts, hd), lambda b, i: (b, i, qblk + 3)),
            pl.BlockSpec((bs, ts, LANES), lambda b, i: (b, i, GATE_COL_BLOCK)),
            pl.BlockSpec((bs, nc, N_GATE, c), lambda b, i: (b, i, 0, 0)),
            _layer_spec(layer, 8, LANES),
            _layer_spec(layer, N_GATE, LANES),
            _layer_spec(layer, 1, hd),
        ],
        out_specs=[
            pl.BlockSpec((bs, ts, hd), lambda b, i: (b, i, 0)),
            pl.BlockSpec((bs, N_HEADS, DH, DH), lambda b, i: (b, 0, 0, 0)),
            pl.BlockSpec((bs, N_HEADS, DH), lambda b, i: (b, 0, 0)),
            pl.BlockSpec((bs, 8, DH), lambda b, i: (b, 0, 0)),
        ],
        out_shape=[jax.ShapeDtypeStruct((nb, t, hd), BF16),
                   jax.ShapeDtypeStruct((nb, N_HEADS, DH, DH), F32),
                   jax.ShapeDtypeStruct((nb, N_HEADS, DH), F32),
                   jax.ShapeDtypeStruct((nb, 8, DH), F32)],
        scratch_shapes=[pltpu.VMEM((bs * N_HEADS, DH, DH), F32),
                        pltpu.VMEM((bs * N_HEADS, c, DH), F32),
                        pltpu.VMEM((bs * N_HEADS, c, DH), F32)],
        compiler_params=_cparams(2),
        name="mlstm_prompt",
    )(proj3, proj3, proj3, proj3, proj3, grow4, pa, ra, nrm)


def _mlstm_decode_kernel(*refs, n_prev):
    (q_ref, k_ref, v_ref, o_ref, gcol_ref, grow_ref, pa_ref, ra_ref, nrm_ref, n0_ref, m0_ref,
     c0_ref), rest = refs[:12], refs[12:]
    prev_refs, (mix_ref, nout_ref, mout_ref, cout_ref) = rest[:n_prev], rest[n_prev:]
    if n_prev:
        for l, p_ref in enumerate(prev_refs):
            cout_ref[l] = p_ref[...]
        cout_ref = cout_ref.at[n_prev]
    c = CHUNK
    nseq = c // SLOT
    nck = q_ref.shape[0] // c
    rowpos = lax.broadcasted_iota(jnp.int32, (c, 1), 0) & (SLOT - 1)
    colpos = lax.broadcasted_iota(jnp.int32, (1, c), 1) & (SLOT - 1)
    lane = lax.broadcasted_iota(jnp.int32, (c, DH), 1)
    probs = []
    for ck in range(nck):
        rows = slice(c * ck, c * (ck + 1))
        blocks = _mlstm_gate_blocks(gcol_ref[rows, :], grow_ref[ck], pa_ref, ra_ref)
        for h in range(N_HEADS):
            hs = slice(DH * h, DH * (h + 1))
            ig_col, fl_col, ig_row, fl_row = _mlstm_gates(blocks, h, SLOT_PAD, rowpos, colpos)
            seqs = [nseq * ck + j for j in range(nseq)]
            n_rows = jnp.concatenate(
                [jnp.broadcast_to(n0_ref[s, h:h + 1, :], (SLOT, DH)) for s in seqs], axis=0)
            probs.append(dict(ck=ck, rows=rows, seqs=seqs, h=h, hs=hs, q=q_ref[rows, hs], k=k_ref[rows, hs],
                              v=v_ref[rows, hs], ig_col=ig_col, fl_col=fl_col, ig_row=ig_row, fl_row=fl_row,
                              mp_col=m0_ref[rows, h:h + 1], n_rows=n_rows,
                              c_list=[c0_ref[s, h] for s in seqs]))

    def finish(probs):
        mos = [jnp.zeros((c, DH), F32) for _ in range(nck)]
        for p in probs:
            h, hs, rows = p["h"], p["hs"], p["rows"]
            for j, s in enumerate(p["seqs"]):
                cout_ref[s, h] = p["new_c"][j]
                rs = slice(SLOT * j, SLOT * (j + 1))
                nout_ref[s, h:h + 1, :] = (p["al"][SLOT * j:SLOT * j + 1] * n0_ref[s, h:h + 1, :]
                                           + jnp.sum(p["kw"][rs], axis=0, keepdims=True))
            mos[p["ck"]] = jnp.where(lane == h, jnp.broadcast_to(p["ml_col"], (c, DH)), mos[p["ck"]])
            _mlstm_out_store(p["hout"], o_ref[rows, hs], nrm_ref[:, hs], mix_ref, (rows, hs))
        for ck in range(nck):
            mout_ref[c * ck:c * (ck + 1), :] = mos[ck]

    _run_skewed([_mlstm_chunks(probs, SLOT, finish)], 1)


def _mlstm_decode(proj, grow, pa, ra, nrm, c0_all, n0_all, m0rows_all, prev, layer, *, nck):
    rows = proj.shape[0]
    c = CHUNK * nck
    hd = N_HEADS * DH
    nseq = c // SLOT
    qblk = (D_CONV + hd) // hd
    st_in, st_out, st_shape = _state_specs(layer, prev, c0_all.shape, nseq)
    return pl.pallas_call(
        functools.partial(_mlstm_decode_kernel, n_prev=len(prev)),
        grid=(rows // c,),
        in_specs=[
            pl.BlockSpec((c, hd), lambda i: (i, qblk)),
            pl.BlockSpec((c, hd), lambda i: (i, qblk + 1)),
            pl.BlockSpec((c, hd), lambda i: (i, qblk + 2)),
            pl.BlockSpec((c, hd), lambda i: (i, qblk + 3)),
            pl.BlockSpec((c, LANES), lambda i: (i, GATE_COL_BLOCK)),
            pl.BlockSpec((nck, N_GATE, CHUNK), lambda i: (i, 0, 0)),
            _layer_spec(layer, 8, LANES),
            _layer_spec(layer, N_GATE, LANES),
            _layer_spec(layer, 1, hd),
            pl.BlockSpec((None, nseq, N_HEADS, DH), lambda i: (layer, i, 0, 0)),
            pl.BlockSpec((None, c, LANES), lambda i: (layer, i, 0)),
        ] + st_in,
        out_specs=[
            pl.BlockSpec((c, hd), lambda i: (i, 0)),
            pl.BlockSpec((nseq, N_HEADS, DH), lambda i: (i, 0, 0)),
            pl.BlockSpec((c, LANES), lambda i: (i, 0)),
            st_out,
        ],
        out_shape=[jax.ShapeDtypeStruct((rows, hd), BF16),
                   jax.ShapeDtypeStruct(n0_all.shape[1:], F32),
                   jax.ShapeDtypeStruct((rows, LANES), F32),
                   st_shape],
        compiler_params=_cparams(1),
        name="mlstm_decode",
    )(proj, proj, proj, proj, proj, grow, pa, ra, nrm, n0_all, m0rows_all, c0_all, *prev)


def _post_kernel(x_ref, ma_ref, mb_ref, p_ref, woa_ref, wob_ref, gf_ref, wg_ref, wu_ref, wd_ref,
                 gp_ref, wpg_ref, wpp_ref, gfin_ref, o_ref, acc_ref, *, final):
    x = x_ref[...] + (_dot(ma_ref[...], woa_ref[...]) + _dot(mb_ref[...], wob_ref[...]))
    ub = _rms(x, gf_ref[...]).astype(BF16)
    for j in range(D_FF // FF_CHUNK):
        sl = slice(j * FF_CHUNK, (j + 1) * FF_CHUNK)
        a = (jax.nn.silu(_dot(ub, wg_ref[:, sl])) * _dot(ub, wu_ref[:, sl])).astype(BF16)
        d = _dot(a, wd_ref[sl, :])
        if j == 0:
            acc_ref[...] = d
        else:
            acc_ref[...] += d
    x = x + acc_ref[...]
    gate = jax.nn.sigmoid(_dot(_rms(x, gp_ref[...]).astype(BF16), wpg_ref[...]))
    x = x + _dot(p_ref[...].astype(BF16), wpp_ref[...]) * gate
    if final:
        x = _rms(x, gfin_ref[...])
    o_ref[...] = x


def _post(x, ma, mb, p_all, wo, gf, wg, wu, wd, gp, wpg, wpp, gfin, layer, *, final):
    rows = x.shape[0]
    tm = ROW_TILE
    hd = N_HEADS * DH
    row = lambda w: pl.BlockSpec((tm, w), lambda i: (i, 0))

    def whole(a, b, blk=0):
        return pl.BlockSpec((None, a, b), lambda i: (layer, blk, 0), pipeline_mode=pl.Buffered(1))

    return pl.pallas_call(
        functools.partial(_post_kernel, final=final),
        grid=(rows // tm,),
        in_specs=[row(D_MODEL), row(hd), row(hd),
                  pl.BlockSpec((None, tm, D_PLE), lambda i: (layer, i, 0)),
                  whole(hd, D_MODEL, 0), whole(hd, D_MODEL, 1), whole(1, D_MODEL),
                  whole(D_MODEL, D_FF), whole(D_MODEL, D_FF), whole(D_FF, D_MODEL),
                  whole(1, D_MODEL), whole(D_MODEL, D_MODEL), whole(D_PLE, D_MODEL),
                  pl.BlockSpec((1, D_MODEL), lambda i: (0, 0))],
        out_specs=row(D_MODEL),
        out_shape=jax.ShapeDtypeStruct((rows, D_MODEL), F32),
        scratch_shapes=[pltpu.VMEM((tm, D_MODEL), F32)],
        compiler_params=_cparams(1),
        name="post",
    )(x, ma, mb, p_all, wo, wo, gf, wg, wu, wd, gp, wpg, wpp, gfin)


def _gate_vec(pairs, depth):
    v = jnp.zeros((depth, N_GATE), F32)
    for off, val in pairs:
        v = v.at[:, off:off + N_HEADS].set(val.astype(F32))
    return v


def _lane_form(v):
    d = v.shape[0]
    return jnp.broadcast_to(jnp.pad(v, ((0, 0), (0, LANES - N_GATE)))[:, None, :], (d, 8, LANES))


def _row_form(v):
    d = v.shape[0]
    return jnp.broadcast_to(v[:, :, None], (d, N_GATE, LANES))


def kernel(x_prompt, x_sample, p_prompt, p_sample, state_gdn, state_gdn_conv, state_mlstm_C, state_mlstm_n, state_mlstm_m, w_in, conv_w, gdn_a_log, gdn_dt_bias, gdn_norm, mlstm_i_bias, mlstm_f_bias, mlstm_norm, w_out, norm_mix, norm_ffn, w_gate, w_up, w_down, norm_ple, w_ple_gate, w_ple_proj, norm_final):
    depth = w_in.shape[0]
    nb, t, _ = x_prompt.shape
    ns, tdec, _ = x_sample.shape
    hd = N_HEADS * DH
    c = CHUNK
    pad = SLOT - tdec

    xp = x_prompt.reshape(nb * t, D_MODEL)
    xs = jnp.pad(x_sample, ((0, 0), (pad, 0), (0, 0))).reshape(ns * SLOT, D_MODEL)
    ps_all = jnp.pad(p_sample, ((0, 0), (0, 0), (pad, 0), (0, 0))).reshape(depth, ns * SLOT, D_PLE)
    pp_all = p_prompt.reshape(depth, nb * t, D_PLE)
    gfin = norm_final.reshape(1, D_MODEL)

    o = D_CONV + hd
    wt = jnp.swapaxes(w_in, 1, 2).astype(BF16)
    w_r = jnp.concatenate([wt[:, :o], wt[:, o + 2 * N_HEADS:o + 2 * N_HEADS + 4 * hd], wt[:, o:o + 2 * N_HEADS],
                           wt[:, o + 2 * N_HEADS + 4 * hd:],
                           jnp.zeros((depth, LANES - N_GATE, D_MODEL), BF16)], axis=1)
    adds = _gate_vec([(N_HEADS, gdn_dt_bias), (2 * N_HEADS, mlstm_i_bias), (3 * N_HEADS, mlstm_f_bias)], depth)
    alog = _gate_vec([(N_HEADS, gdn_a_log)], depth)
    pa, pl_, ra, rl = _lane_form(adds), _lane_form(alog), _row_form(adds), _row_form(alog)
    g_mix = norm_mix.reshape(depth, 1, D_MODEL)
    cw = conv_w.astype(F32)
    gn = gdn_norm.reshape(depth, 1, DH).astype(F32)
    nrm = mlstm_norm.reshape(depth, 1, hd).astype(F32)
    wts = (w_out.astype(BF16), norm_ffn.reshape(depth, 1, D_MODEL), w_gate.astype(BF16), w_up.astype(BF16),
           w_down.astype(BF16), norm_ple.reshape(depth, 1, D_MODEL), w_ple_gate.astype(BF16),
           w_ple_proj.astype(BF16), gfin)
    cprev = jnp.pad(state_gdn_conv.astype(F32), ((0, 0), (0, 0), (pad - (CONV_W - 1), SLOT - pad), (0, 0)))
    cprev = cprev.reshape(depth, ns * SLOT, D_CONV)
    m0rows = jnp.pad(jnp.repeat(state_mlstm_m.astype(F32), SLOT, axis=1), ((0, 0), (0, 0), (0, LANES - N_HEADS)))
    s0_all, c0_all, n0_all = state_gdn.astype(F32), state_mlstm_C.astype(F32), state_mlstm_n.astype(F32)

    outs_p = [[] for _ in range(5)]
    outs_s = [[] for _ in range(5)]
    for i in range(depth):
        final = i == depth - 1
        proj_p, grow_p = _in_proj(xp, g_mix, w_r, i)
        proj_s, grow_s = _in_proj(xs, g_mix, w_r, i)
        proj_p3 = proj_p.reshape(nb, t, N_PROJ)
        proj_s3 = proj_s.reshape(1, ns * SLOT, N_PROJ)

        prep_p = _gdn_prep(proj_p3, grow_p, None, cw, pa, pl_, ra, rl, i, seq_len=c, valid_lo=0, idt=BF16,
                           tb=PREP_ROWS)
        mixa_p, s_p = _gdn_scan_prompt(*prep_p, proj_p3, gn, i, ts=SCAN_ROWS, bs=SCAN_SEQS)
        prep_s = _gdn_prep(proj_s3, grow_s, cprev, cw, pa, pl_, ra, rl, i, seq_len=SLOT, valid_lo=pad, idt=F32,
                           tb=DECODE_PREP_ROWS)
        mixa_s, s_s = _gdn_scan_decode(*prep_s, proj_s, gn, s0_all, outs_s[0] if final else [], i)

        mixb_p, c_p, n_p, m_p = _mlstm_prompt(proj_p3, grow_p.reshape(nb, t // c, N_GATE, c), pa, ra, nrm, i,
                                              ts=SCAN_ROWS, bs=SCAN_SEQS)
        mixb_s, n_s, m_s, c_s = _mlstm_decode(proj_s, grow_s, pa, ra, nrm, c0_all, n0_all, m0rows,
                                              outs_s[2] if final else [], i, nck=DECODE_MLSTM_CHUNKS)

        xp = _post(xp, mixa_p.reshape(nb * t, hd), mixb_p.reshape(nb * t, hd), pp_all, *wts, i, final=final)
        xs = _post(xs, mixa_s, mixb_s, ps_all, *wts, i, final=final)

        outs_p[0].append(s_p)
        outs_p[1].append(proj_p3[:, t - (CONV_W - 1):, :D_CONV])
        outs_p[2].append(c_p)
        outs_p[3].append(n_p)
        outs_p[4].append(m_p[:, 0, :N_HEADS])
        outs_s[0].append(s_s)
        outs_s[1].append(proj_s.reshape(ns, SLOT, N_PROJ)[:, SLOT - (CONV_W - 1):, :D_CONV])
        outs_s[2].append(c_s)
        outs_s[3].append(n_s)
        outs_s[4].append(m_s.reshape(ns, SLOT, LANES)[:, SLOT - 1, :N_HEADS])

    y_prompt = xp.reshape(nb, t, D_MODEL)
    y_sample = xs.reshape(ns, SLOT, D_MODEL)[:, pad:, :]
    sp = [jnp.stack(a, axis=0) for a in outs_p]
    ss = [outs_s[j][-1] if j in (0, 2) else jnp.stack(outs_s[j], axis=0) for j in range(5)]
    return (y_prompt, y_sample, *sp, *ss)
```

```python
import functools

import jax
import jax.numpy as jnp
from jax import lax
from jax.experimental import pallas as pl
from jax.experimental.pallas import tpu as pltpu

F32 = jnp.float32
BF16 = jnp.bfloat16

D_MODEL = 1024
N_HEADS = 4
DH = 128
D_CONV = 3 * N_HEADS * DH
D_FF = 2816
D_PLE = 256
CONV_W = 4
CHUNK = 64
EPS = 1e-6
NEG = -1e30
N_GATE = 16
LANES = 128
N_PROJ = D_CONV + 5 * N_HEADS * DH + LANES
GATE_COL_BLOCK = (N_PROJ - LANES) // LANES
SLOT = 8
SLOT_PAD = 4
V7X_VMEM_LIMIT_BYTES = 56 * 1024 * 1024
ROW_TILE = 512
FF_CHUNK = 256
PREP_ROWS = 512
SCAN_ROWS = 256
SCAN_SEQS = 8
DECODE_PREP_ROWS = 4 * CHUNK
DECODE_MLSTM_CHUNKS = 2
PREP_GROUPS = 2
PREP_SKEW_STAGES = 6


def _cparams(n_axes):
    return pltpu.CompilerParams(dimension_semantics=("arbitrary",) * n_axes,
                                vmem_limit_bytes=V7X_VMEM_LIMIT_BYTES)


def _rms(x, g):
    return x * lax.rsqrt(jnp.mean(x * x, axis=-1, keepdims=True) + EPS) * g


def _softplus(x):
    return jnp.maximum(x, 0.0) + jnp.log1p(jnp.exp(-jnp.abs(x)))


def _dot(a, b):
    return jnp.dot(a, b, preferred_element_type=F32)


def _dot_nt(a, b):
    return lax.dot_general(a, b, (((1,), (1,)), ((), ())), preferred_element_type=F32)


def _split(a):
    hi = a.astype(BF16)
    lo = (a - hi.astype(F32)).astype(BF16)
    return hi, lo


def _split_all(xs):
    return [_split(x) for x in xs]


def _mm3_all(a_list, b_list):
    out = []
    for (ah, al), (bh, bl) in zip(a_list, b_list):
        m = ah.shape[0]
        r = _dot(jnp.concatenate([ah, al], axis=0), bh)
        out.append(r[0:m] + (_dot(ah, bl) + r[m:2 * m]))
    return out


class _PairMasks:
    def __init__(self, seq_len):
        c = CHUNK
        self.r = lax.broadcasted_iota(jnp.int32, (c, 2 * c), 0)
        lane = lax.broadcasted_iota(jnp.int32, (c, 2 * c), 1)
        self.cc = lane & (c - 1)
        self.lo = lane < c
        if seq_len < c:
            sh = seq_len.bit_length() - 1
            same = (self.r >> sh) == (self.cc >> sh)
        else:
            same = self.r >= 0
        self.causal = (self.r >= self.cc) & same
        self.causal_f = jnp.where(self.causal, 1.0, 0.0)
        self.strict_f = jnp.where((self.r > self.cc) & same, 1.0, 0.0)
        self.eye = jnp.where(self.r == self.cc, 1.0, 0.0)
        self.lo_b = jnp.where(self.lo, 1.0, 0.0).astype(BF16)
        self.hi_b = jnp.where(self.lo, 0.0, 1.0).astype(BF16)

    def blockdiag(self, sp):
        return tuple(jnp.concatenate([x * self.lo_b, x * self.hi_b], axis=0) for x in sp)


def _unit_lower_inverse_pairs(l_list, pm, seq_len, out):
    def bd_all(sps):
        return [pm.blockdiag(sp) for sp in sps]

    blk8 = jnp.where((pm.r >> 3) == (pm.cc >> 3), 1.0, 0.0)
    n0 = [l * blk8 for l in l_list]
    n0s = _split_all(n0)
    n2 = _mm3_all(n0s, bd_all(n0s))
    yield
    n2s = _split_all(n2)
    n4 = _mm3_all(n2s, bd_all(n2s))
    p = _mm3_all(_split_all([pm.eye - a for a in n0]), bd_all(_split_all([pm.eye + a for a in n2])))
    yield
    d = _mm3_all(_split_all(p), bd_all(_split_all([pm.eye + a for a in n4])))
    yield
    ds = _split_all(d)
    s = 8
    while s < seq_len:
        sh = s.bit_length() - 1
        off = jnp.where(((pm.r >> (sh + 1)) == (pm.cc >> (sh + 1))) & ((pm.r >> sh) != (pm.cc >> sh)), 1.0, 0.0)
        de = _mm3_all(ds, bd_all(_split_all([l * off for l in l_list])))
        yield
        ded = _mm3_all(_split_all(de), bd_all(ds))
        yield
        d = [a - b for a, b in zip(d, ded)]
        ds = _split_all(d)
        s *= 2
    out.extend(ds)


def _in_proj_kernel(x_ref, g_ref, wt_ref, proj_ref, gt_ref):
    hb = _rms(x_ref[...], g_ref[...]).astype(BF16)
    proj_ref[...] = _dot_nt(hb, wt_ref[...])
    gt = proj_ref[:, N_PROJ - LANES:N_PROJ].T
    for j in range(gt_ref.shape[0]):
        gt_ref[j] = gt[0:N_GATE, CHUNK * j:CHUNK * (j + 1)]


def _layer_spec(layer, *shape):
    zeros = (0,) * len(shape)
    return pl.BlockSpec((None,) + shape, lambda *_: (layer,) + zeros)


def _in_proj(x, g, w, layer):
    rows = x.shape[0]
    tm = ROW_TILE
    return pl.pallas_call(
        _in_proj_kernel,
        grid=(rows // tm,),
        in_specs=[
            pl.BlockSpec((tm, D_MODEL), lambda i: (i, 0)),
            _layer_spec(layer, 1, D_MODEL),
            _layer_spec(layer, N_PROJ, D_MODEL),
        ],
        out_specs=[
            pl.BlockSpec((tm, N_PROJ), lambda i: (i, 0)),
            pl.BlockSpec((tm // CHUNK, N_GATE, CHUNK), lambda i: (i, 0, 0)),
        ],
        out_shape=[jax.ShapeDtypeStruct((rows, N_PROJ), F32),
                   jax.ShapeDtypeStruct((rows // CHUNK, N_GATE, CHUNK), F32)],
        compiler_params=_cparams(1),
        name="in_proj",
    )(x, g, w)


def _chunk_masks(seq_len):
    c = CHUNK
    ri = lax.broadcasted_iota(jnp.int32, (c, c), 0)
    ci = lax.broadcasted_iota(jnp.int32, (c, c), 1)
    if seq_len < c:
        sh = seq_len.bit_length() - 1
        same = (ri >> sh) == (ci >> sh)
    else:
        same = ri >= 0
    return ri, ci, same


def _gdn_prep_kernel(*refs, tb, seq_len, valid_lo, idt):
    nk = tb // CHUNK
    groups = PREP_GROUPS if nk % PREP_GROUPS == 0 else 1
    gens = [_gdn_prep_group(refs[:-5], refs[-5:], g * (nk // groups), nk // groups, seq_len, valid_lo, idt)
            for g in range(groups)]
    _run_skewed(gens, PREP_SKEW_STAGES)


def _run_skewed(gens, skew):
    live = []
    pending = list(gens)
    tick = 0
    while live or pending:
        if pending and tick % skew == 0:
            live.append(pending.pop(0))
        for g in list(live):
            try:
                next(g)
            except StopIteration:
                live.remove(g)
        tick += 1


def _gdn_prep_group(ins, outs, k0, nk, seq_len, valid_lo, idt):
    has_cprev = valid_lo > 0
    if has_cprev:
        u_ref, prev_ref, gcol_ref, grow_ref, cprev_ref, cw_ref, pa_ref, pl_ref, ra_ref, rl_ref = ins
    else:
        u_ref, prev_ref, gcol_ref, grow_ref, cw_ref, pa_ref, pl_ref, ra_ref, rl_ref = ins
    wkqg_ref, wv_ref, qk_ref, kdt_ref, el_ref = outs
    c = CHUNK
    sh = seq_len.bit_length() - 1
    r_lo, nrows = k0 * c, nk * c
    u = u_ref[0, r_lo:r_lo + nrows, :]
    rowpos = lax.broadcasted_iota(jnp.int32, (nrows, 1), 0) & (seq_len - 1)
    if has_cprev:
        u = jnp.where((rowpos >= valid_lo - (CONV_W - 1)) & (rowpos < valid_lo),
                      cprev_ref[0, r_lo:r_lo + nrows, :], u)
    if k0 == 0:
        prev = jnp.where(pl.program_id(1) == 0, 0.0, prev_ref[0])
    else:
        prev = u_ref[0, r_lo - 8:r_lo, :]
    xp = jnp.concatenate([prev, u], axis=0)
    cw = cw_ref[...]
    ys = []
    for part in range(3):
        cs = slice(part * N_HEADS * DH, (part + 1) * N_HEADS * DH)
        y = None
        for i in range(CONV_W):
            s = CONV_W - 1 - i
            xs = u[:, cs] if s == 0 else pltpu.roll(xp[:, cs], s, 0)[8:8 + nrows]
            t = xs * cw[i:i + 1, cs]
            y = t if y is None else y + t
        ys.append(jax.nn.silu(y))
        yield
    yq, yk, yv = ys

    lane = lax.broadcasted_iota(jnp.int32, (1, LANES), 1)
    xg = gcol_ref[0, r_lo:r_lo + nrows, :] + pa_ref[0:1, :]
    gcolv = jnp.where(lane < N_HEADS, jax.nn.sigmoid(xg), -jnp.exp(pl_ref[0:1, :]) * _softplus(xg))
    if valid_lo > 0:
        gcolv = jnp.where(rowpos >= valid_lo, gcolv, 0.0)

    ri, ci, same = _chunk_masks(seq_len)
    causal = (ri >= ci) & same
    upper = (ri <= ci) & same
    colpos = lax.broadcasted_iota(jnp.int32, (1, c), 1) & (seq_len - 1)
    r8 = lax.broadcasted_iota(jnp.int32, (8, c), 0)
    c8 = lax.broadcasted_iota(jnp.int32, (8, c), 1)

    items = []
    for kk in range(nk):
        r0 = kk * c
        k = k0 + kk
        growv = -jnp.exp(rl_ref[:, 0:1]) * _softplus(grow_ref[k] + ra_ref[:, 0:1])
        if valid_lo > 0:
            growv = jnp.where(colpos >= valid_lo, growv, 0.0)
        for h in range(N_HEADS):
            hs = slice(DH * h, DH * (h + 1))
            qc = yq[r0:r0 + c, hs]
            kc = yk[r0:r0 + c, hs]
            g_col = gcolv[r0:r0 + c, N_HEADS + h:N_HEADS + h + 1]
            g_row = growv[N_HEADS + h:N_HEADS + h + 1, :]
            items.append(dict(
                k=k, h=h, hs=hs, qc=qc, kc=kc, vc=yv[r0:r0 + c, hs],
                beta=gcolv[r0:r0 + c, h:h + 1],
                qss=jnp.sum(qc * qc, axis=-1, keepdims=True),
                kss=jnp.sum(kc * kc, axis=-1, keepdims=True),
                gc_col=jnp.sum(jnp.where(causal, g_row, 0.0), axis=1, keepdims=True),
                gc_row=jnp.sum(jnp.where(upper, g_col, 0.0), axis=0, keepdims=True),
                gl_col=jnp.sum(jnp.where(same, g_row, 0.0), axis=1, keepdims=True),
                gl8=jnp.sum(jnp.where((c8 >> sh) == r8, g_row, 0.0), axis=1, keepdims=True)))
    yield
    for it in items:
        k, hs, gc_col, beta = it["k"], it["hs"], it["gc_col"], it["beta"]
        qn = it["qc"] * lax.rsqrt(it["qss"] + EPS) * (DH ** -0.5)
        kn = it["kc"] * lax.rsqrt(it["kss"] + EPS)
        wkqg_ref[0, k, c:2 * c, hs] = (qn * jnp.exp(gc_col)).astype(idt)
        el_ref[0, k, :, hs] = jnp.broadcast_to(jnp.exp(it["gl8"]), (8, DH))
        it["kd"] = kn * jnp.exp(it["gl_col"] - gc_col)
        it["qb"] = qn.astype(BF16)
        it["kb"] = kn.astype(BF16)
        it["rhs"] = jnp.concatenate([beta * it["vc"], (beta * jnp.exp(gc_col)) * kn], axis=1)
    yield
    for a, b in zip(items[0::2], items[1::2]):
        kdt_ref[0, a["k"], a["h"] // 2] = jnp.concatenate([a["kd"], b["kd"]], axis=0).T.astype(idt)
    pm = _PairMasks(seq_len)
    zb = jnp.zeros((c, DH), BF16)
    pairs = []
    for a, b in zip(items[0::2], items[1::2]):
        gcp = jnp.where(pm.lo, a["gc_col"], b["gc_col"])
        grp = jnp.concatenate([a["gc_row"], b["gc_row"]], axis=1)
        decay = jnp.exp(jnp.where(pm.causal, gcp - grp, 0.0)) * pm.causal_f
        kbd = jnp.concatenate([jnp.concatenate([a["kb"], zb], axis=1),
                               jnp.concatenate([zb, b["kb"]], axis=1)], axis=0)
        kq = jnp.concatenate([jnp.concatenate([a["kb"], b["kb"]], axis=1),
                              jnp.concatenate([a["qb"], b["qb"]], axis=1)], axis=0)
        pairs.append(dict(a=a, b=b, decay=decay, kbd=kbd, kq=kq, beta=jnp.where(pm.lo, a["beta"], b["beta"])))
    kqs = [_dot_nt(p["kq"], p["kbd"]) for p in pairs]
    yield
    for p, kq in zip(pairs, kqs):
        qk_ref[0, p["a"]["k"], p["a"]["h"] // 2] = (kq[c:2 * c] * p["decay"]).astype(idt)
    ls = [p["beta"] * kq[0:c] * p["decay"] * pm.strict_f for p, kq in zip(pairs, kqs)]
    tinvs = []
    yield from _unit_lower_inverse_pairs(ls, pm, seq_len, tinvs)
    for p, (th, tl) in zip(pairs, tinvs):
        rh, rl_ = _split(jnp.concatenate([p["a"]["rhs"], p["b"]["rhs"]], axis=0))
        ta_h, tb_h = th * pm.lo_b, th * pm.hi_b
        r1 = _dot(jnp.concatenate([ta_h, tl * pm.lo_b, tb_h, tl * pm.hi_b], axis=0), rh)
        r2 = _dot(jnp.concatenate([ta_h, tb_h], axis=0), rl_)
        p["sol"] = (r1[0:c] + (r2[0:c] + r1[c:2 * c]), r1[2 * c:3 * c] + (r2[c:2 * c] + r1[3 * c:4 * c]))
    yield
    for p in pairs:
        for it, sol in zip((p["a"], p["b"]), p["sol"]):
            wv_ref[0, it["k"], :, it["hs"]] = sol[:, 0:DH]
            wkqg_ref[0, it["k"], 0:c, it["hs"]] = sol[:, DH:2 * DH].astype(idt)


def _gdn_prep(proj3, grow, cprev, cw, pa, pl_, ra, rl, layer, *, seq_len, valid_lo, idt, tb):
    nb, t, _ = proj3.shape
    c = CHUNK
    nt = t // tb
    kb = tb // c
    has_cprev = valid_lo > 0
    in_specs = [
        pl.BlockSpec((1, tb, D_CONV), lambda b, i: (b, i, 0)),
        pl.BlockSpec((1, 8, D_CONV), lambda b, i: (b, jnp.maximum(i * (tb // 8) - 1, 0), 0)),
        pl.BlockSpec((1, tb, LANES), lambda b, i: (b, i, GATE_COL_BLOCK)),
        pl.BlockSpec((kb, N_GATE, c), lambda b, i: (b * nt + i, 0, 0)),
    ]
    args = [proj3, proj3, proj3, grow]
    if has_cprev:
        assert nb == 1
        in_specs.append(pl.BlockSpec((1, tb, D_CONV), lambda b, i: (layer, i, 0)))
        args.append(cprev)
    in_specs += [
        _layer_spec(layer, CONV_W, D_CONV),
        _layer_spec(layer, 8, LANES),
        _layer_spec(layer, 8, LANES),
        _layer_spec(layer, N_GATE, LANES),
        _layer_spec(layer, N_GATE, LANES),
    ]
    args += [cw, pa, pl_, ra, rl]
    n = t // c
    hd = N_HEADS * DH
    out_shape = [
        jax.ShapeDtypeStruct((nb, n, 2 * c, hd), idt),
        jax.ShapeDtypeStruct((nb, n, c, hd), F32),
        jax.ShapeDtypeStruct((nb, n, N_HEADS // 2, c, 2 * c), idt),
        jax.ShapeDtypeStruct((nb, n, N_HEADS // 2, DH, 2 * c), idt),
        jax.ShapeDtypeStruct((nb, n, 8, hd), F32),
    ]
    out_specs = [
        pl.BlockSpec((1, kb, 2 * c, hd), lambda b, i: (b, i, 0, 0)),
        pl.BlockSpec((1, kb, c, hd), lambda b, i: (b, i, 0, 0)),
        pl.BlockSpec((1, kb, N_HEADS // 2, c, 2 * c), lambda b, i: (b, i, 0, 0, 0)),
        pl.BlockSpec((1, kb, N_HEADS // 2, DH, 2 * c), lambda b, i: (b, i, 0, 0, 0)),
        pl.BlockSpec((1, kb, 8, hd), lambda b, i: (b, i, 0, 0)),
    ]
    return pl.pallas_call(
        functools.partial(_gdn_prep_kernel, tb=tb, seq_len=seq_len, valid_lo=valid_lo, idt=idt),
        grid=(nb, nt),
        in_specs=in_specs,
        out_specs=out_specs,
        out_shape=out_shape,
        compiler_params=_cparams(2),
        name="gdn_prep",
    )(*args)


def _lane_half_masks(dtype):
    lo = lax.broadcasted_iota(jnp.int32, (CHUNK, 2 * CHUNK), 1) < CHUNK
    return jnp.where(lo, 1.0, 0.0).astype(dtype), jnp.where(lo, 0.0, 1.0).astype(dtype)


def _mean_square_lanes(x):
    rows = x.shape[0]
    ssq = _dot(jnp.concatenate(_split(x * x), axis=0), jnp.ones((DH, DH), BF16))
    return (ssq[0:rows] + ssq[rows:2 * rows]) * (1.0 / DH)


def _gated_norm_store(o, z, gn, mix_ref, idx):
    mix_ref[idx] = (_rms(o, gn) * jax.nn.silu(z)).astype(BF16)


def _gdn_scan_prompt_kernel(wkqg_ref, wv_ref, qk_ref, kdt_ref, el_ref, z_ref, gn_ref,
                            mix_ref, sout_ref, s_scr, *, nc, bs):
    c = CHUNK

    @pl.when(pl.program_id(1) == 0)
    def _():
        s_scr[...] = jnp.zeros_like(s_scr)

    gn = gn_ref[...]
    units = [(b, h, slice(DH * h, DH * (h + 1))) for b in range(bs) for h in range(N_HEADS)]
    half_b = _lane_half_masks(BF16)
    half2_b = tuple(jnp.concatenate([m, m], axis=0) for m in half_b)

    def body(n, carry):
        rows = pl.ds(pl.multiple_of(n * c, c), c)
        ss = [s_scr[b * N_HEADS + h] for b, h, _ in units]
        aa = [_dot(wkqg_ref[b, n, :, hs], s.astype(BF16)) for (b, _, hs), s in zip(units, ss)]
        ubs = [(wv_ref[b, n, :, hs] - a[0:c]).astype(BF16) for (b, _, hs), a in zip(units, aa)]
        os_, ds = [], []
        for j in range(len(units) // 2):
            b, h, _ = units[2 * j]
            qkp, kdp = qk_ref[b, n, h // 2], kdt_ref[b, n, h // 2]
            r = _dot(jnp.concatenate([qkp * half_b[0], qkp * half_b[1], kdp * half2_b[0], kdp * half2_b[1]],
                                     axis=0),
                     jnp.concatenate(ubs[2 * j:2 * j + 2], axis=0))
            os_ += [r[0:c], r[c:2 * c]]
            ds += [r[2 * c:2 * c + DH], r[2 * c + DH:2 * c + 2 * DH]]
        for i, (b, h, hs) in enumerate(units):
            s_scr[b * N_HEADS + h] = el_ref[b, n, 0:1, hs] * ss[i] + ds[i]
            _gated_norm_store(aa[i][c:2 * c] + os_[i], z_ref[b, rows, hs], gn, mix_ref, (b, rows, hs))
        return carry

    lax.fori_loop(0, nc, body, 0)

    @pl.when(pl.program_id(1) == pl.num_programs(1) - 1)
    def _():
        for b in range(bs):
            sout_ref[b] = s_scr[b * N_HEADS:(b + 1) * N_HEADS]


def _gdn_scan_prompt(wkqg, wv, qk, kdt, el, proj3, gn, layer, *, ts, bs):
    nb, n, _, hd = wkqg.shape
    c = CHUNK
    t = n * c
    nc = ts // c
    zblk = D_CONV // hd
    return pl.pallas_call(
        functools.partial(_gdn_scan_prompt_kernel, nc=nc, bs=bs),
        grid=(nb // bs, t // ts),
        in_specs=[
            pl.BlockSpec((bs, nc, 2 * c, hd), lambda b, i: (b, i, 0, 0)),
            pl.BlockSpec((bs, nc, c, hd), lambda b, i: (b, i, 0, 0)),
            pl.BlockSpec((bs, nc, N_HEADS // 2, c, 2 * c), lambda b, i: (b, i, 0, 0, 0)),
            pl.BlockSpec((bs, nc, N_HEADS // 2, DH, 2 * c), lambda b, i: (b, i, 0, 0, 0)),
            pl.BlockSpec((bs, nc, 8, hd), lambda b, i: (b, i, 0, 0)),
            pl.BlockSpec((bs, ts, hd), lambda b, i: (b, i, zblk)),
            _layer_spec(layer, 1, DH),
        ],
        out_specs=[
            pl.BlockSpec((bs, ts, hd), lambda b, i: (b, i, 0)),
            pl.BlockSpec((bs, N_HEADS, DH, DH), lambda b, i: (b, 0, 0, 0)),
        ],
        out_shape=[jax.ShapeDtypeStruct((nb, t, hd), BF16),
                   jax.ShapeDtypeStruct((nb, N_HEADS, DH, DH), F32)],
        scratch_shapes=[pltpu.VMEM((bs * N_HEADS, DH, DH), F32)],
        compiler_params=_cparams(2),
        name="gdn_scan_prompt",
    )(wkqg, wv, qk, kdt, el, proj3, gn)


def _gdn_scan_decode_kernel(*refs, n_prev):
    (wkqg_ref, wv_ref, qk_ref, kdt_ref, el_ref, z_ref, gn_ref, s0_ref), rest = refs[:8], refs[8:]
    prev_refs, (mix_ref, sout_ref) = rest[:n_prev], rest[n_prev:]
    if n_prev:
        for l, p_ref in enumerate(prev_refs):
            sout_ref[l] = p_ref[...]
        sout_ref = sout_ref.at[n_prev]
    c = CHUNK
    nseq = c // SLOT
    rowseq = lax.broadcasted_iota(jnp.int32, (c, 1), 0) >> 3
    gn = gn_ref[...]
    heads = [slice(DH * h, DH * (h + 1)) for h in range(N_HEADS)]
    aa = []
    for h, hs in enumerate(heads):
        w = wkqg_ref[0, 0, :, hs]
        row = []
        for j in range(nseq):
            wj = jnp.concatenate([w[SLOT * j:SLOT * (j + 1)], w[c + SLOT * j:c + SLOT * (j + 1)]],
                                 axis=0).astype(BF16)
            row.append(_dot(wj, s0_ref[j, h].astype(BF16)))
        aa.append(row)
    us = [wv_ref[0, 0, :, hs] - jnp.concatenate([a[0:SLOT] for a in aa[h]], axis=0)
          for h, hs in enumerate(heads)]
    half_f = _lane_half_masks(F32)
    os_ = []
    for j in range(N_HEADS // 2):
        qkp = qk_ref[0, 0, j]
        o2 = _dot(jnp.concatenate([qkp * half_f[0], qkp * half_f[1]], axis=0).astype(BF16),
                  jnp.concatenate(us[2 * j:2 * j + 2], axis=0).astype(BF16))
        os_ += [o2[0:c], o2[c:2 * c]]
    half2_f = tuple(jnp.concatenate([m, m], axis=0) for m in half_f)
    for p in range(N_HEADS // 2):
        kdp = kdt_ref[0, 0, p]
        kd2 = jnp.concatenate([kdp * half2_f[0], kdp * half2_f[1]], axis=0).astype(BF16)
        for j in range(nseq):
            uj = jnp.concatenate([jnp.where(rowseq == j, us[2 * p + i], 0.0) for i in range(2)], axis=0)
            r = _dot(kd2, uj.astype(BF16))
            for i in range(2):
                h = 2 * p + i
                sout_ref[j, h] = (el_ref[0, 0, j:j + 1, heads[h]] * s0_ref[j, h] + r[DH * i:DH * (i + 1)])
    for h, hs in enumerate(heads):
        o = jnp.concatenate([a[SLOT:2 * SLOT] for a in aa[h]], axis=0) + os_[h]
        _gated_norm_store(o, z_ref[:, hs], gn, mix_ref, (slice(None), hs))


def _state_specs(layer, prev, state_shape, nseq):
    tail = state_shape[2:]
    zeros = (0,) * len(tail)
    in_specs = [pl.BlockSpec((None, nseq) + tail, lambda i: (layer, i) + zeros)]
    in_specs += [pl.BlockSpec((nseq,) + tail, lambda i: (i,) + zeros) for _ in prev]
    if prev:
        out_spec = pl.BlockSpec((len(prev) + 1, nseq) + tail, lambda i: (0, i) + zeros)
        out_shape = jax.ShapeDtypeStruct((len(prev) + 1,) + state_shape[1:], F32)
    else:
        out_spec = pl.BlockSpec((nseq,) + tail, lambda i: (i,) + zeros)
        out_shape = jax.ShapeDtypeStruct(state_shape[1:], F32)
    return in_specs, out_spec, out_shape


def _gdn_scan_decode(wkqg, wv, qk, kdt, el, proj, gn, s0_all, prev, layer):
    _, n, _, hd = wkqg.shape
    c = CHUNK
    nseq = c // SLOT
    zblk = D_CONV // hd
    st_in, st_out, st_shape = _state_specs(layer, prev, s0_all.shape, nseq)
    return pl.pallas_call(
        functools.partial(_gdn_scan_decode_kernel, n_prev=len(prev)),
        grid=(n,),
        in_specs=[
            pl.BlockSpec((1, 1, 2 * c, hd), lambda i: (0, i, 0, 0)),
            pl.BlockSpec((1, 1, c, hd), lambda i: (0, i, 0, 0)),
            pl.BlockSpec((1, 1, N_HEADS // 2, c, 2 * c), lambda i: (0, i, 0, 0, 0)),
            pl.BlockSpec((1, 1, N_HEADS // 2, DH, 2 * c), lambda i: (0, i, 0, 0, 0)),
            pl.BlockSpec((1, 1, 8, hd), lambda i: (0, i, 0, 0)),
            pl.BlockSpec((c, hd), lambda i: (i, zblk)),
            _layer_spec(layer, 1, DH),
        ] + st_in,
        out_specs=[pl.BlockSpec((c, hd), lambda i: (i, 0)), st_out],
        out_shape=[jax.ShapeDtypeStruct((n * c, hd), BF16), st_shape],
        compiler_params=_cparams(1),
        name="gdn_scan_decode",
    )(wkqg, wv, qk, kdt, el, proj, gn, s0_all, *prev)


def _mlstm_gate_blocks(gcol, growk, pa_ref, ra_ref):
    xg = gcol + pa_ref[0:1, :]
    xr = growk + ra_ref[:, 0:1]
    return xg, -_softplus(-xg), xr, -_softplus(-xr)


def _mlstm_gates(blocks, h, valid_lo, rowpos, colpos):
    xg, lg, xr, lr = blocks
    ig_col = xg[:, 2 * N_HEADS + h:2 * N_HEADS + h + 1]
    fl_col = lg[:, 3 * N_HEADS + h:3 * N_HEADS + h + 1]
    ig_row = xr[2 * N_HEADS + h:2 * N_HEADS + h + 1, :]
    fl_row = lr[3 * N_HEADS + h:3 * N_HEADS + h + 1, :]
    if valid_lo > 0:
        ig_col = jnp.where(rowpos >= valid_lo, ig_col, NEG)
        fl_col = jnp.where(rowpos >= valid_lo, fl_col, 0.0)
        ig_row = jnp.where(colpos >= valid_lo, ig_row, NEG)
        fl_row = jnp.where(colpos >= valid_lo, fl_row, 0.0)
    return ig_col, fl_col, ig_row, fl_row


def _mlstm_chunks(probs, seq_len, finish):
    c = CHUNK
    nseq = c // seq_len
    ri, ci, same = _chunk_masks(seq_len)
    causal = (ri >= ci) & same
    causal_f = jnp.where(causal, 1.0, 0.0)
    upper = (ri <= ci) & same
    seq_end = ci == (ri | (seq_len - 1))
    rowseq = lax.broadcasted_iota(jnp.int32, (c, 1), 0) >> (seq_len.bit_length() - 1)
    zpad = jnp.zeros((16 - seq_len, DH), F32) if nseq > 1 else None
    ones_b = jnp.ones((c, LANES), BF16)
    for p in probs:
        p["f_row"] = jnp.sum(jnp.where(upper, p["fl_col"], 0.0), axis=0, keepdims=True)
        p["km"] = p["k"] * (DH ** -0.5)
        p["qb"] = p["q"].astype(BF16)
        p["kb"] = p["km"].astype(BF16)
        p["vb"] = p["v"].astype(BF16)
        if nseq == 1:
            r = _dot(jnp.concatenate(_split(jnp.where(causal, p["fl_row"], 0.0)), axis=0), ones_b)
            p["f_col"] = r[0:c] + r[c:2 * c]
            nb2 = p["n_rows"].astype(BF16)
            p["qn"] = _dot_nt(p["qb"], jnp.concatenate([nb2, nb2], axis=0))
            p["ig_col"] = jnp.broadcast_to(p["ig_col"], (c, LANES))
        else:
            p["f_col"] = jnp.sum(jnp.where(causal, p["fl_row"], 0.0), axis=1, keepdims=True)
            p["qn"] = jnp.sum(p["q"] * p["n_rows"], axis=1, keepdims=True)
    yield
    for p in probs:
        p["b_col"] = p["ig_col"] - p["f_col"]
        p["b_row"] = p["ig_row"] - p["f_row"]
        p["bmax_col"] = jnp.max(jnp.where(causal, p["b_row"], -jnp.inf), axis=1, keepdims=True)
        if nseq == 1:
            p["bmax_col"] = jnp.broadcast_to(p["bmax_col"], (c, LANES))
        if nseq > 1:
            p["fl_end"] = jnp.sum(jnp.where(seq_end, p["f_row"], 0.0), axis=1, keepdims=True)
            p["bmax_end"] = jnp.max(jnp.where(same, p["b_row"], -jnp.inf), axis=1, keepdims=True)
    yield
    for p in probs:
        f_col, mp_col = p["f_col"], p["mp_col"]
        m_col = f_col + jnp.maximum(mp_col, p["bmax_col"])
        p["m_col"] = m_col
        p["a_col"] = jnp.exp(f_col + mp_col - m_col)
        fm = (f_col - m_col)[:, 0:c] if nseq == 1 else f_col - m_col
        p["dexp"] = jnp.exp(jnp.where(causal, fm + p["b_row"], 0.0)) * causal_f
        if nseq == 1:
            fl_end, ml, mp = f_col[c - 1:c], m_col[c - 1:c], mp_col[c - 1:c]
            p["ml_col"] = ml
            p["al"] = jnp.exp(fl_end + mp - ml)
        else:
            fl_end, mp = p["fl_end"], mp_col
            ml = fl_end + jnp.maximum(mp_col, p["bmax_end"])
            p["ml_col"] = ml
            p["al"] = jnp.broadcast_to(jnp.exp(fl_end + mp - ml), (c, DH))
        p["kw"] = p["km"] * jnp.exp(fl_end + p["b_col"] - ml)
    yield
    for p in probs:
        p["kwt"] = p["kw"].T.astype(BF16)
    for p in probs:
        p["qk"] = _dot_nt(p["qb"], p["kb"])
    yield
    for p in probs:
        if nseq == 1:
            p["qc"] = _dot(p["qb"], p["c_list"][0].astype(BF16))
            p["upd"] = [_dot(p["kwt"], p["vb"])]
        else:
            parts, upd = [], []
            for j in range(nseq):
                qj = jnp.concatenate([p["q"][seq_len * j:seq_len * (j + 1)], zpad], axis=0).astype(BF16)
                parts.append(_dot(qj, p["c_list"][j].astype(BF16))[0:seq_len])
                upd.append(_dot(p["kwt"], jnp.where(rowseq == j, p["v"], 0.0).astype(BF16)))
            p["qc"] = jnp.concatenate(parts, axis=0)
            p["upd"] = upd
    yield
    if nseq == 1:
        for p in probs:
            p["w_hl"] = _split(p["dexp"] * p["qk"])
        yield
        for p in probs:
            r = _dot(jnp.concatenate(p["w_hl"], axis=0), jnp.concatenate([p["vb"], ones_b], axis=1))
            p["wv"] = r[0:c, 0:DH]
            p["wsum"] = r[0:c, DH:DH + LANES] + r[c:2 * c, DH:DH + LANES]
    else:
        for p in probs:
            p["w"] = p["dexp"] * p["qk"]
            p["wsum"] = jnp.sum(p["w"], axis=1, keepdims=True)
        for p in probs:
            p["wv"] = _dot(p["w"].astype(BF16), p["vb"])
    yield
    for p in probs:
        num = p["a_col"] * p["qc"] + p["wv"]
        den = p["a_col"] * p["qn"] + p["wsum"]
        p["hout"] = num / jnp.maximum(jnp.abs(den), jnp.exp(-p["m_col"]))
        p["new_c"] = [p["al"][seq_len * j:seq_len * j + 1] * p["c_list"][j] + p["upd"][j]
                      for j in range(nseq)]
    yield
    finish(probs)


def _mlstm_out_store(hout, og, gnorm, mix_ref, idx):
    x = hout * jax.nn.sigmoid(og)
    mix_ref[idx] = (x * lax.rsqrt(_mean_square_lanes(x) + EPS) * gnorm).astype(BF16)


def _mlstm_prompt_kernel(q_ref, k_ref, v_ref, o_ref, gcol_ref, grow_ref, pa_ref, ra_ref, nrm_ref,
                         mix_ref, cout_ref, nout_ref, mout_ref, c_scr, n_scr, m_scr, *, nc, bs):
    c = CHUNK

    @pl.when(pl.program_id(1) == 0)
    def _():
        c_scr[...] = jnp.zeros_like(c_scr)
        n_scr[...] = jnp.zeros_like(n_scr)
        m_scr[...] = jnp.zeros_like(m_scr)

    def body(n, carry):
        r0 = pl.multiple_of(n * c, c)
        rows = pl.ds(r0, c)
        def finish(probs):
            for p in probs:
                st = p["st"]
                c_scr[st] = p["new_c"][0]
                n_scr[st] = jnp.broadcast_to(
                    p["al"][0:1] * p["n_rows"][0:1] + jnp.sum(p["kw"], axis=0, keepdims=True), (c, DH))
                m_scr[st] = jnp.broadcast_to(p["ml_col"], (c, DH))
                _mlstm_out_store(p["hout"], o_ref[p["b"], rows, p["hs"]], nrm_ref[:, p["hs"]], mix_ref,
                                 (p["b"], rows, p["hs"]))

        probs = []
        for b in range(bs):
            blocks = _mlstm_gate_blocks(gcol_ref[b, rows, :], grow_ref[b, n], pa_ref, ra_ref)
            for h in range(N_HEADS):
                hs = slice(DH * h, DH * (h + 1))
                ig_col, fl_col, ig_row, fl_row = _mlstm_gates(blocks, h, 0, None, None)
                st = b * N_HEADS + h
                probs.append(dict(b=b, hs=hs, st=st, q=q_ref[b, rows, hs], k=k_ref[b, rows, hs],
                                  v=v_ref[b, rows, hs], ig_col=ig_col, fl_col=fl_col, ig_row=ig_row,
                                  fl_row=fl_row, mp_col=m_scr[st], n_rows=n_scr[st],
                                  c_list=[c_scr[st]]))
        _run_skewed([_mlstm_chunks(probs, c, finish)], 1)
        return carry

    lax.fori_loop(0, nc, body, 0)

    @pl.when(pl.program_id(1) == pl.num_programs(1) - 1)
    def _():
        lane = lax.broadcasted_iota(jnp.int32, (8, DH), 1)
        for b in range(bs):
            mo = jnp.zeros((8, DH), F32)
            for h in range(N_HEADS):
                st = b * N_HEADS + h
                cout_ref[b, h] = c_scr[st]
                nout_ref[b, h:h + 1, :] = n_scr[st][0:1]
                mo = jnp.where(lane == h, m_scr[st][0:8], mo)
            mout_ref[b] = mo


def _mlstm_prompt(proj3, grow4, pa, ra, nrm, layer, *, ts, bs):
    nb, t, _ = proj3.shape
    c = CHUNK
    hd = N_HEADS * DH
    nc = ts // c
    qblk = (D_CONV + hd) // hd
    return pl.pallas_call(
        functools.partial(_mlstm_prompt_kernel, nc=nc, bs=bs),
        grid=(nb // bs, t // ts),
        in_specs=[
            pl.BlockSpec((bs, ts, hd), lambda b, i: (b, i, qblk)),
            pl.BlockSpec((bs, ts, hd), lambda b, i: (b, i, qblk + 1)),
            pl.BlockSpec((bs, ts, hd), lambda b, i: (b, i, qblk + 2)),
            pl.BlockSpec((bs, ts, hd), lambda b, i: (b, i, qblk + 3)),
            pl.BlockSpec((bs, ts, LANES), lambda b, i: (b, i, GATE_COL_BLOCK)),
            pl.BlockSpec((bs, nc, N_GATE, c), lambda b, i: (b, i, 0, 0)),
            _layer_spec(layer, 8, LANES),
            _layer_spec(layer, N_GATE, LANES),
            _layer_spec(layer, 1, hd),
        ],
        out_specs=[
            pl.BlockSpec((bs, ts, hd), lambda b, i: (b, i, 0)),
            pl.BlockSpec((bs, N_HEADS, DH, DH), lambda b, i: (b, 0, 0, 0)),
            pl.BlockSpec((bs, N_HEADS, DH), lambda b, i: (b, 0, 0)),
            pl.BlockSpec((bs, 8, DH), lambda b, i: (b, 0, 0)),
        ],
        out_shape=[jax.ShapeDtypeStruct((nb, t, hd), BF16),
                   jax.ShapeDtypeStruct((nb, N_HEADS, DH, DH), F32),
                   jax.ShapeDtypeStruct((nb, N_HEADS, DH), F32),
                   jax.ShapeDtypeStruct((nb, 8, DH), F32)],
        scratch_shapes=[pltpu.VMEM((bs * N_HEADS, DH, DH), F32),
                        pltpu.VMEM((bs * N_HEADS, c, DH), F32),
                        pltpu.VMEM((bs * N_HEADS, c, DH), F32)],
        compiler_params=_cparams(2),
        name="mlstm_prompt",
    )(proj3, proj3, proj3, proj3, proj3, grow4, pa, ra, nrm)


def _mlstm_decode_kernel(*refs, n_prev):
    (q_ref, k_ref, v_ref, o_ref, gcol_ref, grow_ref, pa_ref, ra_ref, nrm_ref, n0_ref, m0_ref,
     c0_ref), rest = refs[:12], refs[12:]
    prev_refs, (mix_ref, nout_ref, mout_ref, cout_ref) = rest[:n_prev], rest[n_prev:]
    if n_prev:
        for l, p_ref in enumerate(prev_refs):
            cout_ref[l] = p_ref[...]
        cout_ref = cout_ref.at[n_prev]
    c = CHUNK
    nseq = c // SLOT
    nck = q_ref.shape[0] // c
    rowpos = lax.broadcasted_iota(jnp.int32, (c, 1), 0) & (SLOT - 1)
    colpos = lax.broadcasted_iota(jnp.int32, (1, c), 1) & (SLOT - 1)
    lane = lax.broadcasted_iota(jnp.int32, (c, DH), 1)
    probs = []
    for ck in range(nck):
        rows = slice(c * ck, c * (ck + 1))
        blocks = _mlstm_gate_blocks(gcol_ref[rows, :], grow_ref[ck], pa_ref, ra_ref)
        for h in range(N_HEADS):
            hs = slice(DH * h, DH * (h + 1))
            ig_col, fl_col, ig_row, fl_row = _mlstm_gates(blocks, h, SLOT_PAD, rowpos, colpos)
            seqs = [nseq * ck + j for j in range(nseq)]
            n_rows = jnp.concatenate(
                [jnp.broadcast_to(n0_ref[s, h:h + 1, :], (SLOT, DH)) for s in seqs], axis=0)
            probs.append(dict(ck=ck, rows=rows, seqs=seqs, h=h, hs=hs, q=q_ref[rows, hs], k=k_ref[rows, hs],
                              v=v_ref[rows, hs], ig_col=ig_col, fl_col=fl_col, ig_row=ig_row, fl_row=fl_row,
                              mp_col=m0_ref[rows, h:h + 1], n_rows=n_rows,
                              c_list=[c0_ref[s, h] for s in seqs]))

    def finish(probs):
        mos = [jnp.zeros((c, DH), F32) for _ in range(nck)]
        for p in probs:
            h, hs, rows = p["h"], p["hs"], p["rows"]
            for j, s in enumerate(p["seqs"]):
                cout_ref[s, h] = p["new_c"][j]
                rs = slice(SLOT * j, SLOT * (j + 1))
                nout_ref[s, h:h + 1, :] = (p["al"][SLOT * j:SLOT * j + 1] * n0_ref[s, h:h + 1, :]
                                           + jnp.sum(p["kw"][rs], axis=0, keepdims=True))
            mos[p["ck"]] = jnp.where(lane == h, jnp.broadcast_to(p["ml_col"], (c, DH)), mos[p["ck"]])
            _mlstm_out_store(p["hout"], o_ref[rows, hs], nrm_ref[:, hs], mix_ref, (rows, hs))
        for ck in range(nck):
            mout_ref[c * ck:c * (ck + 1), :] = mos[ck]

    _run_skewed([_mlstm_chunks(probs, SLOT, finish)], 1)


def _mlstm_decode(proj, grow, pa, ra, nrm, c0_all, n0_all, m0rows_all, prev, layer, *, nck):
    rows = proj.shape[0]
    c = CHUNK * nck
    hd = N_HEADS * DH
    nseq = c // SLOT
    qblk = (D_CONV + hd) // hd
    st_in, st_out, st_shape = _state_specs(layer, prev, c0_all.shape, nseq)
    return pl.pallas_call(
        functools.partial(_mlstm_decode_kernel, n_prev=len(prev)),
        grid=(rows // c,),
        in_specs=[
            pl.BlockSpec((c, hd), lambda i: (i, qblk)),
            pl.BlockSpec((c, hd), lambda i: (i, qblk + 1)),
            pl.BlockSpec((c, hd), lambda i: (i, qblk + 2)),
            pl.BlockSpec((c, hd), lambda i: (i, qblk + 3)),
            pl.BlockSpec((c, LANES), lambda i: (i, GATE_COL_BLOCK)),
            pl.BlockSpec((nck, N_GATE, CHUNK), lambda i: (i, 0, 0)),
            _layer_spec(layer, 8, LANES),
            _layer_spec(layer, N_GATE, LANES),
            _layer_spec(layer, 1, hd),
            pl.BlockSpec((None, nseq, N_HEADS, DH), lambda i: (layer, i, 0, 0)),
            pl.BlockSpec((None, c, LANES), lambda i: (layer, i, 0)),
        ] + st_in,
        out_specs=[
            pl.BlockSpec((c, hd), lambda i: (i, 0)),
            pl.BlockSpec((nseq, N_HEADS, DH), lambda i: (i, 0, 0)),
            pl.BlockSpec((c, LANES), lambda i: (i, 0)),
            st_out,
        ],
        out_shape=[jax.ShapeDtypeStruct((rows, hd), BF16),
                   jax.ShapeDtypeStruct(n0_all.shape[1:], F32),
                   jax.ShapeDtypeStruct((rows, LANES), F32),
                   st_shape],
        compiler_params=_cparams(1),
        name="mlstm_decode",
    )(proj, proj, proj, proj, proj, grow, pa, ra, nrm, n0_all, m0rows_all, c0_all, *prev)


def _post_kernel(x_ref, ma_ref, mb_ref, p_ref, woa_ref, wob_ref, gf_ref, wg_ref, wu_ref, wd_ref,
                 gp_ref, wpg_ref, wpp_ref, gfin_ref, o_ref, acc_ref, *, final):
    x = x_ref[...] + (_dot(ma_ref[...], woa_ref[...]) + _dot(mb_ref[...], wob_ref[...]))
    ub = _rms(x, gf_ref[...]).astype(BF16)
    for j in range(D_FF // FF_CHUNK):
        sl = slice(j * FF_CHUNK, (j + 1) * FF_CHUNK)
        a = (jax.nn.silu(_dot(ub, wg_ref[:, sl])) * _dot(ub, wu_ref[:, sl])).astype(BF16)
        d = _dot(a, wd_ref[sl, :])
        if j == 0:
            acc_ref[...] = d
        else:
            acc_ref[...] += d
    x = x + acc_ref[...]
    gate = jax.nn.sigmoid(_dot(_rms(x, gp_ref[...]).astype(BF16), wpg_ref[...]))
    x = x + _dot(p_ref[...].astype(BF16), wpp_ref[...]) * gate
    if final:
        x = _rms(x, gfin_ref[...])
    o_ref[...] = x


def _post(x, ma, mb, p_all, wo, gf, wg, wu, wd, gp, wpg, wpp, gfin, layer, *, final):
    rows = x.shape[0]
    tm = ROW_TILE
    hd = N_HEADS * DH
    row = lambda w: pl.BlockSpec((tm, w), lambda i: (i, 0))

    def whole(a, b, blk=0):
        return pl.BlockSpec((None, a, b), lambda i: (layer, blk, 0), pipeline_mode=pl.Buffered(1))

    return pl.pallas_call(
        functools.partial(_post_kernel, final=final),
        grid=(rows // tm,),
        in_specs=[row(D_MODEL), row(hd), row(hd),
                  pl.BlockSpec((None, tm, D_PLE), lambda i: (layer, i, 0)),
                  whole(hd, D_MODEL, 0), whole(hd, D_MODEL, 1), whole(1, D_MODEL),
                  whole(D_MODEL, D_FF), whole(D_MODEL, D_FF), whole(D_FF, D_MODEL),
                  whole(1, D_MODEL), whole(D_MODEL, D_MODEL), whole(D_PLE, D_MODEL),
                  pl.BlockSpec((1, D_MODEL), lambda i: (0, 0))],
        out_specs=row(D_MODEL),
        out_shape=jax.ShapeDtypeStruct((rows, D_MODEL), F32),
        scratch_shapes=[pltpu.VMEM((tm, D_MODEL), F32)],
        compiler_params=_cparams(1),
        name="post",
    )(x, ma, mb, p_all, wo, wo, gf, wg, wu, wd, gp, wpg, wpp, gfin)


def _gate_vec(pairs, depth):
    v = jnp.zeros((depth, N_GATE), F32)
    for off, val in pairs:
        v = v.at[:, off:off + N_HEADS].set(val.astype(F32))
    return v


def _lane_form(v):
    d = v.shape[0]
    return jnp.broadcast_to(jnp.pad(v, ((0, 0), (0, LANES - N_GATE)))[:, None, :], (d, 8, LANES))


def _row_form(v):
    d = v.shape[0]
    return jnp.broadcast_to(v[:, :, None], (d, N_GATE, LANES))


def kernel(x_prompt, x_sample, p_prompt, p_sample, state_gdn, state_gdn_conv, state_mlstm_C, state_mlstm_n, state_mlstm_m, w_in, conv_w, gdn_a_log, gdn_dt_bias, gdn_norm, mlstm_i_bias, mlstm_f_bias, mlstm_norm, w_out, norm_mix, norm_ffn, w_gate, w_up, w_down, norm_ple, w_ple_gate, w_ple_proj, norm_final):
    depth = w_in.shape[0]
    nb, t, _ = x_prompt.shape
    ns, tdec, _ = x_sample.shape
    hd = N_HEADS * DH
    c = CHUNK
    pad = SLOT - tdec

    xp = x_prompt.reshape(nb * t, D_MODEL)
    xs = jnp.pad(x_sample, ((0, 0), (pad, 0), (0, 0))).reshape(ns * SLOT, D_MODEL)
    ps_all = jnp.pad(p_sample, ((0, 0), (0, 0), (pad, 0), (0, 0))).reshape(depth, ns * SLOT, D_PLE)
    pp_all = p_prompt.reshape(depth, nb * t, D_PLE)
    gfin = norm_final.reshape(1, D_MODEL)

    o = D_CONV + hd
    wt = jnp.swapaxes(w_in, 1, 2).astype(BF16)
    w_r = jnp.concatenate([wt[:, :o], wt[:, o + 2 * N_HEADS:o + 2 * N_HEADS + 4 * hd], wt[:, o:o + 2 * N_HEADS],
                           wt[:, o + 2 * N_HEADS + 4 * hd:],
                           jnp.zeros((depth, LANES - N_GATE, D_MODEL), BF16)], axis=1)
    adds = _gate_vec([(N_HEADS, gdn_dt_bias), (2 * N_HEADS, mlstm_i_bias), (3 * N_HEADS, mlstm_f_bias)], depth)
    alog = _gate_vec([(N_HEADS, gdn_a_log)], depth)
    pa, pl_, ra, rl = _lane_form(adds), _lane_form(alog), _row_form(adds), _row_form(alog)
    g_mix = norm_mix.reshape(depth, 1, D_MODEL)
    cw = conv_w.astype(F32)
    gn = gdn_norm.reshape(depth, 1, DH).astype(F32)
    nrm = mlstm_norm.reshape(depth, 1, hd).astype(F32)
    wts = (w_out.astype(BF16), norm_ffn.reshape(depth, 1, D_MODEL), w_gate.astype(BF16), w_up.astype(BF16),
           w_down.astype(BF16), norm_ple.reshape(depth, 1, D_MODEL), w_ple_gate.astype(BF16),
           w_ple_proj.astype(BF16), gfin)
    cprev = jnp.pad(state_gdn_conv.astype(F32), ((0, 0), (0, 0), (pad - (CONV_W - 1), SLOT - pad), (0, 0)))
    cprev = cprev.reshape(depth, ns * SLOT, D_CONV)
    m0rows = jnp.pad(jnp.repeat(state_mlstm_m.astype(F32), SLOT, axis=1), ((0, 0), (0, 0), (0, LANES - N_HEADS)))
    s0_all, c0_all, n0_all = state_gdn.astype(F32), state_mlstm_C.astype(F32), state_mlstm_n.astype(F32)

    outs_p = [[] for _ in range(5)]
    outs_s = [[] for _ in range(5)]
    for i in range(depth):
        final = i == depth - 1
        proj_p, grow_p = _in_proj(xp, g_mix, w_r, i)
        proj_s, grow_s = _in_proj(xs, g_mix, w_r, i)
        proj_p3 = proj_p.reshape(nb, t, N_PROJ)
        proj_s3 = proj_s.reshape(1, ns * SLOT, N_PROJ)

        prep_p = _gdn_prep(proj_p3, grow_p, None, cw, pa, pl_, ra, rl, i, seq_len=c, valid_lo=0, idt=BF16,
                           tb=PREP_ROWS)
        mixa_p, s_p = _gdn_scan_prompt(*prep_p, proj_p3, gn, i, ts=SCAN_ROWS, bs=SCAN_SEQS)
        prep_s = _gdn_prep(proj_s3, grow_s, cprev, cw, pa, pl_, ra, rl, i, seq_len=SLOT, valid_lo=pad, idt=F32,
                           tb=DECODE_PREP_ROWS)
        mixa_s, s_s = _gdn_scan_decode(*prep_s, proj_s, gn, s0_all, outs_s[0] if final else [], i)

        mixb_p, c_p, n_p, m_p = _mlstm_prompt(proj_p3, grow_p.reshape(nb, t // c, N_GATE, c), pa, ra, nrm, i,
                                              ts=SCAN_ROWS, bs=SCAN_SEQS)
        mixb_s, n_s, m_s, c_s = _mlstm_decode(proj_s, grow_s, pa, ra, nrm, c0_all, n0_all, m0rows,
                                              outs_s[2] if final else [], i, nck=DECODE_MLSTM_CHUNKS)

        xp = _post(xp, mixa_p.reshape(nb * t, hd), mixb_p.reshape(nb * t, hd), pp_all, *wts, i, final=final)
        xs = _post(xs, mixa_s, mixb_s, ps_all, *wts, i, final=final)

        outs_p[0].append(s_p)
        outs_p[1].append(proj_p3[:, t - (CONV_W - 1):, :D_CONV])
        outs_p[2].append(c_p)
        outs_p[3].append(n_p)
        outs_p[4].append(m_p[:, 0, :N_HEADS])
        outs_s[0].append(s_s)
        outs_s[1].append(proj_s.reshape(ns, SLOT, N_PROJ)[:, SLOT - (CONV_W - 1):, :D_CONV])
        outs_s[2].append(c_s)
        outs_s[3].append(n_s)
        outs_s[4].append(m_s.reshape(ns, SLOT, LANES)[:, SLOT - 1, :N_HEADS])

    y_prompt = xp.reshape(nb, t, D_MODEL)
    y_sample = xs.reshape(ns, SLOT, D_MODEL)[:, pad:, :]
    sp = [jnp.stack(a, axis=0) for a in outs_p]
    ss = [outs_s[j][-1] if j in (0, 2) else jnp.stack(outs_s[j], axis=0) for j in range(5)]
    return (y_prompt, y_sample, *sp, *ss)
```

```python
import functools

import jax
import jax.numpy as jnp
from jax import lax
from jax.experimental import pallas as pl
from jax.experimental.pallas import tpu as pltpu

F32 = jnp.float32
BF16 = jnp.bfloat16

D_MODEL = 1024
N_HEADS = 4
DH = 128
D_CONV = 3 * N_HEADS * DH
D_FF = 2816
D_PLE = 256
CONV_W = 4
CHUNK = 64
EPS = 1e-6
NEG = -1e30
N_GATE = 16
LANES = 128
N_PROJ = D_CONV + 5 * N_HEADS * DH + LANES
GATE_COL_BLOCK = (N_PROJ - LANES) // LANES
SLOT = 8
SLOT_PAD = 4
V7X_VMEM_LIMIT_BYTES = 56 * 1024 * 1024
ROW_TILE = 512
FF_CHUNK = 256
PREP_ROWS = 1024
SCAN_ROWS = 256
SCAN_SEQS = 8
DECODE_PREP_ROWS = 4 * CHUNK
DECODE_MLSTM_CHUNKS = 2
PREP_GROUPS = 4
PREP_SKEW_STAGES = 4


def _cparams(n_axes):
    return pltpu.CompilerParams(dimension_semantics=("arbitrary",) * n_axes,
                                vmem_limit_bytes=V7X_VMEM_LIMIT_BYTES)


def _rms(x, g):
    return x * lax.rsqrt(jnp.mean(x * x, axis=-1, keepdims=True) + EPS) * g


def _softplus(x):
    return jnp.maximum(x, 0.0) + jnp.log1p(jnp.exp(-jnp.abs(x)))


def _dot(a, b):
    return jnp.dot(a, b, preferred_element_type=F32)


def _dot_nt(a, b):
    return lax.dot_general(a, b, (((1,), (1,)), ((), ())), preferred_element_type=F32)


def _split(a):
    hi = a.astype(BF16)
    lo = (a - hi.astype(F32)).astype(BF16)
    return hi, lo


def _split_all(xs):
    return [_split(x) for x in xs]


def _mm3_all(a_list, b_list):
    out = []
    for (ah, al), (bh, bl) in zip(a_list, b_list):
        m = ah.shape[0]
        r = _dot(jnp.concatenate([ah, al], axis=0), bh)
        out.append(r[0:m] + (_dot(ah, bl) + r[m:2 * m]))
    return out


class _PairMasks:
    def __init__(self, seq_len):
        c = CHUNK
        self.r = lax.broadcasted_iota(jnp.int32, (c, 2 * c), 0)
        lane = lax.broadcasted_iota(jnp.int32, (c, 2 * c), 1)
        self.cc = lane & (c - 1)
        self.lo = lane < c
        if seq_len < c:
            sh = seq_len.bit_length() - 1
            same = (self.r >> sh) == (self.cc >> sh)
        else:
            same = self.r >= 0
        self.causal = (self.r >= self.cc) & same
        self.causal_f = jnp.where(self.causal, 1.0, 0.0)
        self.strict_f = jnp.where((self.r > self.cc) & same, 1.0, 0.0)
        self.eye = jnp.where(self.r == self.cc, 1.0, 0.0)
        self.lo_b = jnp.where(self.lo, 1.0, 0.0).astype(BF16)
        self.hi_b = jnp.where(self.lo, 0.0, 1.0).astype(BF16)

    def blockdiag(self, sp):
        return tuple(jnp.concatenate([x * self.lo_b, x * self.hi_b], axis=0) for x in sp)


def _unit_lower_inverse_pairs(l_list, pm, seq_len, out):
    def bd_all(sps):
        return [pm.blockdiag(sp) for sp in sps]

    blk8 = jnp.where((pm.r >> 3) == (pm.cc >> 3), 1.0, 0.0)
    n0 = [l * blk8 for l in l_list]
    n0s = _split_all(n0)
    n2 = _mm3_all(n0s, bd_all(n0s))
    yield
    n2s = _split_all(n2)
    n4 = _mm3_all(n2s, bd_all(n2s))
    p = _mm3_all(_split_all([pm.eye - a for a in n0]), bd_all(_split_all([pm.eye + a for a in n2])))
    yield
    d = _mm3_all(_split_all(p), bd_all(_split_all([pm.eye + a for a in n4])))
    yield
    ds = _split_all(d)
    s = 8
    while s < seq_len:
        sh = s.bit_length() - 1
        off = jnp.where(((pm.r >> (sh + 1)) == (pm.cc >> (sh + 1))) & ((pm.r >> sh) != (pm.cc >> sh)), 1.0, 0.0)
        de = _mm3_all(ds, bd_all(_split_all([l * off for l in l_list])))
        yield
        ded = _mm3_all(_split_all(de), bd_all(ds))
        yield
        d = [a - b for a, b in zip(d, ded)]
        ds = _split_all(d)
        s *= 2
    out.extend(ds)


def _in_proj_kernel(x_ref, g_ref, wt_ref, proj_ref, gt_ref):
    hb = _rms(x_ref[...], g_ref[...]).astype(BF16)
    proj_ref[...] = _dot_nt(hb, wt_ref[...])
    gt = proj_ref[:, N_PROJ - LANES:N_PROJ].T
    for j in range(gt_ref.shape[0]):
        gt_ref[j] = gt[0:N_GATE, CHUNK * j:CHUNK * (j + 1)]


def _layer_spec(layer, *shape):
    zeros = (0,) * len(shape)
    return pl.BlockSpec((None,) + shape, lambda *_: (layer,) + zeros)


def _in_proj(x, g, w, layer):
    rows = x.shape[0]
    tm = ROW_TILE
    return pl.pallas_call(
        _in_proj_kernel,
        grid=(rows // tm,),
        in_specs=[
            pl.BlockSpec((tm, D_MODEL), lambda i: (i, 0)),
            _layer_spec(layer, 1, D_MODEL),
            _layer_spec(layer, N_PROJ, D_MODEL),
        ],
        out_specs=[
            pl.BlockSpec((tm, N_PROJ), lambda i: (i, 0)),
            pl.BlockSpec((tm // CHUNK, N_GATE, CHUNK), lambda i: (i, 0, 0)),
        ],
        out_shape=[jax.ShapeDtypeStruct((rows, N_PROJ), F32),
                   jax.ShapeDtypeStruct((rows // CHUNK, N_GATE, CHUNK), F32)],
        compiler_params=_cparams(1),
        name="in_proj",
    )(x, g, w)


def _chunk_masks(seq_len):
    c = CHUNK
    ri = lax.broadcasted_iota(jnp.int32, (c, c), 0)
    ci = lax.broadcasted_iota(jnp.int32, (c, c), 1)
    if seq_len < c:
        sh = seq_len.bit_length() - 1
        same = (ri >> sh) == (ci >> sh)
    else:
        same = ri >= 0
    return ri, ci, same


def _gdn_prep_kernel(*refs, tb, seq_len, valid_lo, idt):
    nk = tb // CHUNK
    groups = PREP_GROUPS
    while groups > 1 and (nk % groups or nk // groups < 2):
        groups //= 2
    gens = [_gdn_prep_group(refs[:-5], refs[-5:], g * (nk // groups), nk // groups, seq_len, valid_lo, idt)
            for g in range(groups)]
    _run_skewed(gens, PREP_SKEW_STAGES)


def _run_skewed(gens, skew):
    live = []
    pending = list(gens)
    tick = 0
    while live or pending:
        if pending and tick % skew == 0:
            live.append(pending.pop(0))
        for g in list(live):
            try:
                next(g)
            except StopIteration:
                live.remove(g)
        tick += 1


def _gdn_prep_group(ins, outs, k0, nk, seq_len, valid_lo, idt):
    has_cprev = valid_lo > 0
    if has_cprev:
        u_ref, prev_ref, gcol_ref, grow_ref, cprev_ref, cw_ref, pa_ref, pl_ref, ra_ref, rl_ref = ins
    else:
        u_ref, prev_ref, gcol_ref, grow_ref, cw_ref, pa_ref, pl_ref, ra_ref, rl_ref = ins
    wkqg_ref, wv_ref, qk_ref, kdt_ref, el_ref = outs
    c = CHUNK
    sh = seq_len.bit_length() - 1
    r_lo, nrows = k0 * c, nk * c
    u = u_ref[0, r_lo:r_lo + nrows, :]
    rowpos = lax.broadcasted_iota(jnp.int32, (nrows, 1), 0) & (seq_len - 1)
    if has_cprev:
        u = jnp.where((rowpos >= valid_lo - (CONV_W - 1)) & (rowpos < valid_lo),
                      cprev_ref[0, r_lo:r_lo + nrows, :], u)
    if k0 == 0:
        prev = jnp.where(pl.program_id(1) == 0, 0.0, prev_ref[0])
    else:
        prev = u_ref[0, r_lo - 8:r_lo, :]
    xp = jnp.concatenate([prev, u], axis=0)
    cw = cw_ref[...]
    ys = []
    for part in range(3):
        cs = slice(part * N_HEADS * DH, (part + 1) * N_HEADS * DH)
        y = None
        for i in range(CONV_W):
            s = CONV_W - 1 - i
            xs = u[:, cs] if s == 0 else pltpu.roll(xp[:, cs], s, 0)[8:8 + nrows]
            t = xs * cw[i:i + 1, cs]
            y = t if y is None else y + t
        ys.append(jax.nn.silu(y))
        yield
    yq, yk, yv = ys

    lane = lax.broadcasted_iota(jnp.int32, (1, LANES), 1)
    xg = gcol_ref[0, r_lo:r_lo + nrows, :] + pa_ref[0:1, :]
    gcolv = jnp.where(lane < N_HEADS, jax.nn.sigmoid(xg), -jnp.exp(pl_ref[0:1, :]) * _softplus(xg))
    if valid_lo > 0:
        gcolv = jnp.where(rowpos >= valid_lo, gcolv, 0.0)

    ri, ci, same = _chunk_masks(seq_len)
    causal = (ri >= ci) & same
    upper = (ri <= ci) & same
    colpos = lax.broadcasted_iota(jnp.int32, (1, c), 1) & (seq_len - 1)
    r8 = lax.broadcasted_iota(jnp.int32, (8, c), 0)
    c8 = lax.broadcasted_iota(jnp.int32, (8, c), 1)

    items = []
    for kk in range(nk):
        r0 = kk * c
        k = k0 + kk
        growv = -jnp.exp(rl_ref[:, 0:1]) * _softplus(grow_ref[k] + ra_ref[:, 0:1])
        if valid_lo > 0:
            growv = jnp.where(colpos >= valid_lo, growv, 0.0)
        for h in range(N_HEADS):
            hs = slice(DH * h, DH * (h + 1))
            qc = yq[r0:r0 + c, hs]
            kc = yk[r0:r0 + c, hs]
            g_col = gcolv[r0:r0 + c, N_HEADS + h:N_HEADS + h + 1]
            g_row = growv[N_HEADS + h:N_HEADS + h + 1, :]
            items.append(dict(
                k=k, h=h, hs=hs, qc=qc, kc=kc, vc=yv[r0:r0 + c, hs],
                beta=gcolv[r0:r0 + c, h:h + 1],
                qss=jnp.sum(qc * qc, axis=-1, keepdims=True),
                kss=jnp.sum(kc * kc, axis=-1, keepdims=True),
                gc_col=jnp.sum(jnp.where(causal, g_row, 0.0), axis=1, keepdims=True),
                gc_row=jnp.sum(jnp.where(upper, g_col, 0.0), axis=0, keepdims=True),
                gl_col=jnp.sum(jnp.where(same, g_row, 0.0), axis=1, keepdims=True),
                gl8=jnp.sum(jnp.where((c8 >> sh) == r8, g_row, 0.0), axis=1, keepdims=True)))
    yield
    for it in items:
        k, hs, gc_col, beta = it["k"], it["hs"], it["gc_col"], it["beta"]
        qn = it["qc"] * lax.rsqrt(it["qss"] + EPS) * (DH ** -0.5)
        kn = it["kc"] * lax.rsqrt(it["kss"] + EPS)
        wkqg_ref[0, k, c:2 * c, hs] = (qn * jnp.exp(gc_col)).astype(idt)
        el_ref[0, k, :, hs] = jnp.broadcast_to(jnp.exp(it["gl8"]), (8, DH))
        it["kd"] = kn * jnp.exp(it["gl_col"] - gc_col)
        it["qb"] = qn.astype(BF16)
        it["kb"] = kn.astype(BF16)
        it["rhs"] = jnp.concatenate([beta * it["vc"], (beta * jnp.exp(gc_col)) * kn], axis=1)
    yield
    for a, b in zip(items[0::2], items[1::2]):
        kdt_ref[0, a["k"], a["h"] // 2] = jnp.concatenate([a["kd"], b["kd"]], axis=0).T.astype(idt)
    pm = _PairMasks(seq_len)
    zb = jnp.zeros((c, DH), BF16)
    pairs = []
    for a, b in zip(items[0::2], items[1::2]):
        gcp = jnp.where(pm.lo, a["gc_col"], b["gc_col"])
        grp = jnp.concatenate([a["gc_row"], b["gc_row"]], axis=1)
        decay = jnp.exp(jnp.where(pm.causal, gcp - grp, 0.0)) * pm.causal_f
        kbd = jnp.concatenate([jnp.concatenate([a["kb"], zb], axis=1),
                               jnp.concatenate([zb, b["kb"]], axis=1)], axis=0)
        kq = jnp.concatenate([jnp.concatenate([a["kb"], b["kb"]], axis=1),
                              jnp.concatenate([a["qb"], b["qb"]], axis=1)], axis=0)
        pairs.append(dict(a=a, b=b, decay=decay, kbd=kbd, kq=kq, beta=jnp.where(pm.lo, a["beta"], b["beta"])))
    kqs = [_dot_nt(p["kq"], p["kbd"]) for p in pairs]
    yield
    for p, kq in zip(pairs, kqs):
        qk_ref[0, p["a"]["k"], p["a"]["h"] // 2] = (kq[c:2 * c] * p["decay"]).astype(idt)
    ls = [p["beta"] * kq[0:c] * p["decay"] * pm.strict_f for p, kq in zip(pairs, kqs)]
    tinvs = []
    yield from _unit_lower_inverse_pairs(ls, pm, seq_len, tinvs)
    for p, (th, tl) in zip(pairs, tinvs):
        rh, rl_ = _split(jnp.concatenate([p["a"]["rhs"], p["b"]["rhs"]], axis=0))
        ta_h, tb_h = th * pm.lo_b, th * pm.hi_b
        r1 = _dot(jnp.concatenate([ta_h, tl * pm.lo_b, tb_h, tl * pm.hi_b], axis=0), rh)
        r2 = _dot(jnp.concatenate([ta_h, tb_h], axis=0), rl_)
        p["sol"] = (r1[0:c] + (r2[0:c] + r1[c:2 * c]), r1[2 * c:3 * c] + (r2[c:2 * c] + r1[3 * c:4 * c]))
    yield
    for p in pairs:
        for it, sol in zip((p["a"], p["b"]), p["sol"]):
            wv_ref[0, it["k"], :, it["hs"]] = sol[:, 0:DH]
            wkqg_ref[0, it["k"], 0:c, it["hs"]] = sol[:, DH:2 * DH].astype(idt)


def _gdn_prep(proj3, grow, cprev, cw, pa, pl_, ra, rl, layer, *, seq_len, valid_lo, idt, tb):
    nb, t, _ = proj3.shape
    c = CHUNK
    nt = t // tb
    kb = tb // c
    has_cprev = valid_lo > 0
    in_specs = [
        pl.BlockSpec((1, tb, D_CONV), lambda b, i: (b, i, 0)),
        pl.BlockSpec((1, 8, D_CONV), lambda b, i: (b, jnp.maximum(i * (tb // 8) - 1, 0), 0)),
        pl.BlockSpec((1, tb, LANES), lambda b, i: (b, i, GATE_COL_BLOCK)),
        pl.BlockSpec((kb, N_GATE, c), lambda b, i: (b * nt + i, 0, 0)),
    ]
    args = [proj3, proj3, proj3, grow]
    if has_cprev:
        assert nb == 1
        in_specs.append(pl.BlockSpec((1, tb, D_CONV), lambda b, i: (layer, i, 0)))
        args.append(cprev)
    in_specs += [
        _layer_spec(layer, CONV_W, D_CONV),
        _layer_spec(layer, 8, LANES),
        _layer_spec(layer, 8, LANES),
        _layer_spec(layer, N_GATE, LANES),
        _layer_spec(layer, N_GATE, LANES),
    ]
    args += [cw, pa, pl_, ra, rl]
    n = t // c
    hd = N_HEADS * DH
    out_shape = [
        jax.ShapeDtypeStruct((nb, n, 2 * c, hd), idt),
        jax.ShapeDtypeStruct((nb, n, c, hd), F32),
        jax.ShapeDtypeStruct((nb, n, N_HEADS // 2, c, 2 * c), idt),
        jax.ShapeDtypeStruct((nb, n, N_HEADS // 2, DH, 2 * c), idt),
        jax.ShapeDtypeStruct((nb, n, 8, hd), F32),
    ]
    out_specs = [
        pl.BlockSpec((1, kb, 2 * c, hd), lambda b, i: (b, i, 0, 0)),
        pl.BlockSpec((1, kb, c, hd), lambda b, i: (b, i, 0, 0)),
        pl.BlockSpec((1, kb, N_HEADS // 2, c, 2 * c), lambda b, i: (b, i, 0, 0, 0)),
        pl.BlockSpec((1, kb, N_HEADS // 2, DH, 2 * c), lambda b, i: (b, i, 0, 0, 0)),
        pl.BlockSpec((1, kb, 8, hd), lambda b, i: (b, i, 0, 0)),
    ]
    return pl.pallas_call(
        functools.partial(_gdn_prep_kernel, tb=tb, seq_len=seq_len, valid_lo=valid_lo, idt=idt),
        grid=(nb, nt),
        in_specs=in_specs,
        out_specs=out_specs,
        out_shape=out_shape,
        compiler_params=_cparams(2),
        name="gdn_prep",
    )(*args)


def _lane_half_masks(dtype):
    lo = lax.broadcasted_iota(jnp.int32, (CHUNK, 2 * CHUNK), 1) < CHUNK
    return jnp.where(lo, 1.0, 0.0).astype(dtype), jnp.where(lo, 0.0, 1.0).astype(dtype)


def _mean_square_lanes(x):
    rows = x.shape[0]
    ssq = _dot(jnp.concatenate(_split(x * x), axis=0), jnp.ones((DH, DH), BF16))
    return (ssq[0:rows] + ssq[rows:2 * rows]) * (1.0 / DH)


def _gated_norm_store(o, z, gn, mix_ref, idx):
    mix_ref[idx] = (_rms(o, gn) * jax.nn.silu(z)).astype(BF16)


def _gdn_scan_prompt_kernel(wkqg_ref, wv_ref, qk_ref, kdt_ref, el_ref, z_ref, gn_ref,
                            mix_ref, sout_ref, s_scr, *, nc, bs):
    c = CHUNK

    @pl.when(pl.program_id(1) == 0)
    def _():
        s_scr[...] = jnp.zeros_like(s_scr)

    gn = gn_ref[...]
    units = [(b, h, slice(DH * h, DH * (h + 1))) for b in range(bs) for h in range(N_HEADS)]
    half_b = _lane_half_masks(BF16)
    half2_b = tuple(jnp.concatenate([m, m], axis=0) for m in half_b)

    def body(n, carry):
        rows = pl.ds(pl.multiple_of(n * c, c), c)
        ss = [s_scr[b * N_HEADS + h] for b, h, _ in units]
        aa = [_dot(wkqg_ref[b, n, :, hs], s.astype(BF16)) for (b, _, hs), s in zip(units, ss)]
        ubs = [(wv_ref[b, n, :, hs] - a[0:c]).astype(BF16) for (b, _, hs), a in zip(units, aa)]
        os_, ds = [], []
        for j in range(len(units) // 2):
            b, h, _ = units[2 * j]
            qkp, kdp = qk_ref[b, n, h // 2], kdt_ref[b, n, h // 2]
            r = _dot(jnp.concatenate([qkp * half_b[0], qkp * half_b[1], kdp * half2_b[0], kdp * half2_b[1]],
                                     axis=0),
                     jnp.concatenate(ubs[2 * j:2 * j + 2], axis=0))
            os_ += [r[0:c], r[c:2 * c]]
            ds += [r[2 * c:2 * c + DH], r[2 * c + DH:2 * c + 2 * DH]]
        for i, (b, h, hs) in enumerate(units):
            s_scr[b * N_HEADS + h] = el_ref[b, n, 0:1, hs] * ss[i] + ds[i]
            _gated_norm_store(aa[i][c:2 * c] + os_[i], z_ref[b, rows, hs], gn, mix_ref, (b, rows, hs))
        return carry

    lax.fori_loop(0, nc, body, 0)

    @pl.when(pl.program_id(1) == pl.num_programs(1) - 1)
    def _():
        for b in range(bs):
            sout_ref[b] = s_scr[b * N_HEADS:(b + 1) * N_HEADS]


def _gdn_scan_prompt(wkqg, wv, qk, kdt, el, proj3, gn, layer, *, ts, bs):
    nb, n, _, hd = wkqg.shape
    c = CHUNK
    t = n * c
    nc = ts // c
    zblk = D_CONV // hd
    return pl.pallas_call(
        functools.partial(_gdn_scan_prompt_kernel, nc=nc, bs=bs),
        grid=(nb // bs, t // ts),
        in_specs=[
            pl.BlockSpec((bs, nc, 2 * c, hd), lambda b, i: (b, i, 0, 0)),
            pl.BlockSpec((bs, nc, c, hd), lambda b, i: (b, i, 0, 0)),
            pl.BlockSpec((bs, nc, N_HEADS // 2, c, 2 * c), lambda b, i: (b, i, 0, 0, 0)),
            pl.BlockSpec((bs, nc, N_HEADS // 2, DH, 2 * c), lambda b, i: (b, i, 0, 0, 0)),
            pl.BlockSpec((bs, nc, 8, hd), lambda b, i: (b, i, 0, 0)),
            pl.BlockSpec((bs, ts, hd), lambda b, i: (b, i, zblk)),
            _layer_spec(layer, 1, DH),
        ],
        out_specs=[
            pl.BlockSpec((bs, ts, hd), lambda b, i: (b, i, 0)),
            pl.BlockSpec((bs, N_HEADS, DH, DH), lambda b, i: (b, 0, 0, 0)),
        ],
        out_shape=[jax.ShapeDtypeStruct((nb, t, hd), BF16),
                   jax.ShapeDtypeStruct((nb, N_HEADS, DH, DH), F32)],
        scratch_shapes=[pltpu.VMEM((bs * N_HEADS, DH, DH), F32)],
        compiler_params=_cparams(2),
        name="gdn_scan_prompt",
    )(wkqg, wv, qk, kdt, el, proj3, gn)


def _gdn_scan_decode_kernel(*refs, n_prev):
    (wkqg_ref, wv_ref, qk_ref, kdt_ref, el_ref, z_ref, gn_ref, s0_ref), rest = refs[:8], refs[8:]
    prev_refs, (mix_ref, sout_ref) = rest[:n_prev], rest[n_prev:]
    if n_prev:
        for l, p_ref in enumerate(prev_refs):
            sout_ref[l] = p_ref[...]
        sout_ref = sout_ref.at[n_prev]
    c = CHUNK
    nseq = c // SLOT
    rowseq = lax.broadcasted_iota(jnp.int32, (c, 1), 0) >> 3
    gn = gn_ref[...]
    heads = [slice(DH * h, DH * (h + 1)) for h in range(N_HEADS)]
    aa = []
    for h, hs in enumerate(heads):
        w = wkqg_ref[0, 0, :, hs]
        row = []
        for j in range(nseq):
            wj = jnp.concatenate([w[SLOT * j:SLOT * (j + 1)], w[c + SLOT * j:c + SLOT * (j + 1)]],
                                 axis=0).astype(BF16)
            row.append(_dot(wj, s0_ref[j, h].astype(BF16)))
        aa.append(row)
    us = [wv_ref[0, 0, :, hs] - jnp.concatenate([a[0:SLOT] for a in aa[h]], axis=0)
          for h, hs in enumerate(heads)]
    half_f = _lane_half_masks(F32)
    os_ = []
    for j in range(N_HEADS // 2):
        qkp = qk_ref[0, 0, j]
        o2 = _dot(jnp.concatenate([qkp * half_f[0], qkp * half_f[1]], axis=0).astype(BF16),
                  jnp.concatenate(us[2 * j:2 * j + 2], axis=0).astype(BF16))
        os_ += [o2[0:c], o2[c:2 * c]]
    half2_f = tuple(jnp.concatenate([m, m], axis=0) for m in half_f)
    for p in range(N_HEADS // 2):
        kdp = kdt_ref[0, 0, p]
        kd2 = jnp.concatenate([kdp * half2_f[0], kdp * half2_f[1]], axis=0).astype(BF16)
        for j in range(nseq):
            uj = jnp.concatenate([jnp.where(rowseq == j, us[2 * p + i], 0.0) for i in range(2)], axis=0)
            r = _dot(kd2, uj.astype(BF16))
            for i in range(2):
                h = 2 * p + i
                sout_ref[j, h] = (el_ref[0, 0, j:j + 1, heads[h]] * s0_ref[j, h] + r[DH * i:DH * (i + 1)])
    for h, hs in enumerate(heads):
        o = jnp.concatenate([a[SLOT:2 * SLOT] for a in aa[h]], axis=0) + os_[h]
        _gated_norm_store(o, z_ref[:, hs], gn, mix_ref, (slice(None), hs))


def _state_specs(layer, prev, state_shape, nseq):
    tail = state_shape[2:]
    zeros = (0,) * len(tail)
    in_specs = [pl.BlockSpec((None, nseq) + tail, lambda i: (layer, i) + zeros)]
    in_specs += [pl.BlockSpec((nseq,) + tail, lambda i: (i,) + zeros) for _ in prev]
    if prev:
        out_spec = pl.BlockSpec((len(prev) + 1, nseq) + tail, lambda i: (0, i) + zeros)
        out_shape = jax.ShapeDtypeStruct((len(prev) + 1,) + state_shape[1:], F32)
    else:
        out_spec = pl.BlockSpec((nseq,) + tail, lambda i: (i,) + zeros)
        out_shape = jax.ShapeDtypeStruct(state_shape[1:], F32)
    return in_specs, out_spec, out_shape


def _gdn_scan_decode(wkqg, wv, qk, kdt, el, proj, gn, s0_all, prev, layer):
    _, n, _, hd = wkqg.shape
    c = CHUNK
    nseq = c // SLOT
    zblk = D_CONV // hd
    st_in, st_out, st_shape = _state_specs(layer, prev, s0_all.shape, nseq)
    return pl.pallas_call(
        functools.partial(_gdn_scan_decode_kernel, n_prev=len(prev)),
        grid=(n,),
        in_specs=[
            pl.BlockSpec((1, 1, 2 * c, hd), lambda i: (0, i, 0, 0)),
            pl.BlockSpec((1, 1, c, hd), lambda i: (0, i, 0, 0)),
            pl.BlockSpec((1, 1, N_HEADS // 2, c, 2 * c), lambda i: (0, i, 0, 0, 0)),
            pl.BlockSpec((1, 1, N_HEADS // 2, DH, 2 * c), lambda i: (0, i, 0, 0, 0)),
            pl.BlockSpec((1, 1, 8, hd), lambda i: (0, i, 0, 0)),
            pl.BlockSpec((c, hd), lambda i: (i, zblk)),
            _layer_spec(layer, 1, DH),
        ] + st_in,
        out_specs=[pl.BlockSpec((c, hd), lambda i: (i, 0)), st_out],
        out_shape=[jax.ShapeDtypeStruct((n * c, hd), BF16), st_shape],
        compiler_params=_cparams(1),
        name="gdn_scan_decode",
    )(wkqg, wv, qk, kdt, el, proj, gn, s0_all, *prev)


def _mlstm_gate_blocks(gcol, growk, pa_ref, ra_ref):
    xg = gcol + pa_ref[0:1, :]
    xr = growk + ra_ref[:, 0:1]
    return xg, -_softplus(-xg), xr, -_softplus(-xr)


def _mlstm_gates(blocks, h, valid_lo, rowpos, colpos):
    xg, lg, xr, lr = blocks
    ig_col = xg[:, 2 * N_HEADS + h:2 * N_HEADS + h + 1]
    fl_col = lg[:, 3 * N_HEADS + h:3 * N_HEADS + h + 1]
    ig_row = xr[2 * N_HEADS + h:2 * N_HEADS + h + 1, :]
    fl_row = lr[3 * N_HEADS + h:3 * N_HEADS + h + 1, :]
    if valid_lo > 0:
        ig_col = jnp.where(rowpos >= valid_lo, ig_col, NEG)
        fl_col = jnp.where(rowpos >= valid_lo, fl_col, 0.0)
        ig_row = jnp.where(colpos >= valid_lo, ig_row, NEG)
        fl_row = jnp.where(colpos >= valid_lo, fl_row, 0.0)
    return ig_col, fl_col, ig_row, fl_row


def _mlstm_chunks(probs, seq_len, finish):
    c = CHUNK
    nseq = c // seq_len
    ri, ci, same = _chunk_masks(seq_len)
    causal = (ri >= ci) & same
    causal_f = jnp.where(causal, 1.0, 0.0)
    upper = (ri <= ci) & same
    seq_end = ci == (ri | (seq_len - 1))
    rowseq = lax.broadcasted_iota(jnp.int32, (c, 1), 0) >> (seq_len.bit_length() - 1)
    zpad = jnp.zeros((16 - seq_len, DH), F32) if nseq > 1 else None
    ones_b = jnp.ones((c, LANES), BF16)
    for p in probs:
        p["f_row"] = jnp.sum(jnp.where(upper, p["fl_col"], 0.0), axis=0, keepdims=True)
        p["km"] = p["k"] * (DH ** -0.5)
        p["qb"] = p["q"].astype(BF16)
        p["kb"] = p["km"].astype(BF16)
        p["vb"] = p["v"].astype(BF16)
        if nseq == 1:
            r = _dot(jnp.concatenate(_split(jnp.where(causal, p["fl_row"], 0.0)), axis=0), ones_b)
            p["f_col"] = r[0:c] + r[c:2 * c]
            nb2 = p["n_rows"].astype(BF16)
            p["qn"] = _dot_nt(p["qb"], jnp.concatenate([nb2, nb2], axis=0))
            p["ig_col"] = jnp.broadcast_to(p["ig_col"], (c, LANES))
        else:
            p["f_col"] = jnp.sum(jnp.where(causal, p["fl_row"], 0.0), axis=1, keepdims=True)
            p["qn"] = jnp.sum(p["q"] * p["n_rows"], axis=1, keepdims=True)
    yield
    for p in probs:
        p["b_col"] = p["ig_col"] - p["f_col"]
        p["b_row"] = p["ig_row"] - p["f_row"]
        p["bmax_col"] = jnp.max(jnp.where(causal, p["b_row"], -jnp.inf), axis=1, keepdims=True)
        if nseq == 1:
            p["bmax_col"] = jnp.broadcast_to(p["bmax_col"], (c, LANES))
        if nseq > 1:
            p["fl_end"] = jnp.sum(jnp.where(seq_end, p["f_row"], 0.0), axis=1, keepdims=True)
            p["bmax_end"] = jnp.max(jnp.where(same, p["b_row"], -jnp.inf), axis=1, keepdims=True)
    yield
    for p in probs:
        f_col, mp_col = p["f_col"], p["mp_col"]
        m_col = f_col + jnp.maximum(mp_col, p["bmax_col"])
        p["m_col"] = m_col
        p["a_col"] = jnp.exp(f_col + mp_col - m_col)
        fm = (f_col - m_col)[:, 0:c] if nseq == 1 else f_col - m_col
        p["dexp"] = jnp.exp(jnp.where(causal, fm + p["b_row"], 0.0)) * causal_f
        if nseq == 1:
            fl_end, ml, mp = f_col[c - 1:c], m_col[c - 1:c], mp_col[c - 1:c]
            p["ml_col"] = ml
            p["al"] = jnp.exp(fl_end + mp - ml)
        else:
            fl_end, mp = p["fl_end"], mp_col
            ml = fl_end + jnp.maximum(mp_col, p["bmax_end"])
            p["ml_col"] = ml
            p["al"] = jnp.broadcast_to(jnp.exp(fl_end + mp - ml), (c, DH))
        p["kw"] = p["km"] * jnp.exp(fl_end + p["b_col"] - ml)
    yield
    for p in probs:
        p["kwt"] = p["kw"].T.astype(BF16)
    for p in probs:
        p["qk"] = _dot_nt(p["qb"], p["kb"])
    yield
    for p in probs:
        if nseq == 1:
            p["qc"] = _dot(p["qb"], p["c_list"][0].astype(BF16))
            p["upd"] = [_dot(p["kwt"], p["vb"])]
        else:
            parts, upd = [], []
            for j in range(nseq):
                qj = jnp.concatenate([p["q"][seq_len * j:seq_len * (j + 1)], zpad], axis=0).astype(BF16)
                parts.append(_dot(qj, p["c_list"][j].astype(BF16))[0:seq_len])
                upd.append(_dot(p["kwt"], jnp.where(rowseq == j, p["v"], 0.0).astype(BF16)))
            p["qc"] = jnp.concatenate(parts, axis=0)
            p["upd"] = upd
    yield
    if nseq == 1:
        for p in probs:
            p["w_hl"] = _split(p["dexp"] * p["qk"])
        yield
        for p in probs:
            r = _dot(jnp.concatenate(p["w_hl"], axis=0), jnp.concatenate([p["vb"], ones_b], axis=1))
            p["wv"] = r[0:c, 0:DH]
            p["wsum"] = r[0:c, DH:DH + LANES] + r[c:2 * c, DH:DH + LANES]
    else:
        for p in probs:
            p["w"] = p["dexp"] * p["qk"]
            p["wsum"] = jnp.sum(p["w"], axis=1, keepdims=True)
        for p in probs:
            p["wv"] = _dot(p["w"].astype(BF16), p["vb"])
    yield
    for p in probs:
        num = p["a_col"] * p["qc"] + p["wv"]
        den = p["a_col"] * p["qn"] + p["wsum"]
        p["hout"] = num / jnp.maximum(jnp.abs(den), jnp.exp(-p["m_col"]))
        p["new_c"] = [p["al"][seq_len * j:seq_len * j + 1] * p["c_list"][j] + p["upd"][j]
                      for j in range(nseq)]
    yield
    finish(probs)


def _mlstm_out_store(hout, og, gnorm, mix_ref, idx):
    x = hout * jax.nn.sigmoid(og)
    mix_ref[idx] = (x * lax.rsqrt(_mean_square_lanes(x) + EPS) * gnorm).astype(BF16)


def _mlstm_prompt_kernel(q_ref, k_ref, v_ref, o_ref, gcol_ref, grow_ref, pa_ref, ra_ref, nrm_ref,
                         mix_ref, cout_ref, nout_ref, mout_ref, c_scr, n_scr, m_scr, *, nc, bs):
    c = CHUNK

    @pl.when(pl.program_id(1) == 0)
    def _():
        c_scr[...] = jnp.zeros_like(c_scr)
        n_scr[...] = jnp.zeros_like(n_scr)
        m_scr[...] = jnp.zeros_like(m_scr)

    def body(n, carry):
        r0 = pl.multiple_of(n * c, c)
        rows = pl.ds(r0, c)
        def finish(probs):
            for p in probs:
                st = p["st"]
                c_scr[st] = p["new_c"][0]
                n_scr[st] = jnp.broadcast_to(
                    p["al"][0:1] * p["n_rows"][0:1] + jnp.sum(p["kw"], axis=0, keepdims=True), (c, DH))
                m_scr[st] = jnp.broadcast_to(p["ml_col"], (c, DH))
                _mlstm_out_store(p["hout"], o_ref[p["b"], rows, p["hs"]], nrm_ref[:, p["hs"]], mix_ref,
                                 (p["b"], rows, p["hs"]))

        probs = []
        for b in range(bs):
            blocks = _mlstm_gate_blocks(gcol_ref[b, rows, :], grow_ref[b, n], pa_ref, ra_ref)
            for h in range(N_HEADS):
                hs = slice(DH * h, DH * (h + 1))
                ig_col, fl_col, ig_row, fl_row = _mlstm_gates(blocks, h, 0, None, None)
                st = b * N_HEADS + h
                probs.append(dict(b=b, hs=hs, st=st, q=q_ref[b, rows, hs], k=k_ref[b, rows, hs],
                                  v=v_ref[b, rows, hs], ig_col=ig_col, fl_col=fl_col, ig_row=ig_row,
                                  fl_row=fl_row, mp_col=m_scr[st], n_rows=n_scr[st],
                                  c_list=[c_scr[st]]))
        _run_skewed([_mlstm_chunks(probs, c, finish)], 1)
        return carry

    lax.fori_loop(0, nc, body, 0)

    @pl.when(pl.program_id(1) == pl.num_programs(1) - 1)
    def _():
        lane = lax.broadcasted_iota(jnp.int32, (8, DH), 1)
        for b in range(bs):
            mo = jnp.zeros((8, DH), F32)
            for h in range(N_HEADS):
                st = b * N_HEADS + h
                cout_ref[b, h] = c_scr[st]
                nout_ref[b, h:h + 1, :] = n_scr[st][0:1]
                mo = jnp.where(lane == h, m_scr[st][0:8], mo)
            mout_ref[b] = mo


def _mlstm_prompt(proj3, grow4, pa, ra, nrm, layer, *, ts, bs):
    nb, t, _ = proj3.shape
    c = CHUNK
    hd = N_HEADS * DH
    nc = ts // c
    qblk = (D_CONV + hd) // hd
    return pl.pallas_call(
        functools.partial(_mlstm_prompt_kernel, nc=nc, bs=bs),
        grid=(nb // bs, t // ts),
        in_specs=[
            pl.BlockSpec((bs, ts, hd), lambda b, i: (b, i, qblk)),
            pl.BlockSpec((bs, ts, hd), lambda b, i: (b, i, qblk + 1)),
            pl.BlockSpec((bs, ts, hd), lambda b, i: (b, i, qblk + 2)),
            pl.BlockSpec((bs, ts, hd), lambda b, i: (b, i, qblk + 3)),
            pl.BlockSpec((bs, ts, LANES), lambda b, i: (b, i, GATE_COL_BLOCK)),
            pl.BlockSpec((bs, nc, N_GATE, c), lambda b, i: (b, i, 0, 0)),
            _layer_spec(layer, 8, LANES),
            _layer_spec(layer, N_GATE, LANES),
            _layer_spec(layer, 1, hd),
        ],
        out_specs=[
            pl.BlockSpec((bs, ts, hd), lambda b, i: (b, i, 0)),
            pl.BlockSpec((bs, N_HEADS, DH, DH), lambda b, i: (b, 0, 0, 0)),
            pl.BlockSpec((bs, N_HEADS, DH), lambda b, i: (b, 0, 0)),
            pl.BlockSpec((bs, 8, DH), lambda b, i: (b, 0, 0)),
        ],
        out_shape=[jax.ShapeDtypeStruct((nb, t, hd), BF16),
                   jax.ShapeDtypeStruct((nb, N_HEADS, DH, DH), F32),
                   jax.ShapeDtypeStruct((nb, N_HEADS, DH), F32),
                   jax.ShapeDtypeStruct((nb, 8, DH), F32)],
        scratch_shapes=[pltpu.VMEM((bs * N_HEADS, DH, DH), F32),
                        pltpu.VMEM((bs * N_HEADS, c, DH), F32),
                        pltpu.VMEM((bs * N_HEADS, c, DH), F32)],
        compiler_params=_cparams(2),
        name="mlstm_prompt",
    )(proj3, proj3, proj3, proj3, proj3, grow4, pa, ra, nrm)


def _mlstm_decode_kernel(*refs, n_prev):
    (q_ref, k_ref, v_ref, o_ref, gcol_ref, grow_ref, pa_ref, ra_ref, nrm_ref, n0_ref, m0_ref,
     c0_ref), rest = refs[:12], refs[12:]
    prev_refs, (mix_ref, nout_ref, mout_ref, cout_ref) = rest[:n_prev], rest[n_prev:]
    if n_prev:
        for l, p_ref in enumerate(prev_refs):
            cout_ref[l] = p_ref[...]
        cout_ref = cout_ref.at[n_prev]
    c = CHUNK
    nseq = c // SLOT
    nck = q_ref.shape[0] // c
    rowpos = lax.broadcasted_iota(jnp.int32, (c, 1), 0) & (SLOT - 1)
    colpos = lax.broadcasted_iota(jnp.int32, (1, c), 1) & (SLOT - 1)
    lane = lax.broadcasted_iota(jnp.int32, (c, DH), 1)
    probs = []
    for ck in range(nck):
        rows = slice(c * ck, c * (ck + 1))
        blocks = _mlstm_gate_blocks(gcol_ref[rows, :], grow_ref[ck], pa_ref, ra_ref)
        for h in range(N_HEADS):
            hs = slice(DH * h, DH * (h + 1))
            ig_col, fl_col, ig_row, fl_row = _mlstm_gates(blocks, h, SLOT_PAD, rowpos, colpos)
            seqs = [nseq * ck + j for j in range(nseq)]
            n_rows = jnp.concatenate(
                [jnp.broadcast_to(n0_ref[s, h:h + 1, :], (SLOT, DH)) for s in seqs], axis=0)
            probs.append(dict(ck=ck, rows=rows, seqs=seqs, h=h, hs=hs, q=q_ref[rows, hs], k=k_ref[rows, hs],
                              v=v_ref[rows, hs], ig_col=ig_col, fl_col=fl_col, ig_row=ig_row, fl_row=fl_row,
                              mp_col=m0_ref[rows, h:h + 1], n_rows=n_rows,
                              c_list=[c0_ref[s, h] for s in seqs]))

    def finish(probs):
        mos = [jnp.zeros((c, DH), F32) for _ in range(nck)]
        for p in probs:
            h, hs, rows = p["h"], p["hs"], p["rows"]
            for j, s in enumerate(p["seqs"]):
                cout_ref[s, h] = p["new_c"][j]
                rs = slice(SLOT * j, SLOT * (j + 1))
                nout_ref[s, h:h + 1, :] = (p["al"][SLOT * j:SLOT * j + 1] * n0_ref[s, h:h + 1, :]
                                           + jnp.sum(p["kw"][rs], axis=0, keepdims=True))
            mos[p["ck"]] = jnp.where(lane == h, jnp.broadcast_to(p["ml_col"], (c, DH)), mos[p["ck"]])
            _mlstm_out_store(p["hout"], o_ref[rows, hs], nrm_ref[:, hs], mix_ref, (rows, hs))
        for ck in range(nck):
            mout_ref[c * ck:c * (ck + 1), :] = mos[ck]

    _run_skewed([_mlstm_chunks(probs, SLOT, finish)], 1)


def _mlstm_decode(proj, grow, pa, ra, nrm, c0_all, n0_all, m0rows_all, prev, layer, *, nck):
    rows = proj.shape[0]
    c = CHUNK * nck
    hd = N_HEADS * DH
    nseq = c // SLOT
    qblk = (D_CONV + hd) // hd
    st_in, st_out, st_shape = _state_specs(layer, prev, c0_all.shape, nseq)
    return pl.pallas_call(
        functools.partial(_mlstm_decode_kernel, n_prev=len(prev)),
        grid=(rows // c,),
        in_specs=[
            pl.BlockSpec((c, hd), lambda i: (i, qblk)),
            pl.BlockSpec((c, hd), lambda i: (i, qblk + 1)),
            pl.BlockSpec((c, hd), lambda i: (i, qblk + 2)),
            pl.BlockSpec((c, hd), lambda i: (i, qblk + 3)),
            pl.BlockSpec((c, LANES), lambda i: (i, GATE_COL_BLOCK)),
            pl.BlockSpec((nck, N_GATE, CHUNK), lambda i: (i, 0, 0)),
            _layer_spec(layer, 8, LANES),
            _layer_spec(layer, N_GATE, LANES),
            _layer_spec(layer, 1, hd),
            pl.BlockSpec((None, nseq, N_HEADS, DH), lambda i: (layer, i, 0, 0)),
            pl.BlockSpec((None, c, LANES), lambda i: (layer, i, 0)),
        ] + st_in,
        out_specs=[
            pl.BlockSpec((c, hd), lambda i: (i, 0)),
            pl.BlockSpec((nseq, N_HEADS, DH), lambda i: (i, 0, 0)),
            pl.BlockSpec((c, LANES), lambda i: (i, 0)),
            st_out,
        ],
        out_shape=[jax.ShapeDtypeStruct((rows, hd), BF16),
                   jax.ShapeDtypeStruct(n0_all.shape[1:], F32),
                   jax.ShapeDtypeStruct((rows, LANES), F32),
                   st_shape],
        compiler_params=_cparams(1),
        name="mlstm_decode",
    )(proj, proj, proj, proj, proj, grow, pa, ra, nrm, n0_all, m0rows_all, c0_all, *prev)


def _post_kernel(x_ref, ma_ref, mb_ref, p_ref, woa_ref, wob_ref, gf_ref, wg_ref, wu_ref, wd_ref,
                 gp_ref, wpg_ref, wpp_ref, gfin_ref, o_ref, acc_ref, *, final):
    x = x_ref[...] + (_dot(ma_ref[...], woa_ref[...]) + _dot(mb_ref[...], wob_ref[...]))
    ub = _rms(x, gf_ref[...]).astype(BF16)
    for j in range(D_FF // FF_CHUNK):
        sl = slice(j * FF_CHUNK, (j + 1) * FF_CHUNK)
        a = (jax.nn.silu(_dot(ub, wg_ref[:, sl])) * _dot(ub, wu_ref[:, sl])).astype(BF16)
        d = _dot(a, wd_ref[sl, :])
        if j == 0:
            acc_ref[...] = d
        else:
            acc_ref[...] += d
    x = x + acc_ref[...]
    gate = jax.nn.sigmoid(_dot(_rms(x, gp_ref[...]).astype(BF16), wpg_ref[...]))
    x = x + _dot(p_ref[...].astype(BF16), wpp_ref[...]) * gate
    if final:
        x = _rms(x, gfin_ref[...])
    o_ref[...] = x


def _post(x, ma, mb, p_all, wo, gf, wg, wu, wd, gp, wpg, wpp, gfin, layer, *, final):
    rows = x.shape[0]
    tm = ROW_TILE
    hd = N_HEADS * DH
    row = lambda w: pl.BlockSpec((tm, w), lambda i: (i, 0))

    def whole(a, b, blk=0):
        return pl.BlockSpec((None, a, b), lambda i: (layer, blk, 0), pipeline_mode=pl.Buffered(1))

    return pl.pallas_call(
        functools.partial(_post_kernel, final=final),
        grid=(rows // tm,),
        in_specs=[row(D_MODEL), row(hd), row(hd),
                  pl.BlockSpec((None, tm, D_PLE), lambda i: (layer, i, 0)),
                  whole(hd, D_MODEL, 0), whole(hd, D_MODEL, 1), whole(1, D_MODEL),
                  whole(D_MODEL, D_FF), whole(D_MODEL, D_FF), whole(D_FF, D_MODEL),
                  whole(1, D_MODEL), whole(D_MODEL, D_MODEL), whole(D_PLE, D_MODEL),
                  pl.BlockSpec((1, D_MODEL), lambda i: (0, 0))],
        out_specs=row(D_MODEL),
        out_shape=jax.ShapeDtypeStruct((rows, D_MODEL), F32),
        scratch_shapes=[pltpu.VMEM((tm, D_MODEL), F32)],
        compiler_params=_cparams(1),
        name="post",
    )(x, ma, mb, p_all, wo, wo, gf, wg, wu, wd, gp, wpg, wpp, gfin)


def _gate_vec(pairs, depth):
    v = jnp.zeros((depth, N_GATE), F32)
    for off, val in pairs:
        v = v.at[:, off:off + N_HEADS].set(val.astype(F32))
    return v


def _lane_form(v):
    d = v.shape[0]
    return jnp.broadcast_to(jnp.pad(v, ((0, 0), (0, LANES - N_GATE)))[:, None, :], (d, 8, LANES))


def _row_form(v):
    d = v.shape[0]
    return jnp.broadcast_to(v[:, :, None], (d, N_GATE, LANES))


def kernel(x_prompt, x_sample, p_prompt, p_sample, state_gdn, state_gdn_conv, state_mlstm_C, state_mlstm_n, state_mlstm_m, w_in, conv_w, gdn_a_log, gdn_dt_bias, gdn_norm, mlstm_i_bias, mlstm_f_bias, mlstm_norm, w_out, norm_mix, norm_ffn, w_gate, w_up, w_down, norm_ple, w_ple_gate, w_ple_proj, norm_final):
    depth = w_in.shape[0]
    nb, t, _ = x_prompt.shape
    ns, tdec, _ = x_sample.shape
    hd = N_HEADS * DH
    c = CHUNK
    pad = SLOT - tdec

    xp = x_prompt.reshape(nb * t, D_MODEL)
    xs = jnp.pad(x_sample, ((0, 0), (pad, 0), (0, 0))).reshape(ns * SLOT, D_MODEL)
    ps_all = jnp.pad(p_sample, ((0, 0), (0, 0), (pad, 0), (0, 0))).reshape(depth, ns * SLOT, D_PLE)
    pp_all = p_prompt.reshape(depth, nb * t, D_PLE)
    gfin = norm_final.reshape(1, D_MODEL)

    o = D_CONV + hd
    wt = jnp.swapaxes(w_in, 1, 2).astype(BF16)
    w_r = jnp.concatenate([wt[:, :o], wt[:, o + 2 * N_HEADS:o + 2 * N_HEADS + 4 * hd], wt[:, o:o + 2 * N_HEADS],
                           wt[:, o + 2 * N_HEADS + 4 * hd:],
                           jnp.zeros((depth, LANES - N_GATE, D_MODEL), BF16)], axis=1)
    adds = _gate_vec([(N_HEADS, gdn_dt_bias), (2 * N_HEADS, mlstm_i_bias), (3 * N_HEADS, mlstm_f_bias)], depth)
    alog = _gate_vec([(N_HEADS, gdn_a_log)], depth)
    pa, pl_, ra, rl = _lane_form(adds), _lane_form(alog), _row_form(adds), _row_form(alog)
    g_mix = norm_mix.reshape(depth, 1, D_MODEL)
    cw = conv_w.astype(F32)
    gn = gdn_norm.reshape(depth, 1, DH).astype(F32)
    nrm = mlstm_norm.reshape(depth, 1, hd).astype(F32)
    wts = (w_out.astype(BF16), norm_ffn.reshape(depth, 1, D_MODEL), w_gate.astype(BF16), w_up.astype(BF16),
           w_down.astype(BF16), norm_ple.reshape(depth, 1, D_MODEL), w_ple_gate.astype(BF16),
           w_ple_proj.astype(BF16), gfin)
    cprev = jnp.pad(state_gdn_conv.astype(F32), ((0, 0), (0, 0), (pad - (CONV_W - 1), SLOT - pad), (0, 0)))
    cprev = cprev.reshape(depth, ns * SLOT, D_CONV)
    m0rows = jnp.pad(jnp.repeat(state_mlstm_m.astype(F32), SLOT, axis=1), ((0, 0), (0, 0), (0, LANES - N_HEADS)))
    s0_all, c0_all, n0_all = state_gdn.astype(F32), state_mlstm_C.astype(F32), state_mlstm_n.astype(F32)

    outs_p = [[] for _ in range(5)]
    outs_s = [[] for _ in range(5)]
    for i in range(depth):
        final = i == depth - 1
        proj_p, grow_p = _in_proj(xp, g_mix, w_r, i)
        proj_s, grow_s = _in_proj(xs, g_mix, w_r, i)
        proj_p3 = proj_p.reshape(nb, t, N_PROJ)
        proj_s3 = proj_s.reshape(1, ns * SLOT, N_PROJ)

        prep_p = _gdn_prep(proj_p3, grow_p, None, cw, pa, pl_, ra, rl, i, seq_len=c, valid_lo=0, idt=BF16,
                           tb=PREP_ROWS)
        mixa_p, s_p = _gdn_scan_prompt(*prep_p, proj_p3, gn, i, ts=SCAN_ROWS, bs=SCAN_SEQS)
        prep_s = _gdn_prep(proj_s3, grow_s, cprev, cw, pa, pl_, ra, rl, i, seq_len=SLOT, valid_lo=pad, idt=F32,
                           tb=DECODE_PREP_ROWS)
        mixa_s, s_s = _gdn_scan_decode(*prep_s, proj_s, gn, s0_all, outs_s[0] if final else [], i)

        mixb_p, c_p, n_p, m_p = _mlstm_prompt(proj_p3, grow_p.reshape(nb, t // c, N_GATE, c), pa, ra, nrm, i,
                                              ts=SCAN_ROWS, bs=SCAN_SEQS)
        mixb_s, n_s, m_s, c_s = _mlstm_decode(proj_s, grow_s, pa, ra, nrm, c0_all, n0_all, m0rows,
                                              outs_s[2] if final else [], i, nck=DECODE_MLSTM_CHUNKS)

        xp = _post(xp, mixa_p.reshape(nb * t, hd), mixb_p.reshape(nb * t, hd), pp_all, *wts, i, final=final)
        xs = _post(xs, mixa_s, mixb_s, ps_all, *wts, i, final=final)

        outs_p[0].append(s_p)
        outs_p[1].append(proj_p3[:, t - (CONV_W - 1):, :D_CONV])
        outs_p[2].append(c_p)
        outs_p[3].append(n_p)
        outs_p[4].append(m_p[:, 0, :N_HEADS])
        outs_s[0].append(s_s)
        outs_s[1].append(proj_s.reshape(ns, SLOT, N_PROJ)[:, SLOT - (CONV_W - 1):, :D_CONV])
        outs_s[2].append(c_s)
        outs_s[3].append(n_s)
        outs_s[4].append(m_s.reshape(ns, SLOT, LANES)[:, SLOT - 1, :N_HEADS])

    y_prompt = xp.reshape(nb, t, D_MODEL)
    y_sample = xs.reshape(ns, SLOT, D_MODEL)[:, pad:, :]
    sp = [jnp.stack(a, axis=0) for a in outs_p]
    ss = [outs_s[j][-1] if j in (0, 2) else jnp.stack(outs_s[j], axis=0) for j in range(5)]
    return (y_prompt, y_sample, *sp, *ss)
```
